```python
import jax
import jax.numpy as jnp
from jax import lax
import numpy as np

D_MODEL = 2048
BATCH = 4
SEQ = 2048
DEPTH = 4

CTX_LEN = 256
GRID_W = 64
N_MIXERS = 2
RMS_EPS = 1e-6
POS_BASE = 10000.0
N_MOD = 6

GLA_HEADS = 4
GLA_DK = D_MODEL // (2 * GLA_HEADS)
GLA_DV = D_MODEL // GLA_HEADS
GLA_QK = GLA_HEADS * GLA_DK
GLA_V = GLA_HEADS * GLA_DV
GLA_GATE_RANK = 16
GLA_GATE_NORM = 16.0
GLA_CHUNK = 64
GLA_STATE_COLS = GLA_QK + GLA_V + 2 * GLA_GATE_RANK
GLA_IN = GLA_STATE_COLS + GLA_QK + GLA_V

FNET_GROUPS = 4
FNET_GROUP_DIM = D_MODEL // FNET_GROUPS

MOE_GROUPS = 4
MOE_PER_GROUP = 8
MOE_EXPERTS = MOE_GROUPS * MOE_PER_GROUP
MOE_TOP_K = 2
MOE_D_EXPERT = D_MODEL // 4
MOE_BLOCK = 128

kernel_name = 'hybrid_gla_fnet_hmoe_prefix_dit'


def rmsnorm(x, g):
    xf = x.astype(jnp.float32)
    y = xf * lax.rsqrt(jnp.mean(xf * xf, axis=-1, keepdims=True) + RMS_EPS)
    return (y * g.astype(jnp.float32)).astype(x.dtype)


def modulate(x, shift, scale):
    return x * (1.0 + scale) + shift


def sincos_2d(rows, cols, d, dtype):
    quarter = d // 4
    omega = 1.0 / (POS_BASE ** (jnp.arange(quarter, dtype=jnp.float32) / quarter))

    def axis_emb(n):
        p = jnp.arange(n, dtype=jnp.float32)[:, None] * omega[None, :]
        return jnp.concatenate([jnp.sin(p), jnp.cos(p)], axis=-1)

    er, ec = axis_emb(rows), axis_emb(cols)
    half = 2 * quarter
    pos = jnp.concatenate([jnp.broadcast_to(er[:, None, :], (rows, cols, half)),
                           jnp.broadcast_to(ec[None, :, :], (rows, cols, half))], axis=-1)
    return pos.reshape(rows * cols, 2 * half).astype(dtype)


def _heads(t, dh):
    return t.reshape(t.shape[0], t.shape[1], GLA_HEADS, dh)


def _flip(t):
    return jnp.flip(t, axis=1)


def gla_kvg(proj, wg_f, bg_f, wg_b, bg_b):
    k = _heads(proj[..., :GLA_QK], GLA_DK)
    v = _heads(proj[..., GLA_QK:GLA_QK + GLA_V], GLA_DV)
    z = proj[..., GLA_QK + GLA_V:GLA_STATE_COLS].astype(jnp.float32)
    zf, zb = z[..., :GLA_GATE_RANK], z[..., GLA_GATE_RANK:]
    gf = jax.nn.log_sigmoid(zf @ wg_f.astype(jnp.float32) + bg_f.astype(jnp.float32)) / GLA_GATE_NORM
    gb = jax.nn.log_sigmoid(zb @ wg_b.astype(jnp.float32) + bg_b.astype(jnp.float32)) / GLA_GATE_NORM
    return k, v, _heads(gf, GLA_DK), _heads(gb, GLA_DK)


def gla_chunk_scan(q, k, v, g, s0):
    B, T = q.shape[0], q.shape[1]
    nc = T // GLA_CHUNK

    def to_chunks(t):
        return t.astype(jnp.float32).reshape(B, nc, GLA_CHUNK, GLA_HEADS, t.shape[-1]).transpose(1, 0, 3, 2, 4)

    mask = jnp.tril(jnp.ones((GLA_CHUNK, GLA_CHUNK), dtype=bool))

    def step(S, xs):
        qc, kc, vc, gc = xs
        G = jnp.cumsum(gc, axis=2)
        G_last = G[:, :, -1:, :]
        qe = qc * jnp.exp(G)
        ke = kc * jnp.exp(-G)
        A = jnp.where(mask, jnp.einsum('bhid,bhjd->bhij', qe, ke), 0.0)
        o = jnp.einsum('bhij,bhjv->bhiv', A, vc) + jnp.einsum('bhid,bhdv->bhiv', qe, S)
        S = jnp.exp(G_last[:, :, 0, :, None]) * S + jnp.einsum('bhjd,bhjv->bhdv', kc * jnp.exp(G_last - G), vc)
        return S, o

    S, o = lax.scan(step, s0, (to_chunks(q), to_chunks(k), to_chunks(v), to_chunks(g)))
    o = o.transpose(1, 0, 3, 2, 4).reshape(B, T, GLA_HEADS, GLA_DV)
    return o, S


def gla_final_state(k, v, g):
    gcum = jnp.cumsum(g, axis=1)
    w = jnp.exp(gcum[:, -1:] - gcum)
    return jnp.einsum('bthd,bthv->bhdv', k.astype(jnp.float32) * w, v.astype(jnp.float32))


def gla_mixer(h, w_in, wg_f, bg_f, wg_b, bg_b, onorm_g, w_out, s0f, s0b):
    B, T, _ = h.shape
    proj = h @ w_in
    k, v, gf, gb = gla_kvg(proj, wg_f, bg_f, wg_b, bg_b)
    q = _heads(proj[..., GLA_STATE_COLS:GLA_STATE_COLS + GLA_QK], GLA_DK) * (GLA_DK ** -0.5)
    r = _heads(proj[..., GLA_STATE_COLS + GLA_QK:], GLA_DV)
    of, sf = gla_chunk_scan(q, k, v, gf, s0f)
    ob, sb = gla_chunk_scan(_flip(q), _flip(k), _flip(v), _flip(gb), s0b)
    o = rmsnorm(of + _flip(ob), onorm_g) * jax.nn.silu(r.astype(jnp.float32))
    return o.reshape(B, T, GLA_V).astype(h.dtype) @ w_out, sf, sb


def gla_context_states(h, w_in, wg_f, bg_f, wg_b, bg_b):
    proj = h @ w_in[:, :GLA_STATE_COLS]
    k, v, gf, gb = gla_kvg(proj, wg_f, bg_f, wg_b, bg_b)
    return gla_final_state(k, v, gf), gla_final_state(_flip(k), _flip(v), _flip(gb))


def fourier_mixer(h, w_in, w_out):
    B, T, D = h.shape
    u = (h @ w_in).astype(jnp.float32).reshape(B, T, FNET_GROUPS, FNET_GROUP_DIM)
    y = jnp.fft.fft2(u, axes=(1, 3), norm='ortho').real
    return y.reshape(B, T, D).astype(h.dtype) @ w_out


def hier_moe(h, rw_g, rb_g, rw_e, rb_e, w_gate, w_up, w_down):
    T, D = h.shape
    hf = h.astype(jnp.float32)
    lg = hf @ rw_g.astype(jnp.float32) + rb_g.astype(jnp.float32)
    pg = jax.nn.softmax(lg, axis=-1)
    gidx = jnp.argmax(lg, axis=-1).astype(jnp.int32)
    g_w = jnp.take_along_axis(pg, gidx[:, None], axis=1)
    le_all = jnp.einsum('td,gde->tge', hf, rw_e.astype(jnp.float32)) + rb_e.astype(jnp.float32)
    le = jnp.take_along_axis(le_all, gidx[:, None, None], axis=1)[:, 0]
    top_v, top_i = lax.top_k(le, MOE_TOP_K)
    weights = g_w * jax.nn.softmax(top_v, axis=-1)
    eid = gidx[:, None] * MOE_PER_GROUP + top_i.astype(jnp.int32)

    A = T * MOE_TOP_K
    nb = -(-A // MOE_BLOCK) + MOE_EXPERTS
    e_flat = eid.reshape(-1)
    tok = jnp.repeat(jnp.arange(T, dtype=jnp.int32), MOE_TOP_K)
    w_flat = weights.reshape(-1)
    order = jnp.argsort(e_flat)
    e_s, tok_s, w_s = e_flat[order], tok[order], w_flat[order]
    counts = jnp.zeros((MOE_EXPERTS,), jnp.int32).at[e_flat].add(1)
    padded = ((counts + MOE_BLOCK - 1) // MOE_BLOCK) * MOE_BLOCK
    pad_end = jnp.cumsum(padded)
    pad_start = pad_end - padded
    start = jnp.cumsum(counts) - counts
    dest = pad_start[e_s] + (jnp.arange(A, dtype=jnp.int32) - start[e_s])
    buf_tok = jnp.zeros((nb * MOE_BLOCK,), jnp.int32).at[dest].set(tok_s)
    buf_w = jnp.zeros((nb * MOE_BLOCK,), h.dtype).at[dest].set(w_s.astype(h.dtype))
    block_e = jnp.minimum(jnp.searchsorted(pad_end, jnp.arange(nb, dtype=jnp.int32) * MOE_BLOCK, side='right'),
                          MOE_EXPERTS - 1).astype(jnp.int32)
    xb = h[buf_tok].reshape(nb, MOE_BLOCK, D)

    def expert_block(args):
        xblk, e = args
        return (jax.nn.silu(xblk @ w_gate[e]) * (xblk @ w_up[e])) @ w_down[e]

    yb = lax.map(expert_block, (xb, block_e)).reshape(nb * MOE_BLOCK, D)
    return jnp.zeros_like(h).at[buf_tok].add(yb * buf_w[:, None])


def setup_inputs(seed: int = 0) -> dict:
    key = jax.random.key(seed)
    ks = jax.random.split(key, 25)
    n_gla = (DEPTH + N_MIXERS - 1) // N_MIXERS
    n_fnet = DEPTH // N_MIXERS
    D = D_MODEL

    def nrm(k, shape, scale):
        return scale * jax.random.normal(k, shape, jnp.float32)

    return {
        'x': nrm(ks[0], (BATCH, SEQ, D), 1.0),
        'c': nrm(ks[1], (BATCH, D), 1.0),
        'ctx': nrm(ks[2], (BATCH, CTX_LEN, D), 1.0),
        'c_ctx': nrm(ks[3], (D,), 1.0),
        'ada_w': nrm(ks[4], (DEPTH, D, N_MOD * D), 0.5 * D ** -0.5),
        'ada_b': nrm(ks[5], (DEPTH, N_MOD * D), 0.02),
        'norm1_g': 1.0 + nrm(ks[6], (DEPTH, D), 0.1),
        'norm2_g': 1.0 + nrm(ks[7], (DEPTH, D), 0.1),
        'gla_w_in': nrm(ks[8], (n_gla, D, GLA_IN), D ** -0.5),
        'gla_wg_f': nrm(ks[9], (n_gla, GLA_GATE_RANK, GLA_QK), GLA_GATE_RANK ** -0.5),
        'gla_bg_f': nrm(ks[10], (n_gla, GLA_QK), 0.1),
        'gla_wg_b': nrm(ks[11], (n_gla, GLA_GATE_RANK, GLA_QK), GLA_GATE_RANK ** -0.5),
        'gla_bg_b': nrm(ks[12], (n_gla, GLA_QK), 0.1),
        'gla_onorm_g': 1.0 + nrm(ks[13], (n_gla, GLA_DV), 0.1),
        'gla_w_out': nrm(ks[14], (n_gla, GLA_V, D), GLA_V ** -0.5),
        'fnet_w_in': nrm(ks[15], (n_fnet, D, D), D ** -0.5),
        'fnet_w_out': nrm(ks[16], (n_fnet, D, D), D ** -0.5),
        'moe_rw_group': nrm(ks[17], (DEPTH, D, MOE_GROUPS), D ** -0.5),
        'moe_rb_group': nrm(ks[18], (DEPTH, MOE_GROUPS), 0.01),
        'moe_rw_expert': nrm(ks[19], (DEPTH, MOE_GROUPS, D, MOE_PER_GROUP), D ** -0.5),
        'moe_rb_expert': nrm(ks[20], (DEPTH, MOE_GROUPS, MOE_PER_GROUP), 0.01),
        'moe_w_gate': nrm(ks[21], (DEPTH, MOE_EXPERTS, D, MOE_D_EXPERT), D ** -0.5),
        'moe_w_up': nrm(ks[22], (DEPTH, MOE_EXPERTS, D, MOE_D_EXPERT), D ** -0.5),
        'moe_w_down': nrm(ks[23], (DEPTH, MOE_EXPERTS, MOE_D_EXPERT, D), MOE_D_EXPERT ** -0.5),
        'final_g': 1.0 + nrm(ks[24], (D,), 0.1),
    }


def reference(x, c, ctx, c_ctx, ada_w, ada_b, norm1_g, norm2_g,
              gla_w_in, gla_wg_f, gla_bg_f, gla_wg_b, gla_bg_b, gla_onorm_g, gla_w_out,
              fnet_w_in, fnet_w_out,
              moe_rw_group, moe_rb_group, moe_rw_expert, moe_rb_expert, moe_w_gate, moe_w_up, moe_w_down,
              final_g):
    B, N, D = x.shape
    Lc = ctx.shape[1]
    rows = N // GRID_W
    x = x + sincos_2d(rows, GRID_W, D, x.dtype)[None]
    silu_c = jax.nn.silu(c)
    silu_cc = jax.nn.silu(c_ctx)
    last_reader = ((DEPTH - 1) // N_MIXERS) * N_MIXERS
    s_zero = jnp.zeros((B, GLA_HEADS, GLA_DK, GLA_DV), jnp.float32)

    for i in range(DEPTH):
        kind = i % N_MIXERS
        j = i // N_MIXERS
        ctx_live = i < last_reader
        ctx_needed = i <= last_reader
        mod = (silu_c @ ada_w[i] + ada_b[i])[:, None, :]
        sh1, sc1, gt1, sh2, sc2, gt2 = jnp.split(mod, N_MOD, axis=-1)
        h = modulate(rmsnorm(x, norm1_g[i]), sh1, sc1)
        if ctx_needed:
            mod_c = silu_cc @ ada_w[i] + ada_b[i]
            csh1, csc1, cgt1, csh2, csc2, cgt2 = jnp.split(mod_c, N_MOD, axis=-1)
            hc = modulate(rmsnorm(ctx, norm1_g[i]), csh1, csc1)

        if kind == 0:
            gp = (gla_w_in[j], gla_wg_f[j], gla_bg_f[j], gla_wg_b[j], gla_bg_b[j])
            if ctx_live:
                oc, sf, sb = gla_mixer(hc, *gp, gla_onorm_g[j], gla_w_out[j], s_zero, s_zero)
            else:
                sf, sb = gla_context_states(hc, *gp)
            ol, _, _ = gla_mixer(h, *gp, gla_onorm_g[j], gla_w_out[j], sf, sb)
        else:
            ol = fourier_mixer(h, fnet_w_in[j], fnet_w_out[j])
            if ctx_live:
                oc = fourier_mixer(hc, fnet_w_in[j], fnet_w_out[j])
        x = x + gt1 * ol
        if ctx_live:
            ctx = ctx + cgt1 * oc

        mp = (moe_rw_group[i], moe_rb_group[i], moe_rw_expert[i], moe_rb_expert[i],
              moe_w_gate[i], moe_w_up[i], moe_w_down[i])
        h2 = modulate(rmsnorm(x, norm2_g[i]), sh2, sc2)
        if ctx_live:
            h2c = modulate(rmsnorm(ctx, norm2_g[i]), csh2, csc2)
            y = hier_moe(jnp.concatenate([h2c.reshape(-1, D), h2.reshape(-1, D)], axis=0), *mp)
            ctx = ctx + cgt2 * y[:B * Lc].reshape(B, Lc, D)
            x = x + gt2 * y[B * Lc:].reshape(B, N, D)
        else:
            x = x + gt2 * hier_moe(h2.reshape(-1, D), *mp).reshape(B, N, D)

    return rmsnorm(x, final_g)
```

```python
import functools

import jax
import jax.numpy as jnp
import numpy as np
from jax import lax
from jax.experimental import pallas as pl
from jax.experimental.pallas import tpu as pltpu

F32 = jnp.float32
BF16 = jnp.bfloat16

D = 2048
B = 4
SEQ = 2048
CTX = 256
DEPTH = 4
GRID_W = 64
EPS = 1e-6
POS_BASE = 10000.0
N_MOD = 6

H = 4
DK = 256
DV = 512
QK = H * DK
VV = H * DV
RANK = 16
GATE_NORM = 16.0
CHUNK = 64
STATE_COLS = QK + VV + 2 * RANK
ZPAD = 256
PROJ_N = QK + VV + QK + VV + ZPAD
COL_K, COL_V, COL_Q, COL_R, COL_Z = 0, QK, QK + VV, 2 * QK + VV, 2 * QK + 2 * VV

FG = 4
FD = D // FG

MG = 4
MPG = 8
ME = MG * MPG
MDE = D // 4
TME = 256

LAT = B * SEQ
NCTX = B * CTX
ROWS = LAT + NCTX
SEGS = 8
CTX_SEG = B

VMEM_LIMIT = 56 * 1024 * 1024


def _cp(*sem):
    return pltpu.CompilerParams(dimension_semantics=sem, vmem_limit_bytes=VMEM_LIMIT)


def _seg_of(i, tm):
    return jnp.where(i < LAT // tm, i // (SEQ // tm), CTX_SEG)


def _split2(a):
    hi = a.astype(BF16)
    lo = (a - hi.astype(F32)).astype(BF16)
    return hi, lo


def _dot(a, b):
    return jnp.dot(a, b, preferred_element_type=F32)


def _dot3(a, b_hi, b_lo):
    a_hi, a_lo = _split2(a)
    return _dot(a_hi, b_hi) + _dot(a_lo, b_hi) + _dot(a_hi, b_lo)


def _silu(a):
    return a * jax.nn.sigmoid(a)


def _mod_body(c_ref, w_ref, b_ref, o_ref):
    s = _silu(c_ref[...])
    w_hi, w_lo = _split2(w_ref[0])
    o_ref[0] = _dot3(s, w_hi, w_lo) + b_ref[0]


def _mod_all(cvec, ada_w, ada_b):
    tn = 512
    return pl.pallas_call(
        _mod_body,
        out_shape=jax.ShapeDtypeStruct((DEPTH, SEGS, N_MOD * D), F32),
        grid=(DEPTH, N_MOD * D // tn),
        in_specs=[
            pl.BlockSpec((SEGS, D), lambda l, j: (0, 0)),
            pl.BlockSpec((1, D, tn), lambda l, j: (l, 0, j)),
            pl.BlockSpec((1, 1, tn), lambda l, j: (l, 0, j)),
        ],
        out_specs=pl.BlockSpec((1, SEGS, tn), lambda l, j: (l, 0, j)),
        compiler_params=_cp("arbitrary", "arbitrary"),
        name="adaln_mod",
    )(cvec, ada_w, ada_b.reshape(DEPTH, 1, N_MOD * D))


def _assemble_body(x_ref, pos_ref, ctx_ref, o_ref, *, nlat):
    i = pl.program_id(0)

    @pl.when(i < nlat)
    def _():
        o_ref[...] = x_ref[...] + pos_ref[...]

    @pl.when(i >= nlat)
    def _():
        o_ref[...] = ctx_ref[...]


def _assemble(x2, pos, ctx2):
    tm = 256
    nlat = LAT // tm
    return pl.pallas_call(
        functools.partial(_assemble_body, nlat=nlat),
        out_shape=jax.ShapeDtypeStruct((ROWS, D), F32),
        grid=(ROWS // tm,),
        in_specs=[
            pl.BlockSpec((tm, D), lambda i: (jnp.minimum(i, nlat - 1), 0)),
            pl.BlockSpec((tm, D), lambda i: (i % (SEQ // tm), 0)),
            pl.BlockSpec((tm, D), lambda i: (jnp.maximum(i - nlat, 0), 0)),
        ],
        out_specs=pl.BlockSpec((tm, D), lambda i: (i, 0)),
        compiler_params=_cp("arbitrary"),
        name="assemble_stream",
    )(x2, pos, ctx2)


def _norm_mod_val(x, g, sh, sc):
    y = x * lax.rsqrt(jnp.mean(x * x, axis=-1, keepdims=True) + EPS) * g
    return y * (1.0 + sc) + sh


def _norm_mod_body(x_ref, g_ref, sh_ref, sc_ref, o_ref):
    o_ref[...] = _norm_mod_val(x_ref[...], g_ref[...], sh_ref[...], sc_ref[...]).astype(o_ref.dtype)


def _mod_spec(which, tm):
    return pl.BlockSpec((None, 1, D), lambda i: (_seg_of(i, tm) * N_MOD + which, 0, 0))


def _norm_mod(xs, g, mod3, which_shift, rows):
    tm = 256
    return pl.pallas_call(
        _norm_mod_body,
        out_shape=jax.ShapeDtypeStruct((rows, D), BF16),
        grid=(rows // tm,),
        in_specs=[
            pl.BlockSpec((tm, D), lambda i: (i, 0)),
            pl.BlockSpec((1, D), lambda i: (0, 0)),
            _mod_spec(which_shift, tm),
            _mod_spec(which_shift + 1, tm),
        ],
        out_specs=pl.BlockSpec((tm, D), lambda i: (i, 0)),
        compiler_params=_cp("arbitrary"),
        name="norm_mod",
    )(xs, g.reshape(1, D), mod3, mod3)


def _norm_route_body(x_ref, g_ref, sh_ref, sc_ref, rwh_ref, rwl_ref, rb_ref, h_ref, eid_ref, wt_ref):
    h2 = _norm_mod_val(x_ref[...], g_ref[...], sh_ref[...], sc_ref[...])
    h_ref[...] = h2.astype(BF16)
    lg = _dot3(h2, rwh_ref[...], rwl_ref[...]) + rb_ref[...]
    lane_i = lax.broadcasted_iota(jnp.int32, lg.shape, 1)
    lane = lane_i.astype(F32)
    neg = jnp.float32(-jnp.inf)
    big = jnp.float32(1024.0)

    glog = jnp.where(lane < MG, lg, neg)
    gmax = jnp.max(glog, axis=-1, keepdims=True)
    gidx = jnp.min(jnp.where(glog == gmax, lane, big), axis=-1, keepdims=True)
    g_w = 1.0 / jnp.sum(jnp.exp(glog - gmax), axis=-1, keepdims=True)

    lo = MG + MPG * gidx
    el = jnp.where((lane >= lo) & (lane < lo + MPG), lg, neg)
    m1 = jnp.max(el, axis=-1, keepdims=True)
    i1 = jnp.min(jnp.where(el == m1, lane, big), axis=-1, keepdims=True)
    el2 = jnp.where(lane == i1, neg, el)
    m2 = jnp.max(el2, axis=-1, keepdims=True)
    i2 = jnp.min(jnp.where(el2 == m2, lane, big), axis=-1, keepdims=True)
    e2 = jnp.exp(m2 - m1)
    den = 1.0 / (1.0 + e2)
    w1 = g_w * den
    w2 = g_w * (e2 * den)
    eid_ref[...] = jnp.where(lane_i == 0, i1 - MG, jnp.where(lane_i == 1, i2 - MG, 0.0)).astype(jnp.int32)
    wt_ref[...] = jnp.where(lane_i == 0, w1, jnp.where(lane_i == 1, w2, 0.0))


def _norm_route(xs, g, mod3, rw_hi, rw_lo, rb, rows):
    tm = 256
    return pl.pallas_call(
        _norm_route_body,
        out_shape=(
            jax.ShapeDtypeStruct((rows, D), BF16),
            jax.ShapeDtypeStruct((rows, 128), jnp.int32),
            jax.ShapeDtypeStruct((rows, 128), F32),
        ),
        grid=(rows // tm,),
        in_specs=[
            pl.BlockSpec((tm, D), lambda i: (i, 0)),
            pl.BlockSpec((1, D), lambda i: (0, 0)),
            _mod_spec(3, tm),
            _mod_spec(4, tm),
            pl.BlockSpec((D, 128), lambda i: (0, 0)),
            pl.BlockSpec((D, 128), lambda i: (0, 0)),
            pl.BlockSpec((1, 128), lambda i: (0, 0)),
        ],
        out_specs=(
            pl.BlockSpec((tm, D), lambda i: (i, 0)),
            pl.BlockSpec((tm, 128), lambda i: (i, 0)),
            pl.BlockSpec((tm, 128), lambda i: (i, 0)),
        ),
        compiler_params=_cp("arbitrary"),
        name="norm_route",
    )(xs, g.reshape(1, D), mod3, mod3, rw_hi, rw_lo, rb)


def _mm_body(x_ref, w_ref, o_ref):
    o_ref[...] = _dot(x_ref[...], w_ref[...]).astype(o_ref.dtype)


def _mm(x, w, rows, tm, tn, out_dtype, name):
    k = x.shape[1]
    n = w.shape[1]
    return pl.pallas_call(
        _mm_body,
        out_shape=jax.ShapeDtypeStruct((rows, n), out_dtype),
        grid=(n // tn, rows // tm),
        in_specs=[
            pl.BlockSpec((tm, k), lambda j, i: (i, 0)),
            pl.BlockSpec((k, tn), lambda j, i: (0, j)),
        ],
        out_specs=pl.BlockSpec((tm, tn), lambda j, i: (i, j)),
        compiler_params=_cp("arbitrary", "arbitrary"),
        name=name,
    )(x, w)


def _mm_res_body(x_ref, w_ref, res_ref, gate_ref, o_ref):
    o_ref[...] = res_ref[...] + gate_ref[...] * _dot(x_ref[...], w_ref[...])


def _mm_res(a, w, xs, mod3, which_gate, rows):
    tm, tn = 512, 1024
    k = a.shape[1]
    return pl.pallas_call(
        _mm_res_body,
        out_shape=jax.ShapeDtypeStruct((rows, D), F32),
        grid=(D // tn, rows // tm),
        in_specs=[
            pl.BlockSpec((tm, k), lambda j, i: (i, 0)),
            pl.BlockSpec((k, tn), lambda j, i: (0, j)),
            pl.BlockSpec((tm, tn), lambda j, i: (i, j)),
            pl.BlockSpec((None, 1, tn), lambda j, i: (_seg_of(i, tm) * N_MOD + which_gate, 0, j)),
        ],
        out_specs=pl.BlockSpec((tm, tn), lambda j, i: (i, j)),
        compiler_params=_cp("arbitrary", "arbitrary"),
        name="mm_residual",
    )(a, w, xs, mod3)


def _log_sigmoid(a):
    return jnp.minimum(a, 0.0) - jnp.log1p(jnp.exp(-jnp.abs(a)))


def _gla_body(k_ref, v_ref, q_ref, r_ref, z_ref, wgf_ref, bgf_ref, wgb_ref, bgb_ref, on_ref,
              s0f_ref, s0b_ref, o_ref, sf_ref, sb_ref, gf_scr, gb_scr, o_scr, st_scr, *, T, emit_o):
    nc = T // CHUNK
    z = z_ref[:, :128]
    wf_hi, wf_lo = _split2(wgf_ref[...])
    wb_hi, wb_lo = _split2(wgb_ref[...])
    gf_scr[...] = _log_sigmoid(_dot3(z, wf_hi, wf_lo) + bgf_ref[...]) / GATE_NORM
    gb_scr[...] = _log_sigmoid(_dot3(z, wb_hi, wb_lo) + bgb_ref[...]) / GATE_NORM

    row = lax.broadcasted_iota(jnp.int32, (CHUNK, CHUNK), 0)
    col = lax.broadcasted_iota(jnp.int32, (CHUNK, CHUNK), 1)
    scale = DK ** -0.5

    def scan(g_scr, s0_ref, s_out_ref, causal, emit):
        keep = (row >= col) if causal else (row <= col)
        tri = jnp.where(keep, 1.0, 0.0).astype(BF16)
        edge = CHUNK - 1 if causal else 0
        st_scr[...] = s0_ref[0, 0]

        def step(ci, carry):
            c = ci if causal else nc - 1 - ci
            rows = pl.ds(pl.multiple_of(c * CHUNK, CHUNK), CHUNK)
            g = g_scr[rows, :]
            g1 = g.astype(BF16)
            rem = g - g1.astype(F32)
            g2 = rem.astype(BF16)
            g3 = (rem - g2.astype(F32)).astype(BF16)
            G = _dot(tri, g1) + _dot(tri, g2) + _dot(tri, g3)
            G_edge = G[edge:edge + 1, :]
            kc = k_ref[rows, :]
            vb = v_ref[rows, :].astype(BF16)
            qe = (q_ref[rows, :] * scale * jnp.exp(G)).astype(BF16)
            ke = (kc * jnp.exp(-G)).astype(BF16)
            kd = (kc * jnp.exp(G_edge - G)).astype(BF16)
            st = st_scr[...]
            if emit:
                a = lax.dot_general(qe, ke, (((1,), (1,)), ((), ())), preferred_element_type=F32)
                a = jnp.where(keep, a, 0.0).astype(BF16)
                o = _dot(a, vb) + lax.dot_general(qe, st.astype(BF16), (((1,), (1,)), ((), ())),
                                                  preferred_element_type=F32)
                if causal:
                    o_scr[rows, :] = o
                else:
                    tot = o_scr[rows, :] + o
                    y = tot * lax.rsqrt(jnp.mean(tot * tot, axis=-1, keepdims=True) + EPS) * on_ref[...]
                    o_ref[rows, :] = (y * _silu(r_ref[rows, :])).astype(o_ref.dtype)
            upd = lax.dot_general(vb, kd, (((0,), (0,)), ((), ())), preferred_element_type=F32)
            st_scr[...] = st * jnp.exp(G_edge) + upd
            return carry

        lax.fori_loop(0, nc, step, 0)
        s_out_ref[0, 0] = st_scr[...]

    scan(gf_scr, s0f_ref, sf_ref, True, emit_o)
    scan(gb_scr, s0b_ref, sb_ref, False, emit_o)
    if not emit_o:
        o_ref[...] = jnp.zeros_like(o_ref)


def _gla(proj, wgf, bgf, wgb, bgb, onorm, s0f, s0b, T, row_off, emit_o):
    rb = row_off // T
    st_spec = pl.BlockSpec((1, 1, DV, DK), lambda b, h: (b, h, 0, 0))
    return pl.pallas_call(
        functools.partial(_gla_body, T=T, emit_o=emit_o),
        out_shape=(
            jax.ShapeDtypeStruct((B * T, VV), BF16),
            jax.ShapeDtypeStruct((B, H, DV, DK), F32),
            jax.ShapeDtypeStruct((B, H, DV, DK), F32),
        ),
        grid=(B, H),
        in_specs=[
            pl.BlockSpec((T, DK), lambda b, h: (rb + b, COL_K // DK + h)),
            pl.BlockSpec((T, DV), lambda b, h: (rb + b, COL_V // DV + h)),
            pl.BlockSpec((T, DK), lambda b, h: (rb + b, COL_Q // DK + h)),
            pl.BlockSpec((T, DV), lambda b, h: (rb + b, COL_R // DV + h)),
            pl.BlockSpec((T, ZPAD), lambda b, h: (rb + b, COL_Z // ZPAD)),
            pl.BlockSpec((128, DK), lambda b, h: (0, h)),
            pl.BlockSpec((1, DK), lambda b, h: (0, h)),
            pl.BlockSpec((128, DK), lambda b, h: (0, h)),
            pl.BlockSpec((1, DK), lambda b, h: (0, h)),
            pl.BlockSpec((1, DV), lambda b, h: (0, 0)),
            st_spec,
            st_spec,
        ],
        out_specs=(
            pl.BlockSpec((T, DV), lambda b, h: (b, h)),
            st_spec,
            st_spec,
        ),
        scratch_shapes=[
            pltpu.VMEM((T, DK), F32),
            pltpu.VMEM((T, DK), F32),
            pltpu.VMEM((T, DV), F32),
            pltpu.VMEM((DV, DK), F32),
        ],
        compiler_params=_cp("arbitrary", "arbitrary"),
        name="gla_scan",
    )(proj, proj, proj, proj, proj, wgf, bgf, wgb, bgb, onorm, s0f, s0b)


def _fnet_in_body(h_ref, w_ref, cs_ref, a_ref, b_ref):
    u = _dot(h_ref[...], w_ref[...]).astype(BF16)
    for g in range(FG):
        ab = _dot(u[:, g * FD:(g + 1) * FD], cs_ref[...])
        a_ref[:, g * FD:(g + 1) * FD] = ab[:, :FD].astype(BF16)
        b_ref[:, g * FD:(g + 1) * FD] = ab[:, FD:].astype(BF16)


def _fnet_in(h, w, cs, rows):
    tm = 512
    return pl.pallas_call(
        _fnet_in_body,
        out_shape=(jax.ShapeDtypeStruct((rows, D), BF16), jax.ShapeDtypeStruct((rows, D), BF16)),
        grid=(rows // tm,),
        in_specs=[
            pl.BlockSpec((tm, D), lambda i: (i, 0)),
            pl.BlockSpec((D, D), lambda i: (0, 0)),
            pl.BlockSpec((FD, 2 * FD), lambda i: (0, 0)),
        ],
        out_specs=(pl.BlockSpec((tm, D), lambda i: (i, 0)), pl.BlockSpec((tm, D), lambda i: (i, 0))),
        compiler_params=_cp("arbitrary"),
        name="fnet_in",
    )(h, w, cs)


def _fnet_time_body(ct_ref, st_ref, a_ref, b_ref, o_ref):
    o_ref[...] = (_dot(ct_ref[...], a_ref[...]) + _dot(st_ref[...], b_ref[...])).astype(o_ref.dtype)


def _fnet_time(ct, st, a, b, T, row_off):
    tm = min(T, 512)
    tn = 1024
    rb = row_off // T
    return pl.pallas_call(
        _fnet_time_body,
        out_shape=jax.ShapeDtypeStruct((B * T, D), BF16),
        grid=(B, D // tn, T // tm),
        in_specs=[
            pl.BlockSpec((tm, T), lambda s, j, i: (i, 0)),
            pl.BlockSpec((tm, T), lambda s, j, i: (i, 0)),
            pl.BlockSpec((T, tn), lambda s, j, i: (rb + s, j)),
            pl.BlockSpec((T, tn), lambda s, j, i: (rb + s, j)),
        ],
        out_specs=pl.BlockSpec((tm, tn), lambda s, j, i: (s * (T // tm) + i, j)),
        compiler_params=_cp("arbitrary", "arbitrary", "arbitrary"),
        name="fnet_time",
    )(ct, st, a, b)


def _dft_mats(n, scale):
    idx = np.arange(n, dtype=np.int64)
    ang = 2.0 * np.pi * ((idx[:, None] * idx[None, :]) % n).astype(np.float64) / n
    return np.cos(ang) * scale, np.sin(ang) * scale


def _moe_body(be_ref, nu_ref, x_ref, wg_ref, wu_ref, wd_ref, o_ref, wgb, wub, wdb):
    i = pl.program_id(0)
    e_now = be_ref[i]
    e_prev = be_ref[jnp.maximum(i - 1, 0)]

    @pl.when((i == 0) | (e_now != e_prev))
    def _():
        wgb[...] = wg_ref[0].astype(BF16)
        wub[...] = wu_ref[0].astype(BF16)
        wdb[...] = wd_ref[0].astype(BF16)

    @pl.when(i < nu_ref[0])
    def _():
        x = x_ref[...]
        hmid = _silu(_dot(x, wgb[...])) * _dot(x, wub[...])
        o_ref[...] = _dot(hmid.astype(BF16), wdb[...])

    @pl.when(i >= nu_ref[0])
    def _():
        o_ref[...] = jnp.zeros_like(o_ref)


def _moe_experts(block_e, n_used, xb, w_gate, w_up, w_down):
    nb = xb.shape[0] // TME
    return pl.pallas_call(
        _moe_body,
        out_shape=jax.ShapeDtypeStruct((nb * TME, D), F32),
        grid_spec=pltpu.PrefetchScalarGridSpec(
            num_scalar_prefetch=2,
            grid=(nb,),
            in_specs=[
                pl.BlockSpec((TME, D), lambda i, be, nu: (i, 0)),
                pl.BlockSpec((1, D, MDE), lambda i, be, nu: (be[i], 0, 0)),
                pl.BlockSpec((1, D, MDE), lambda i, be, nu: (be[i], 0, 0)),
                pl.BlockSpec((1, MDE, D), lambda i, be, nu: (be[i], 0, 0)),
            ],
            out_specs=pl.BlockSpec((TME, D), lambda i, be, nu: (i, 0)),
            scratch_shapes=[
                pltpu.VMEM((D, MDE), BF16),
                pltpu.VMEM((D, MDE), BF16),
                pltpu.VMEM((MDE, D), BF16),
            ],
        ),
        compiler_params=_cp("arbitrary"),
        name="moe_experts",
    )(block_e, n_used, xb, w_gate, w_up, w_down)


def _combine_body(x_ref, gate_ref, y0_ref, y1_ref, wt_ref, o_ref):
    wt = wt_ref[...]
    y = wt[:, 0:1] * y0_ref[...] + wt[:, 1:2] * y1_ref[...]
    o_ref[...] = x_ref[...] + gate_ref[...] * y


def _combine(xs, mod3, y0, y1, wt, rows):
    tm = 256
    return pl.pallas_call(
        _combine_body,
        out_shape=jax.ShapeDtypeStruct((rows, D), F32),
        grid=(rows // tm,),
        in_specs=[
            pl.BlockSpec((tm, D), lambda i: (i, 0)),
            _mod_spec(5, tm),
            pl.BlockSpec((tm, D), lambda i: (i, 0)),
            pl.BlockSpec((tm, D), lambda i: (i, 0)),
            pl.BlockSpec((tm, 128), lambda i: (i, 0)),
        ],
        out_specs=pl.BlockSpec((tm, D), lambda i: (i, 0)),
        compiler_params=_cp("arbitrary"),
        name="moe_combine",
    )(xs, mod3, y0, y1, wt)


def _route_plan(eid, rows):
    a = rows * 2
    nb = -(-a // TME) + ME
    e_flat = eid.reshape(-1)
    onehot = (e_flat[:, None] == jnp.arange(ME, dtype=jnp.int32)[None, :]).astype(jnp.int32)
    csum = jnp.cumsum(onehot, axis=0)
    counts = csum[-1]
    rank = jnp.take_along_axis(csum, e_flat[:, None], axis=1)[:, 0] - 1
    padded = ((counts + TME - 1) // TME) * TME
    pad_end = jnp.cumsum(padded)
    pad_start = pad_end - padded
    dest = pad_start[e_flat] + rank
    tok = jnp.arange(a, dtype=jnp.int32) // 2
    buf_tok = jnp.zeros((nb * TME,), jnp.int32).at[dest].set(tok)
    n_used = (pad_end[-1] // TME).astype(jnp.int32)
    blk = jnp.minimum(jnp.arange(nb, dtype=jnp.int32), n_used - 1)
    block_e = jnp.minimum(jnp.searchsorted(pad_end, blk * TME, side='right'), ME - 1).astype(jnp.int32)
    return dest.reshape(rows, 2), buf_tok, block_e, n_used.reshape(1)


def _final_body(x_ref, g_ref, o_ref):
    x = x_ref[...]
    o_ref[...] = x * lax.rsqrt(jnp.mean(x * x, axis=-1, keepdims=True) + EPS) * g_ref[...]


def _final_norm(xs, g):
    tm = 256
    return pl.pallas_call(
        _final_body,
        out_shape=jax.ShapeDtypeStruct((LAT, D), F32),
        grid=(LAT // tm,),
        in_specs=[pl.BlockSpec((tm, D), lambda i: (i, 0)), pl.BlockSpec((1, D), lambda i: (0, 0))],
        out_specs=pl.BlockSpec((tm, D), lambda i: (i, 0)),
        compiler_params=_cp("arbitrary"),
        name="final_norm",
    )(xs, g.reshape(1, D))


def _sincos_2d(rows, cols, d):
    quarter = d // 4
    omega = 1.0 / (POS_BASE ** (jnp.arange(quarter, dtype=F32) / quarter))

    def axis_emb(n):
        p = jnp.arange(n, dtype=F32)[:, None] * omega[None, :]
        return jnp.concatenate([jnp.sin(p), jnp.cos(p)], axis=-1)

    er, ec = axis_emb(rows), axis_emb(cols)
    half = 2 * quarter
    pos = jnp.concatenate([jnp.broadcast_to(er[:, None, :], (rows, cols, half)),
                           jnp.broadcast_to(ec[None, :, :], (rows, cols, half))], axis=-1)
    return pos.reshape(rows * cols, 2 * half)


def _gla_weights(w_in, wg_f, wg_b):
    z = jnp.zeros((D, ZPAD - 2 * RANK), F32)
    w_cat = jnp.concatenate([
        w_in[:, :QK], w_in[:, QK:QK + VV],
        w_in[:, STATE_COLS:STATE_COLS + QK], w_in[:, STATE_COLS + QK:],
        w_in[:, QK + VV:STATE_COLS], z], axis=1).astype(BF16)
    wgf = jnp.zeros((128, QK), F32).at[:RANK].set(wg_f)
    wgb = jnp.zeros((128, QK), F32).at[RANK:2 * RANK].set(wg_b)
    return w_cat, wgf, wgb


def kernel(x, c, ctx, c_ctx, ada_w, ada_b, norm1_g, norm2_g, gla_w_in, gla_wg_f, gla_bg_f, gla_wg_b, gla_bg_b,
           gla_onorm_g, gla_w_out, fnet_w_in, fnet_w_out, moe_rw_group, moe_rb_group, moe_rw_expert,
           moe_rb_expert, moe_w_gate, moe_w_up, moe_w_down, final_g):
    cvec = jnp.zeros((SEGS, D), F32).at[:B].set(c).at[CTX_SEG].set(c_ctx)
    mods = _mod_all(cvec, ada_w, ada_b)
    pos = _sincos_2d(SEQ // GRID_W, GRID_W, D)
    xs = _assemble(x.reshape(LAT, D), pos, ctx.reshape(NCTX, D))

    cc, sc = _dft_mats(FD, 1.0)
    cs = jnp.asarray(np.concatenate([cc, sc], axis=1), BF16)
    ct_l, st_l = _dft_mats(SEQ, (SEQ * FD) ** -0.5)
    ct_c, st_c = _dft_mats(CTX, (CTX * FD) ** -0.5)
    ct_l, st_l = jnp.asarray(ct_l, BF16), jnp.asarray(-st_l, BF16)
    ct_c, st_c = jnp.asarray(ct_c, BF16), jnp.asarray(-st_c, BF16)
    s_zero = jnp.zeros((B, H, DV, DK), F32)
    last_reader = ((DEPTH - 1) // 2) * 2

    for i in range(DEPTH):
        kind, j = i % 2, i // 2
        ctx_live = i < last_reader
        ctx_needed = i <= last_reader
        mod3 = mods[i].reshape(SEGS * N_MOD, 1, D)
        rows_in = ROWS if ctx_needed else LAT
        rows_out = ROWS if ctx_live else LAT
        h = _norm_mod(xs, norm1_g[i], mod3, 0, rows_in)

        if kind == 0:
            w_cat, wgf, wgb = _gla_weights(gla_w_in[j], gla_wg_f[j], gla_wg_b[j])
            bgf, bgb = gla_bg_f[j].reshape(1, QK), gla_bg_b[j].reshape(1, QK)
            onorm = gla_onorm_g[j].reshape(1, DV)
            proj = _mm(h, w_cat, rows_in, 512, 1280, F32, "gla_proj")
            o_c, sf, sb = _gla(proj, wgf, bgf, wgb, bgb, onorm, s_zero, s_zero, CTX, LAT, ctx_live)
            o_l, _, _ = _gla(proj, wgf, bgf, wgb, bgb, onorm, sf, sb, SEQ, 0, True)
            mix = jnp.concatenate([o_l, o_c], axis=0) if ctx_live else o_l
            xs = _mm_res(mix, gla_w_out[j].astype(BF16), xs, mod3, 2, rows_out)
        else:
            a, b = _fnet_in(h, fnet_w_in[j].astype(BF16), cs, rows_in)
            y = _fnet_time(ct_l, st_l, a, b, SEQ, 0)
            if ctx_live:
                y = jnp.concatenate([y, _fnet_time(ct_c, st_c, a, b, CTX, LAT)], axis=0)
            xs = _mm_res(y, fnet_w_out[j].astype(BF16), xs, mod3, 2, rows_out)

        rw = jnp.zeros((D, 128), F32).at[:, :MG].set(moe_rw_group[i])
        rw = rw.at[:, MG:MG + ME].set(jnp.transpose(moe_rw_expert[i], (1, 0, 2)).reshape(D, ME))
        rb = jnp.zeros((1, 128), F32).at[0, :MG].set(moe_rb_group[i]).at[0, MG:MG + ME].set(
            moe_rb_expert[i].reshape(ME))
        rw_hi = rw.astype(BF16)
        rw_lo = (rw - rw_hi.astype(F32)).astype(BF16)
        h2, eid, wt = _norm_route(xs, norm2_g[i], mod3, rw_hi, rw_lo, rb, rows_out)
        dest, buf_tok, block_e, n_used = _route_plan(eid[:, :2], rows_out)
        xb = jnp.take(h2, buf_tok, axis=0)
        yb = _moe_experts(block_e, n_used, xb, moe_w_gate[i], moe_w_up[i], moe_w_down[i])
        y0 = jnp.take(yb, dest[:, 0], axis=0)
        y1 = jnp.take(yb, dest[:, 1], axis=0)
        xs = _combine(xs, mod3, y0, y1, wt, rows_out)

    return _final_norm(xs, final_g).reshape(B, SEQ, D)
```

```python
import functools

import jax
import jax.numpy as jnp
import numpy as np
from jax import lax
from jax.experimental import pallas as pl
from jax.experimental.pallas import tpu as pltpu

F32 = jnp.float32
BF16 = jnp.bfloat16

D = 2048
B = 4
SEQ = 2048
CTX = 256
DEPTH = 4
GRID_W = 64
EPS = 1e-6
POS_BASE = 10000.0
N_MOD = 6

H = 4
DK = 256
DV = 512
QK = H * DK
VV = H * DV
RANK = 16
GATE_NORM = 16.0
CHUNK = 64
STATE_COLS = QK + VV + 2 * RANK
ZPAD = 256
PROJ_N = QK + VV + QK + VV + ZPAD
COL_K, COL_V, COL_Q, COL_R, COL_Z = 0, QK, QK + VV, 2 * QK + VV, 2 * QK + 2 * VV

FG = 4
FD = D // FG

MG = 4
MPG = 8
ME = MG * MPG
MDE = D // 4
TME = 256

LAT = B * SEQ
NCTX = B * CTX
ROWS = LAT + NCTX
SEGS = 8
CTX_SEG = B

VMEM_LIMIT = 56 * 1024 * 1024


def _cp(*sem):
    return pltpu.CompilerParams(dimension_semantics=sem, vmem_limit_bytes=VMEM_LIMIT)


def _seg_of(i, tm):
    return jnp.where(i < LAT // tm, i // (SEQ // tm), CTX_SEG)


def _split2(a):
    hi = a.astype(BF16)
    lo = (a - hi.astype(F32)).astype(BF16)
    return hi, lo


def _dot(a, b):
    return jnp.dot(a, b, preferred_element_type=F32)


def _dot3(a, b_hi, b_lo):
    a_hi, a_lo = _split2(a)
    return _dot(a_hi, b_hi) + _dot(a_lo, b_hi) + _dot(a_hi, b_lo)


def _silu(a):
    return a * jax.nn.sigmoid(a)


def _mod_body(c_ref, w_ref, b_ref, o_ref):
    s = _silu(c_ref[...])
    w_hi, w_lo = _split2(w_ref[0])
    o_ref[0] = _dot3(s, w_hi, w_lo) + b_ref[0]


def _mod_all(cvec, ada_w, ada_b):
    tn = 512
    return pl.pallas_call(
        _mod_body,
        out_shape=jax.ShapeDtypeStruct((DEPTH, SEGS, N_MOD * D), F32),
        grid=(DEPTH, N_MOD * D // tn),
        in_specs=[
            pl.BlockSpec((SEGS, D), lambda l, j: (0, 0)),
            pl.BlockSpec((1, D, tn), lambda l, j: (l, 0, j)),
            pl.BlockSpec((1, 1, tn), lambda l, j: (l, 0, j)),
        ],
        out_specs=pl.BlockSpec((1, SEGS, tn), lambda l, j: (l, 0, j)),
        compiler_params=_cp("arbitrary", "arbitrary"),
        name="adaln_mod",
    )(cvec, ada_w, ada_b.reshape(DEPTH, 1, N_MOD * D))


def _assemble_body(x_ref, pos_ref, ctx_ref, o_ref, *, nlat):
    i = pl.program_id(0)

    @pl.when(i < nlat)
    def _():
        o_ref[...] = x_ref[...] + pos_ref[...]

    @pl.when(i >= nlat)
    def _():
        o_ref[...] = ctx_ref[...]


def _assemble(x2, pos, ctx2):
    tm = 256
    nlat = LAT // tm
    return pl.pallas_call(
        functools.partial(_assemble_body, nlat=nlat),
        out_shape=jax.ShapeDtypeStruct((ROWS, D), F32),
        grid=(ROWS // tm,),
        in_specs=[
            pl.BlockSpec((tm, D), lambda i: (jnp.minimum(i, nlat - 1), 0)),
            pl.BlockSpec((tm, D), lambda i: (i % (SEQ // tm), 0)),
            pl.BlockSpec((tm, D), lambda i: (jnp.maximum(i - nlat, 0), 0)),
        ],
        out_specs=pl.BlockSpec((tm, D), lambda i: (i, 0)),
        compiler_params=_cp("arbitrary"),
        name="assemble_stream",
    )(x2, pos, ctx2)


def _norm_mod_val(x, g, sh, sc):
    y = x * lax.rsqrt(jnp.mean(x * x, axis=-1, keepdims=True) + EPS) * g
    return y * (1.0 + sc) + sh


def _norm_mod_body(x_ref, g_ref, sh_ref, sc_ref, o_ref):
    o_ref[...] = _norm_mod_val(x_ref[...], g_ref[...], sh_ref[...], sc_ref[...]).astype(o_ref.dtype)


def _mod_spec(which, tm):
    return pl.BlockSpec((None, 1, D), lambda i: (_seg_of(i, tm) * N_MOD + which, 0, 0))


def _norm_mod(xs, g, mod3, which_shift, rows):
    tm = 256
    return pl.pallas_call(
        _norm_mod_body,
        out_shape=jax.ShapeDtypeStruct((rows, D), BF16),
        grid=(rows // tm,),
        in_specs=[
            pl.BlockSpec((tm, D), lambda i: (i, 0)),
            pl.BlockSpec((1, D), lambda i: (0, 0)),
            _mod_spec(which_shift, tm),
            _mod_spec(which_shift + 1, tm),
        ],
        out_specs=pl.BlockSpec((tm, D), lambda i: (i, 0)),
        compiler_params=_cp("arbitrary"),
        name="norm_mod",
    )(xs, g.reshape(1, D), mod3, mod3)


def _norm_route_body(x_ref, g_ref, sh_ref, sc_ref, rwh_ref, rwl_ref, rb_ref, h_ref, eid_ref, wt_ref):
    h2 = _norm_mod_val(x_ref[...], g_ref[...], sh_ref[...], sc_ref[...])
    h_ref[...] = h2
    lg = _dot3(h2, rwh_ref[...], rwl_ref[...]) + rb_ref[...]
    lane_i = lax.broadcasted_iota(jnp.int32, lg.shape, 1)
    lane = lane_i.astype(F32)
    neg = jnp.float32(-jnp.inf)
    big = jnp.float32(1024.0)

    glog = jnp.where(lane < MG, lg, neg)
    gmax = jnp.max(glog, axis=-1, keepdims=True)
    gidx = jnp.min(jnp.where(glog == gmax, lane, big), axis=-1, keepdims=True)
    g_w = 1.0 / jnp.sum(jnp.exp(glog - gmax), axis=-1, keepdims=True)

    lo = MG + MPG * gidx
    el = jnp.where((lane >= lo) & (lane < lo + MPG), lg, neg)
    m1 = jnp.max(el, axis=-1, keepdims=True)
    i1 = jnp.min(jnp.where(el == m1, lane, big), axis=-1, keepdims=True)
    el2 = jnp.where(lane == i1, neg, el)
    m2 = jnp.max(el2, axis=-1, keepdims=True)
    i2 = jnp.min(jnp.where(el2 == m2, lane, big), axis=-1, keepdims=True)
    e2 = jnp.exp(m2 - m1)
    den = 1.0 / (1.0 + e2)
    w1 = g_w * den
    w2 = g_w * (e2 * den)
    eid_ref[...] = jnp.where(lane_i == 0, i1 - MG, jnp.where(lane_i == 1, i2 - MG, 0.0)).astype(jnp.int32)
    wt_ref[...] = jnp.where(lane_i == 0, w1, jnp.where(lane_i == 1, w2, 0.0))


def _norm_route(xs, g, mod3, rw_hi, rw_lo, rb, rows):
    tm = 256
    return pl.pallas_call(
        _norm_route_body,
        out_shape=(
            jax.ShapeDtypeStruct((rows, D), F32),
            jax.ShapeDtypeStruct((rows, 128), jnp.int32),
            jax.ShapeDtypeStruct((rows, 128), F32),
        ),
        grid=(rows // tm,),
        in_specs=[
            pl.BlockSpec((tm, D), lambda i: (i, 0)),
            pl.BlockSpec((1, D), lambda i: (0, 0)),
            _mod_spec(3, tm),
            _mod_spec(4, tm),
            pl.BlockSpec((D, 128), lambda i: (0, 0)),
            pl.BlockSpec((D, 128), lambda i: (0, 0)),
            pl.BlockSpec((1, 128), lambda i: (0, 0)),
        ],
        out_specs=(
            pl.BlockSpec((tm, D), lambda i: (i, 0)),
            pl.BlockSpec((tm, 128), lambda i: (i, 0)),
            pl.BlockSpec((tm, 128), lambda i: (i, 0)),
        ),
        compiler_params=_cp("arbitrary"),
        name="norm_route",
    )(xs, g.reshape(1, D), mod3, mod3, rw_hi, rw_lo, rb)


def _mm_body(x_ref, w_ref, o_ref):
    o_ref[...] = _dot(x_ref[...], w_ref[...]).astype(o_ref.dtype)


def _mm(x, w, rows, tm, tn, out_dtype, name):
    k = x.shape[1]
    n = w.shape[1]
    return pl.pallas_call(
        _mm_body,
        out_shape=jax.ShapeDtypeStruct((rows, n), out_dtype),
        grid=(n // tn, rows // tm),
        in_specs=[
            pl.BlockSpec((tm, k), lambda j, i: (i, 0)),
            pl.BlockSpec((k, tn), lambda j, i: (0, j)),
        ],
        out_specs=pl.BlockSpec((tm, tn), lambda j, i: (i, j)),
        compiler_params=_cp("arbitrary", "arbitrary"),
        name=name,
    )(x, w)


def _mm_res_body(x_ref, w_ref, res_ref, gate_ref, o_ref):
    o_ref[...] = res_ref[...] + gate_ref[...] * _dot(x_ref[...], w_ref[...])


def _mm_res(a, w, xs, mod3, which_gate, rows):
    tm, tn = 512, 1024
    k = a.shape[1]
    return pl.pallas_call(
        _mm_res_body,
        out_shape=jax.ShapeDtypeStruct((rows, D), F32),
        grid=(D // tn, rows // tm),
        in_specs=[
            pl.BlockSpec((tm, k), lambda j, i: (i, 0)),
            pl.BlockSpec((k, tn), lambda j, i: (0, j)),
            pl.BlockSpec((tm, tn), lambda j, i: (i, j)),
            pl.BlockSpec((None, 1, tn), lambda j, i: (_seg_of(i, tm) * N_MOD + which_gate, 0, j)),
        ],
        out_specs=pl.BlockSpec((tm, tn), lambda j, i: (i, j)),
        compiler_params=_cp("arbitrary", "arbitrary"),
        name="mm_residual",
    )(a, w, xs, mod3)


def _log_sigmoid(a):
    return jnp.minimum(a, 0.0) - jnp.log1p(jnp.exp(-jnp.abs(a)))


def _gla_body(k_ref, v_ref, q_ref, r_ref, z_ref, wgf_ref, bgf_ref, wgb_ref, bgb_ref, on_ref,
              s0f_ref, s0b_ref, o_ref, sf_ref, sb_ref, gf_scr, gb_scr, o_scr, st_scr, *, T, emit_o):
    nc = T // CHUNK
    z = z_ref[:, :128]
    wf_hi, wf_lo = _split2(wgf_ref[...])
    wb_hi, wb_lo = _split2(wgb_ref[...])
    gf_scr[...] = _log_sigmoid(_dot3(z, wf_hi, wf_lo) + bgf_ref[...]) / GATE_NORM
    gb_scr[...] = _log_sigmoid(_dot3(z, wb_hi, wb_lo) + bgb_ref[...]) / GATE_NORM

    row = lax.broadcasted_iota(jnp.int32, (CHUNK, CHUNK), 0)
    col = lax.broadcasted_iota(jnp.int32, (CHUNK, CHUNK), 1)
    scale = DK ** -0.5

    def scan(g_scr, s0_ref, s_out_ref, causal, emit):
        keep = (row >= col) if causal else (row <= col)
        tri = jnp.where(keep, 1.0, 0.0).astype(BF16)
        edge = CHUNK - 1 if causal else 0
        st_scr[...] = s0_ref[0, 0]

        def step(ci, carry):
            c = ci if causal else nc - 1 - ci
            rows = pl.ds(pl.multiple_of(c * CHUNK, CHUNK), CHUNK)
            g = g_scr[rows, :]
            g1 = g.astype(BF16)
            rem = g - g1.astype(F32)
            g2 = rem.astype(BF16)
            g3 = (rem - g2.astype(F32)).astype(BF16)
            G = _dot(tri, g1) + _dot(tri, g2) + _dot(tri, g3)
            G_edge = G[edge:edge + 1, :]
            kc = k_ref[rows, :]
            vb = v_ref[rows, :].astype(BF16)
            qe = (q_ref[rows, :] * scale * jnp.exp(G)).astype(BF16)
            ke = (kc * jnp.exp(-G)).astype(BF16)
            kd = (kc * jnp.exp(G_edge - G)).astype(BF16)
            st = st_scr[...]
            if emit:
                a = lax.dot_general(qe, ke, (((1,), (1,)), ((), ())), preferred_element_type=F32)
                a = jnp.where(keep, a, 0.0).astype(BF16)
                o = _dot(a, vb) + lax.dot_general(qe, st.astype(BF16), (((1,), (1,)), ((), ())),
                                                  preferred_element_type=F32)
                if causal:
                    o_scr[rows, :] = o
                else:
                    tot = o_scr[rows, :] + o
                    y = tot * lax.rsqrt(jnp.mean(tot * tot, axis=-1, keepdims=True) + EPS) * on_ref[...]
                    o_ref[rows, :] = (y * _silu(r_ref[rows, :])).astype(o_ref.dtype)
            upd = lax.dot_general(vb, kd, (((0,), (0,)), ((), ())), preferred_element_type=F32)
            st_scr[...] = st * jnp.exp(G_edge) + upd
            return carry

        lax.fori_loop(0, nc, step, 0)
        s_out_ref[0, 0] = st_scr[...]

    scan(gf_scr, s0f_ref, sf_ref, True, emit_o)
    scan(gb_scr, s0b_ref, sb_ref, False, emit_o)
    if not emit_o:
        o_ref[...] = jnp.zeros_like(o_ref)


def _gla(proj, wgf, bgf, wgb, bgb, onorm, s0f, s0b, T, row_off, emit_o):
    rb = row_off // T
    st_spec = pl.BlockSpec((1, 1, DV, DK), lambda b, h: (b, h, 0, 0))
    return pl.pallas_call(
        functools.partial(_gla_body, T=T, emit_o=emit_o),
        out_shape=(
            jax.ShapeDtypeStruct((B * T, VV), BF16),
            jax.ShapeDtypeStruct((B, H, DV, DK), F32),
            jax.ShapeDtypeStruct((B, H, DV, DK), F32),
        ),
        grid=(B, H),
        in_specs=[
            pl.BlockSpec((T, DK), lambda b, h: (rb + b, COL_K // DK + h)),
            pl.BlockSpec((T, DV), lambda b, h: (rb + b, COL_V // DV + h)),
            pl.BlockSpec((T, DK), lambda b, h: (rb + b, COL_Q // DK + h)),
            pl.BlockSpec((T, DV), lambda b, h: (rb + b, COL_R // DV + h)),
            pl.BlockSpec((T, ZPAD), lambda b, h: (rb + b, COL_Z // ZPAD)),
            pl.BlockSpec((128, DK), lambda b, h: (0, h)),
            pl.BlockSpec((1, DK), lambda b, h: (0, h)),
            pl.BlockSpec((128, DK), lambda b, h: (0, h)),
            pl.BlockSpec((1, DK), lambda b, h: (0, h)),
            pl.BlockSpec((1, DV), lambda b, h: (0, 0)),
            st_spec,
            st_spec,
        ],
        out_specs=(
            pl.BlockSpec((T, DV), lambda b, h: (b, h)),
            st_spec,
            st_spec,
        ),
        scratch_shapes=[
            pltpu.VMEM((T, DK), F32),
            pltpu.VMEM((T, DK), F32),
            pltpu.VMEM((T, DV), F32),
            pltpu.VMEM((DV, DK), F32),
        ],
        compiler_params=_cp("arbitrary", "arbitrary"),
        name="gla_scan",
    )(proj, proj, proj, proj, proj, wgf, bgf, wgb, bgb, onorm, s0f, s0b)


def _fnet_in_body(h_ref, w_ref, cs_ref, a_ref, b_ref):
    u = _dot(h_ref[...], w_ref[...]).astype(BF16)
    for g in range(FG):
        ab = _dot(u[:, g * FD:(g + 1) * FD], cs_ref[...])
        a_ref[:, g * FD:(g + 1) * FD] = ab[:, :FD].astype(BF16)
        b_ref[:, g * FD:(g + 1) * FD] = ab[:, FD:].astype(BF16)


def _fnet_in(h, w, cs, rows):
    tm = 512
    return pl.pallas_call(
        _fnet_in_body,
        out_shape=(jax.ShapeDtypeStruct((rows, D), BF16), jax.ShapeDtypeStruct((rows, D), BF16)),
        grid=(rows // tm,),
        in_specs=[
            pl.BlockSpec((tm, D), lambda i: (i, 0)),
            pl.BlockSpec((D, D), lambda i: (0, 0)),
            pl.BlockSpec((FD, 2 * FD), lambda i: (0, 0)),
        ],
        out_specs=(pl.BlockSpec((tm, D), lambda i: (i, 0)), pl.BlockSpec((tm, D), lambda i: (i, 0))),
        compiler_params=_cp("arbitrary"),
        name="fnet_in",
    )(h, w, cs)


def _fnet_time_body(ct_ref, st_ref, a_ref, b_ref, o_ref):
    o_ref[...] = (_dot(ct_ref[...], a_ref[...]) + _dot(st_ref[...], b_ref[...])).astype(o_ref.dtype)


def _fnet_time(ct, st, a, b, T, row_off):
    tm = min(T, 512)
    tn = 1024
    rb = row_off // T
    return pl.pallas_call(
        _fnet_time_body,
        out_shape=jax.ShapeDtypeStruct((B * T, D), BF16),
        grid=(B, D // tn, T // tm),
        in_specs=[
            pl.BlockSpec((tm, T), lambda s, j, i: (i, 0)),
            pl.BlockSpec((tm, T), lambda s, j, i: (i, 0)),
            pl.BlockSpec((T, tn), lambda s, j, i: (rb + s, j)),
            pl.BlockSpec((T, tn), lambda s, j, i: (rb + s, j)),
        ],
        out_specs=pl.BlockSpec((tm, tn), lambda s, j, i: (s * (T // tm) + i, j)),
        compiler_params=_cp("arbitrary", "arbitrary", "arbitrary"),
        name="fnet_time",
    )(ct, st, a, b)


def _dft_mats(n, scale):
    idx = np.arange(n, dtype=np.int64)
    ang = 2.0 * np.pi * ((idx[:, None] * idx[None, :]) % n).astype(np.float64) / n
    return np.cos(ang) * scale, np.sin(ang) * scale


def _row_copy(src_hbm, row, dst_vmem, r, sem):
    return pltpu.make_async_copy(src_hbm.at[pl.ds(row, 1)], dst_vmem.at[pl.ds(r, 1)], sem)


def _moe_body(be_ref, nu_ref, tok_ref, h_hbm, wg_ref, wu_ref, wd_ref, o_ref, xbuf, sem, wgb, wub, wdb):
    i = pl.program_id(0)
    nu = nu_ref[0]

    def gather(blk, slot):
        base = blk * TME

        def issue(r, carry):
            _row_copy(h_hbm, tok_ref[base + r], xbuf.at[slot], r, sem.at[slot]).start()
            return carry

        lax.fori_loop(0, TME, issue, 0, unroll=8)

    @pl.when((i == 0) & (nu > 0))
    def _():
        gather(0, 0)

    @pl.when(i + 1 < nu)
    def _():
        gather(i + 1, (i + 1) % 2)

    e_now = be_ref[i]
    e_prev = be_ref[jnp.maximum(i - 1, 0)]

    @pl.when((i == 0) | (e_now != e_prev))
    def _():
        wgb[...] = wg_ref[...].astype(BF16)
        wub[...] = wu_ref[...].astype(BF16)
        wdb[...] = wd_ref[...].astype(BF16)

    @pl.when(i < nu)
    def _():
        slot = i % 2
        pltpu.make_async_copy(h_hbm.at[pl.ds(0, TME)], xbuf.at[slot], sem.at[slot]).wait()
        x = xbuf[slot].astype(BF16)
        hmid = _silu(_dot(x, wgb[...])) * _dot(x, wub[...])
        o_ref[...] = _dot(hmid.astype(BF16), wdb[...])

    @pl.when(i >= nu)
    def _():
        o_ref[...] = jnp.zeros_like(o_ref)


def _moe_experts(layer, block_e, n_used, buf_tok, h2, w_gate, w_up, w_down):
    nb = buf_tok.shape[0] // TME
    w_in_spec = pl.BlockSpec((None, None, D, MDE), lambda i, be, nu, tk: (layer, be[i], 0, 0))
    return pl.pallas_call(
        _moe_body,
        out_shape=jax.ShapeDtypeStruct((nb * TME, D), F32),
        grid_spec=pltpu.PrefetchScalarGridSpec(
            num_scalar_prefetch=3,
            grid=(nb,),
            in_specs=[
                pl.BlockSpec(memory_space=pl.ANY),
                w_in_spec,
                w_in_spec,
                pl.BlockSpec((None, None, MDE, D), lambda i, be, nu, tk: (layer, be[i], 0, 0)),
            ],
            out_specs=pl.BlockSpec((TME, D), lambda i, be, nu, tk: (i, 0)),
            scratch_shapes=[
                pltpu.VMEM((2, TME, D), F32),
                pltpu.SemaphoreType.DMA((2,)),
                pltpu.VMEM((D, MDE), BF16),
                pltpu.VMEM((D, MDE), BF16),
                pltpu.VMEM((MDE, D), BF16),
            ],
        ),
        compiler_params=_cp("arbitrary"),
        name="moe_experts",
    )(block_e, n_used, buf_tok, h2, w_gate, w_up, w_down)


TMC = 128


def _combine_body(dest_ref, x_ref, gate_ref, wt_ref, y_hbm, o_ref, ybuf, sem, *, nt):
    i = pl.program_id(0)

    def gather(tile, slot):
        base = tile * (2 * TMC)

        def issue(r, carry):
            _row_copy(y_hbm, dest_ref[base + 2 * r], ybuf.at[slot, 0], r, sem.at[slot]).start()
            _row_copy(y_hbm, dest_ref[base + 2 * r + 1], ybuf.at[slot, 1], r, sem.at[slot]).start()
            return carry

        lax.fori_loop(0, TMC, issue, 0, unroll=8)

    @pl.when(i == 0)
    def _():
        gather(0, 0)

    @pl.when(i + 1 < nt)
    def _():
        gather(i + 1, (i + 1) % 2)

    slot = i % 2
    for k in range(2):
        pltpu.make_async_copy(y_hbm.at[pl.ds(0, TMC)], ybuf.at[slot, k], sem.at[slot]).wait()
    wt = wt_ref[...]
    y = wt[:, 0:1] * ybuf[slot, 0] + wt[:, 1:2] * ybuf[slot, 1]
    o_ref[...] = x_ref[...] + gate_ref[...] * y


def _combine(dest, xs, mod3, wt, yb, rows):
    nt = rows // TMC
    return pl.pallas_call(
        functools.partial(_combine_body, nt=nt),
        out_shape=jax.ShapeDtypeStruct((rows, D), F32),
        grid_spec=pltpu.PrefetchScalarGridSpec(
            num_scalar_prefetch=1,
            grid=(nt,),
            in_specs=[
                pl.BlockSpec((TMC, D), lambda i, d: (i, 0)),
                pl.BlockSpec((None, 1, D), lambda i, d: (_seg_of(i, TMC) * N_MOD + 5, 0, 0)),
                pl.BlockSpec((TMC, 128), lambda i, d: (i, 0)),
                pl.BlockSpec(memory_space=pl.ANY),
            ],
            out_specs=pl.BlockSpec((TMC, D), lambda i, d: (i, 0)),
            scratch_shapes=[
                pltpu.VMEM((2, 2, TMC, D), F32),
                pltpu.SemaphoreType.DMA((2,)),
            ],
        ),
        compiler_params=_cp("arbitrary"),
        name="moe_combine",
    )(dest, xs, mod3, wt, yb)


def _route_plan(eid, rows):
    a = rows * 2
    nb = -(-a // TME) + ME
    e_flat = eid.reshape(-1)
    onehot = (e_flat[:, None] == jnp.arange(ME, dtype=jnp.int32)[None, :]).astype(jnp.int32)
    csum = jnp.cumsum(onehot, axis=0)
    counts = csum[-1]
    rank = jnp.take_along_axis(csum, e_flat[:, None], axis=1)[:, 0] - 1
    padded = ((counts + TME - 1) // TME) * TME
    pad_end = jnp.cumsum(padded)
    pad_start = pad_end - padded
    dest = pad_start[e_flat] + rank
    tok = jnp.arange(a, dtype=jnp.int32) // 2
    buf_tok = jnp.zeros((nb * TME,), jnp.int32).at[dest].set(tok)
    n_used = (pad_end[-1] // TME).astype(jnp.int32)
    blk = jnp.minimum(jnp.arange(nb, dtype=jnp.int32), n_used - 1)
    block_e = jnp.minimum(jnp.searchsorted(pad_end, blk * TME, side='right'), ME - 1).astype(jnp.int32)
    return dest, buf_tok, block_e, n_used.reshape(1)


def _final_body(x_ref, g_ref, o_ref):
    x = x_ref[...]
    o_ref[...] = x * lax.rsqrt(jnp.mean(x * x, axis=-1, keepdims=True) + EPS) * g_ref[...]


def _final_norm(xs, g):
    tm = 256
    return pl.pallas_call(
        _final_body,
        out_shape=jax.ShapeDtypeStruct((LAT, D), F32),
        grid=(LAT // tm,),
        in_specs=[pl.BlockSpec((tm, D), lambda i: (i, 0)), pl.BlockSpec((1, D), lambda i: (0, 0))],
        out_specs=pl.BlockSpec((tm, D), lambda i: (i, 0)),
        compiler_params=_cp("arbitrary"),
        name="final_norm",
    )(xs, g.reshape(1, D))


def _sincos_2d(rows, cols, d):
    quarter = d // 4
    omega = 1.0 / (POS_BASE ** (jnp.arange(quarter, dtype=F32) / quarter))

    def axis_emb(n):
        p = jnp.arange(n, dtype=F32)[:, None] * omega[None, :]
        return jnp.concatenate([jnp.sin(p), jnp.cos(p)], axis=-1)

    er, ec = axis_emb(rows), axis_emb(cols)
    half = 2 * quarter
    pos = jnp.concatenate([jnp.broadcast_to(er[:, None, :], (rows, cols, half)),
                           jnp.broadcast_to(ec[None, :, :], (rows, cols, half))], axis=-1)
    return pos.reshape(rows * cols, 2 * half)


def _gla_weights(w_in, wg_f, wg_b):
    z = jnp.zeros((D, ZPAD - 2 * RANK), F32)
    w_cat = jnp.concatenate([
        w_in[:, :QK], w_in[:, QK:QK + VV],
        w_in[:, STATE_COLS:STATE_COLS + QK], w_in[:, STATE_COLS + QK:],
        w_in[:, QK + VV:STATE_COLS], z], axis=1).astype(BF16)
    wgf = jnp.zeros((128, QK), F32).at[:RANK].set(wg_f)
    wgb = jnp.zeros((128, QK), F32).at[RANK:2 * RANK].set(wg_b)
    return w_cat, wgf, wgb


def kernel(x, c, ctx, c_ctx, ada_w, ada_b, norm1_g, norm2_g, gla_w_in, gla_wg_f, gla_bg_f, gla_wg_b, gla_bg_b,
           gla_onorm_g, gla_w_out, fnet_w_in, fnet_w_out, moe_rw_group, moe_rb_group, moe_rw_expert,
           moe_rb_expert, moe_w_gate, moe_w_up, moe_w_down, final_g):
    cvec = jnp.zeros((SEGS, D), F32).at[:B].set(c).at[CTX_SEG].set(c_ctx)
    mods = _mod_all(cvec, ada_w, ada_b)
    pos = _sincos_2d(SEQ // GRID_W, GRID_W, D)
    xs = _assemble(x.reshape(LAT, D), pos, ctx.reshape(NCTX, D))

    cc, sc = _dft_mats(FD, 1.0)
    cs = jnp.asarray(np.concatenate([cc, sc], axis=1), BF16)
    ct_l, st_l = _dft_mats(SEQ, (SEQ * FD) ** -0.5)
    ct_c, st_c = _dft_mats(CTX, (CTX * FD) ** -0.5)
    ct_l, st_l = jnp.asarray(ct_l, BF16), jnp.asarray(-st_l, BF16)
    ct_c, st_c = jnp.asarray(ct_c, BF16), jnp.asarray(-st_c, BF16)
    s_zero = jnp.zeros((B, H, DV, DK), F32)
    last_reader = ((DEPTH - 1) // 2) * 2

    for i in range(DEPTH):
        kind, j = i % 2, i // 2
        ctx_live = i < last_reader
        ctx_needed = i <= last_reader
        mod3 = mods[i].reshape(SEGS * N_MOD, 1, D)
        rows_in = ROWS if ctx_needed else LAT
        rows_out = ROWS if ctx_live else LAT
        h = _norm_mod(xs, norm1_g[i], mod3, 0, rows_in)

        if kind == 0:
            w_cat, wgf, wgb = _gla_weights(gla_w_in[j], gla_wg_f[j], gla_wg_b[j])
            bgf, bgb = gla_bg_f[j].reshape(1, QK), gla_bg_b[j].reshape(1, QK)
            onorm = gla_onorm_g[j].reshape(1, DV)
            proj = _mm(h, w_cat, rows_in, 512, 1280, F32, "gla_proj")
            o_c, sf, sb = _gla(proj, wgf, bgf, wgb, bgb, onorm, s_zero, s_zero, CTX, LAT, ctx_live)
            o_l, _, _ = _gla(proj, wgf, bgf, wgb, bgb, onorm, sf, sb, SEQ, 0, True)
            mix = jnp.concatenate([o_l, o_c], axis=0) if ctx_live else o_l
            xs = _mm_res(mix, gla_w_out[j].astype(BF16), xs, mod3, 2, rows_out)
        else:
            a, b = _fnet_in(h, fnet_w_in[j].astype(BF16), cs, rows_in)
            y = _fnet_time(ct_l, st_l, a, b, SEQ, 0)
            if ctx_live:
                y = jnp.concatenate([y, _fnet_time(ct_c, st_c, a, b, CTX, LAT)], axis=0)
            xs = _mm_res(y, fnet_w_out[j].astype(BF16), xs, mod3, 2, rows_out)

        rw = jnp.zeros((D, 128), F32).at[:, :MG].set(moe_rw_group[i])
        rw = rw.at[:, MG:MG + ME].set(jnp.transpose(moe_rw_expert[i], (1, 0, 2)).reshape(D, ME))
        rb = jnp.zeros((1, 128), F32).at[0, :MG].set(moe_rb_group[i]).at[0, MG:MG + ME].set(
            moe_rb_expert[i].reshape(ME))
        rw_hi = rw.astype(BF16)
        rw_lo = (rw - rw_hi.astype(F32)).astype(BF16)
        h2, eid, wt = _norm_route(xs, norm2_g[i], mod3, rw_hi, rw_lo, rb, rows_out)
        dest, buf_tok, block_e, n_used = _route_plan(eid[:, :2], rows_out)
        yb = _moe_experts(i, block_e, n_used, buf_tok, h2, moe_w_gate, moe_w_up, moe_w_down)
        xs = _combine(dest, xs, mod3, wt, yb, rows_out)

    return _final_norm(xs, final_g).reshape(B, SEQ, D)
```

```python
import functools

import jax
import jax.numpy as jnp
import numpy as np
from jax import lax
from jax.experimental import pallas as pl
from jax.experimental.pallas import tpu as pltpu

F32 = jnp.float32
BF16 = jnp.bfloat16

D = 2048
B = 4
SEQ = 2048
CTX = 256
DEPTH = 4
GRID_W = 64
EPS = 1e-6
POS_BASE = 10000.0
N_MOD = 6

H = 4
DK = 256
DV = 512
QK = H * DK
VV = H * DV
RANK = 16
GATE_NORM = 16.0
CHUNK = 64
STATE_COLS = QK + VV + 2 * RANK
ZPAD = 128
PROJ_N = QK + VV + QK + VV + ZPAD
COL_K, COL_V, COL_Q, COL_R, COL_Z = 0, QK, QK + VV, 2 * QK + VV, 2 * QK + 2 * VV

FG = 4
FD = D // FG

MG = 4
MPG = 8
ME = MG * MPG
MDE = D // 4
TME = 256

LAT = B * SEQ
NCTX = B * CTX
ROWS = LAT + NCTX
SEGS = 8
CTX_SEG = B

VMEM_LIMIT = 56 * 1024 * 1024


def _cp(*sem):
    return pltpu.CompilerParams(dimension_semantics=sem, vmem_limit_bytes=VMEM_LIMIT)


def _seg_of(i, tm):
    return jnp.where(i < LAT // tm, i // (SEQ // tm), CTX_SEG)


def _split2(a):
    hi = a.astype(BF16)
    lo = (a - hi.astype(F32)).astype(BF16)
    return hi, lo


def _dot(a, b):
    return jnp.dot(a, b, preferred_element_type=F32)


def _dot3(a, b_hi, b_lo):
    a_hi, a_lo = _split2(a)
    return _dot(a_hi, b_hi) + _dot(a_lo, b_hi) + _dot(a_hi, b_lo)


def _silu(a):
    return a * jax.nn.sigmoid(a)


def _mod_body(c_ref, w_ref, b_ref, o_ref):
    s = _silu(c_ref[...])
    w_hi, w_lo = _split2(w_ref[0])
    o_ref[0] = _dot3(s, w_hi, w_lo) + b_ref[0]


def _mod_all(cvec, ada_w, ada_b):
    tn = 512
    return pl.pallas_call(
        _mod_body,
        out_shape=jax.ShapeDtypeStruct((DEPTH, SEGS, N_MOD * D), F32),
        grid=(DEPTH, N_MOD * D // tn),
        in_specs=[
            pl.BlockSpec((SEGS, D), lambda l, j: (0, 0)),
            pl.BlockSpec((1, D, tn), lambda l, j: (l, 0, j)),
            pl.BlockSpec((1, 1, tn), lambda l, j: (l, 0, j)),
        ],
        out_specs=pl.BlockSpec((1, SEGS, tn), lambda l, j: (l, 0, j)),
        compiler_params=_cp("arbitrary", "arbitrary"),
        name="adaln_mod",
    )(cvec, ada_w, ada_b.reshape(DEPTH, 1, N_MOD * D))


def _assemble_body(x_ref, pos_ref, ctx_ref, o_ref, *, nlat):
    i = pl.program_id(0)

    @pl.when(i < nlat)
    def _():
        o_ref[...] = x_ref[...] + pos_ref[...]

    @pl.when(i >= nlat)
    def _():
        o_ref[...] = ctx_ref[...]


def _assemble(x2, pos, ctx2):
    tm = 256
    nlat = LAT // tm
    return pl.pallas_call(
        functools.partial(_assemble_body, nlat=nlat),
        out_shape=jax.ShapeDtypeStruct((ROWS, D), F32),
        grid=(ROWS // tm,),
        in_specs=[
            pl.BlockSpec((tm, D), lambda i: (jnp.minimum(i, nlat - 1), 0)),
            pl.BlockSpec((tm, D), lambda i: (i % (SEQ // tm), 0)),
            pl.BlockSpec((tm, D), lambda i: (jnp.maximum(i - nlat, 0), 0)),
        ],
        out_specs=pl.BlockSpec((tm, D), lambda i: (i, 0)),
        compiler_params=_cp("arbitrary"),
        name="assemble_stream",
    )(x2, pos, ctx2)


def _norm_mod_val(x, g, sh, sc):
    y = x * lax.rsqrt(jnp.mean(x * x, axis=-1, keepdims=True) + EPS) * g
    return y * (1.0 + sc) + sh


def _norm_mod_body(x_ref, g_ref, sh_ref, sc_ref, o_ref):
    o_ref[...] = _norm_mod_val(x_ref[...], g_ref[...], sh_ref[...], sc_ref[...]).astype(o_ref.dtype)


def _mod_spec(which, tm):
    return pl.BlockSpec((None, 1, D), lambda i: (_seg_of(i, tm) * N_MOD + which, 0, 0))


def _norm_mod(xs, g, mod3, which_shift, rows):
    tm = 256
    return pl.pallas_call(
        _norm_mod_body,
        out_shape=jax.ShapeDtypeStruct((rows, D), BF16),
        grid=(rows // tm,),
        in_specs=[
            pl.BlockSpec((tm, D), lambda i: (i, 0)),
            pl.BlockSpec((1, D), lambda i: (0, 0)),
            _mod_spec(which_shift, tm),
            _mod_spec(which_shift + 1, tm),
        ],
        out_specs=pl.BlockSpec((tm, D), lambda i: (i, 0)),
        compiler_params=_cp("arbitrary"),
        name="norm_mod",
    )(xs, g.reshape(1, D), mod3, mod3)


def _norm_route_body(x_ref, g_ref, sh_ref, sc_ref, rwh_ref, rwl_ref, rb_ref, h_ref, eid_ref, wt_ref):
    h2 = _norm_mod_val(x_ref[...], g_ref[...], sh_ref[...], sc_ref[...])
    h_ref[...] = h2
    lg = _dot3(h2, rwh_ref[...], rwl_ref[...]) + rb_ref[...]
    lane_i = lax.broadcasted_iota(jnp.int32, lg.shape, 1)
    lane = lane_i.astype(F32)
    neg = jnp.float32(-jnp.inf)
    big = jnp.float32(1024.0)

    glog = jnp.where(lane < MG, lg, neg)
    gmax = jnp.max(glog, axis=-1, keepdims=True)
    gidx = jnp.min(jnp.where(glog == gmax, lane, big), axis=-1, keepdims=True)
    g_w = 1.0 / jnp.sum(jnp.exp(glog - gmax), axis=-1, keepdims=True)

    lo = MG + MPG * gidx
    el = jnp.where((lane >= lo) & (lane < lo + MPG), lg, neg)
    m1 = jnp.max(el, axis=-1, keepdims=True)
    i1 = jnp.min(jnp.where(el == m1, lane, big), axis=-1, keepdims=True)
    el2 = jnp.where(lane == i1, neg, el)
    m2 = jnp.max(el2, axis=-1, keepdims=True)
    i2 = jnp.min(jnp.where(el2 == m2, lane, big), axis=-1, keepdims=True)
    e2 = jnp.exp(m2 - m1)
    den = 1.0 / (1.0 + e2)
    w1 = g_w * den
    w2 = g_w * (e2 * den)
    eid_ref[...] = jnp.where(lane_i == 0, i1 - MG, jnp.where(lane_i == 1, i2 - MG, 0.0)).astype(jnp.int32)
    wt_ref[...] = jnp.where(lane_i == 0, w1, jnp.where(lane_i == 1, w2, 0.0))


def _norm_route(xs, g, mod3, rw_hi, rw_lo, rb, rows):
    tm = 256
    return pl.pallas_call(
        _norm_route_body,
        out_shape=(
            jax.ShapeDtypeStruct((rows, D), F32),
            jax.ShapeDtypeStruct((rows, 128), jnp.int32),
            jax.ShapeDtypeStruct((rows, 128), F32),
        ),
        grid=(rows // tm,),
        in_specs=[
            pl.BlockSpec((tm, D), lambda i: (i, 0)),
            pl.BlockSpec((1, D), lambda i: (0, 0)),
            _mod_spec(3, tm),
            _mod_spec(4, tm),
            pl.BlockSpec((D, 128), lambda i: (0, 0)),
            pl.BlockSpec((D, 128), lambda i: (0, 0)),
            pl.BlockSpec((1, 128), lambda i: (0, 0)),
        ],
        out_specs=(
            pl.BlockSpec((tm, D), lambda i: (i, 0)),
            pl.BlockSpec((tm, 128), lambda i: (i, 0)),
            pl.BlockSpec((tm, 128), lambda i: (i, 0)),
        ),
        compiler_params=_cp("arbitrary"),
        name="norm_route",
    )(xs, g.reshape(1, D), mod3, mod3, rw_hi, rw_lo, rb)


def _mm_body(x_ref, w_ref, o_ref):
    o_ref[...] = _dot(x_ref[...], w_ref[...]).astype(o_ref.dtype)


def _mm(x, w, rows, tm, tn, out_dtype, name):
    k = x.shape[1]
    n = w.shape[1]
    return pl.pallas_call(
        _mm_body,
        out_shape=jax.ShapeDtypeStruct((rows, n), out_dtype),
        grid=(n // tn, rows // tm),
        in_specs=[
            pl.BlockSpec((tm, k), lambda j, i: (i, 0)),
            pl.BlockSpec((k, tn), lambda j, i: (0, j)),
        ],
        out_specs=pl.BlockSpec((tm, tn), lambda j, i: (i, j)),
        compiler_params=_cp("arbitrary", "arbitrary"),
        name=name,
    )(x, w)


def _mm_res_body(xl_ref, xc_ref, w_ref, res_ref, gate_ref, o_ref, *, nlat):
    i = pl.program_id(1)

    @pl.when(i < nlat)
    def _():
        o_ref[...] = res_ref[...] + gate_ref[...] * _dot(xl_ref[...], w_ref[...])

    @pl.when(i >= nlat)
    def _():
        o_ref[...] = res_ref[...] + gate_ref[...] * _dot(xc_ref[...], w_ref[...])


def _mm_res(a_lat, a_ctx, w, xs, mod3, which_gate, rows):
    tm, tn = 512, 1024
    k = a_lat.shape[1]
    nlat = LAT // tm
    return pl.pallas_call(
        functools.partial(_mm_res_body, nlat=nlat),
        out_shape=jax.ShapeDtypeStruct((rows, D), F32),
        grid=(D // tn, rows // tm),
        in_specs=[
            pl.BlockSpec((tm, k), lambda j, i: (jnp.minimum(i, nlat - 1), 0)),
            pl.BlockSpec((tm, k), lambda j, i: (jnp.maximum(i - nlat, 0), 0)),
            pl.BlockSpec((k, tn), lambda j, i: (0, j)),
            pl.BlockSpec((tm, tn), lambda j, i: (i, j)),
            pl.BlockSpec((None, 1, tn), lambda j, i: (_seg_of(i, tm) * N_MOD + which_gate, 0, j)),
        ],
        out_specs=pl.BlockSpec((tm, tn), lambda j, i: (i, j)),
        compiler_params=_cp("arbitrary", "arbitrary"),
        name="mm_residual",
    )(a_lat, a_ctx, w, xs, mod3)


def _log_sigmoid(a):
    return jnp.minimum(a, 0.0) - jnp.log1p(jnp.exp(-jnp.abs(a)))


PREP = 256


def _gla_body(k_ref, v_ref, q_ref, r_ref, z_ref, wgf_ref, bgf_ref, wgb_ref, bgb_ref, on_ref, s0f_ref, s0b_ref,
              o_ref, sf_ref, sb_ref,
              qf_s, kf_s, df_s, qb_s, kb_s, db_s, vb_s, dec_s, o_s, stf_s, stb_s, *, T, emit_o):
    nc = T // CHUNK
    nblk = T // PREP
    cpb = PREP // CHUNK
    scale = DK ** -0.5

    row = lax.broadcasted_iota(jnp.int32, (PREP, PREP), 0)
    col = lax.broadcasted_iota(jnp.int32, (PREP, PREP), 1)
    shift = CHUNK.bit_length() - 1
    same = lax.shift_right_logical(row, shift) == lax.shift_right_logical(col, shift)
    tri_f = jnp.where(same & (row >= col), 1.0, 0.0).astype(BF16)
    tri_b = jnp.where(same & (row <= col), 1.0, 0.0).astype(BF16)
    wf_hi, wf_lo = _split2(wgf_ref[...])
    wb_hi, wb_lo = _split2(wgb_ref[...])

    def chunk_sums(tri, g):
        g1 = g.astype(BF16)
        rem = g - g1.astype(F32)
        g2 = rem.astype(BF16)
        g3 = (rem - g2.astype(F32)).astype(BF16)
        return _dot(tri, g1) + _dot(tri, g2) + _dot(tri, g3)

    def edge_rows(G, e):
        return jnp.concatenate(
            [jnp.broadcast_to(G[c * CHUNK + e:c * CHUNK + e + 1, :], (CHUNK, DK)) for c in range(cpb)], axis=0)

    def prep(blk, carry):
        rows = pl.ds(pl.multiple_of(blk * PREP, PREP), PREP)
        z = z_ref[rows, :]
        gf = _log_sigmoid(_dot3(z, wf_hi, wf_lo) + bgf_ref[...]) / GATE_NORM
        gb = _log_sigmoid(_dot3(z, wb_hi, wb_lo) + bgb_ref[...]) / GATE_NORM
        Gf = chunk_sums(tri_f, gf)
        Gb = chunk_sums(tri_b, gb)
        k = k_ref[rows, :]
        q = q_ref[rows, :] * scale
        qf_s[rows, :] = (q * jnp.exp(Gf)).astype(BF16)
        kf_s[rows, :] = (k * jnp.exp(-Gf)).astype(BF16)
        df_s[rows, :] = (k * jnp.exp(edge_rows(Gf, CHUNK - 1) - Gf)).astype(BF16)
        qb_s[rows, :] = (q * jnp.exp(Gb)).astype(BF16)
        kb_s[rows, :] = (k * jnp.exp(-Gb)).astype(BF16)
        db_s[rows, :] = (k * jnp.exp(edge_rows(Gb, 0) - Gb)).astype(BF16)
        vb_s[rows, :] = v_ref[rows, :].astype(BF16)
        for c in range(cpb):
            ef = Gf[c * CHUNK + CHUNK - 1:c * CHUNK + CHUNK, :]
            eb = Gb[c * CHUNK:c * CHUNK + 1, :]
            dec_s[0, blk * cpb + c] = jnp.broadcast_to(jnp.exp(ef), (8, DK))
            dec_s[1, blk * cpb + c] = jnp.broadcast_to(jnp.exp(eb), (8, DK))
        return carry

    lax.fori_loop(0, nblk, prep, 0)

    stf_s[...] = s0f_ref[0, 0]
    stb_s[...] = s0b_ref[0, 0]
    r64 = lax.broadcasted_iota(jnp.int32, (CHUNK, CHUNK), 0)
    c64 = lax.broadcasted_iota(jnp.int32, (CHUNK, CHUNK), 1)
    nt_dims = (((1,), (1,)), ((), ()))
    tn_dims = (((0,), (0,)), ((), ()))

    def chunk(c, q_s, k_s, d_s, st_s, di, keep):
        rows = pl.ds(pl.multiple_of(c * CHUNK, CHUNK), CHUNK)
        qe = q_s[rows, :]
        vb = vb_s[rows, :]
        st = st_s[...]
        o = None
        if emit_o:
            a = lax.dot_general(qe, k_s[rows, :], nt_dims, preferred_element_type=F32)
            a = jnp.where(keep, a, 0.0).astype(BF16)
            o = _dot(a, vb) + lax.dot_general(qe, st.astype(BF16), nt_dims, preferred_element_type=F32)
        upd = lax.dot_general(vb, d_s[rows, :], tn_dims, preferred_element_type=F32)
        st_s[...] = st * dec_s[di, c][0:1, :] + upd
        return o, rows

    def step(ci, carry, accumulate):
        of, rows_f = chunk(ci, qf_s, kf_s, df_s, stf_s, 0, r64 >= c64)
        ob, rows_b = chunk(nc - 1 - ci, qb_s, kb_s, db_s, stb_s, 1, r64 <= c64)
        if emit_o:
            if accumulate:
                o_s[rows_f, :] = o_s[rows_f, :] + of
                o_s[rows_b, :] = o_s[rows_b, :] + ob
            else:
                o_s[rows_f, :] = of
                o_s[rows_b, :] = ob
        return carry

    lax.fori_loop(0, nc // 2, functools.partial(step, accumulate=False), 0)
    lax.fori_loop(nc // 2, nc, functools.partial(step, accumulate=True), 0)
    sf_ref[0, 0] = stf_s[...]
    sb_ref[0, 0] = stb_s[...]

    if emit_o:
        def fin(blk, carry):
            rows = pl.ds(pl.multiple_of(blk * PREP, PREP), PREP)
            tot = o_s[rows, :]
            y = tot * lax.rsqrt(jnp.mean(tot * tot, axis=-1, keepdims=True) + EPS) * on_ref[...]
            o_ref[rows, :] = (y * _silu(r_ref[rows, :])).astype(o_ref.dtype)
            return carry

        lax.fori_loop(0, nblk, fin, 0)
    else:
        o_ref[...] = jnp.zeros_like(o_ref)


def _gla(proj, wgf, bgf, wgb, bgb, onorm, s0f, s0b, T, row_off, emit_o):
    rb = row_off // T
    st_spec = pl.BlockSpec((1, 1, DV, DK), lambda b, h: (b, h, 0, 0))
    return pl.pallas_call(
        functools.partial(_gla_body, T=T, emit_o=emit_o),
        out_shape=(
            jax.ShapeDtypeStruct((B * T, VV), BF16),
            jax.ShapeDtypeStruct((B, H, DV, DK), F32),
            jax.ShapeDtypeStruct((B, H, DV, DK), F32),
        ),
        grid=(B, H),
        in_specs=[
            pl.BlockSpec((T, DK), lambda b, h: (rb + b, COL_K // DK + h)),
            pl.BlockSpec((T, DV), lambda b, h: (rb + b, COL_V // DV + h)),
            pl.BlockSpec((T, DK), lambda b, h: (rb + b, COL_Q // DK + h)),
            pl.BlockSpec((T, DV), lambda b, h: (rb + b, COL_R // DV + h)),
            pl.BlockSpec((T, ZPAD), lambda b, h: (rb + b, COL_Z // ZPAD)),
            pl.BlockSpec((128, DK), lambda b, h: (0, h)),
            pl.BlockSpec((1, DK), lambda b, h: (0, h)),
            pl.BlockSpec((128, DK), lambda b, h: (0, h)),
            pl.BlockSpec((1, DK), lambda b, h: (0, h)),
            pl.BlockSpec((1, DV), lambda b, h: (0, 0)),
            st_spec,
            st_spec,
        ],
        out_specs=(
            pl.BlockSpec((T, DV), lambda b, h: (b, h)),
            st_spec,
            st_spec,
        ),
        scratch_shapes=[pltpu.VMEM((T, DK), BF16)] * 6 + [
            pltpu.VMEM((T, DV), BF16),
            pltpu.VMEM((2, T // CHUNK, 8, DK), F32),
            pltpu.VMEM((T, DV), F32),
            pltpu.VMEM((DV, DK), F32),
            pltpu.VMEM((DV, DK), F32),
        ],
        compiler_params=_cp("arbitrary", "arbitrary"),
        name="gla_scan",
    )(proj, proj, proj, proj, proj, wgf, bgf, wgb, bgb, onorm, s0f, s0b)


def _fnet_in_body(h_ref, w_ref, cs_ref, a_ref, b_ref):
    u = _dot(h_ref[...], w_ref[...]).astype(BF16)
    for g in range(FG):
        ab = _dot(u[:, g * FD:(g + 1) * FD], cs_ref[...])
        a_ref[:, g * FD:(g + 1) * FD] = ab[:, :FD].astype(BF16)
        b_ref[:, g * FD:(g + 1) * FD] = ab[:, FD:].astype(BF16)


def _fnet_in(h, w, cs, rows):
    tm = 512
    return pl.pallas_call(
        _fnet_in_body,
        out_shape=(jax.ShapeDtypeStruct((rows, D), BF16), jax.ShapeDtypeStruct((rows, D), BF16)),
        grid=(rows // tm,),
        in_specs=[
            pl.BlockSpec((tm, D), lambda i: (i, 0)),
            pl.BlockSpec((D, D), lambda i: (0, 0)),
            pl.BlockSpec((FD, 2 * FD), lambda i: (0, 0)),
        ],
        out_specs=(pl.BlockSpec((tm, D), lambda i: (i, 0)), pl.BlockSpec((tm, D), lambda i: (i, 0))),
        compiler_params=_cp("arbitrary"),
        name="fnet_in",
    )(h, w, cs)


def _fnet_time_body(ct_ref, st_ref, a_ref, b_ref, o_ref):
    o_ref[...] = (_dot(ct_ref[...], a_ref[...]) + _dot(st_ref[...], b_ref[...])).astype(o_ref.dtype)


def _fnet_time(ct, st, a, b, T, row_off):
    tm = min(T, 512)
    tn = 1024
    rb = row_off // T
    return pl.pallas_call(
        _fnet_time_body,
        out_shape=jax.ShapeDtypeStruct((B * T, D), BF16),
        grid=(B, D // tn, T // tm),
        in_specs=[
            pl.BlockSpec((tm, T), lambda s, j, i: (i, 0)),
            pl.BlockSpec((tm, T), lambda s, j, i: (i, 0)),
            pl.BlockSpec((T, tn), lambda s, j, i: (rb + s, j)),
            pl.BlockSpec((T, tn), lambda s, j, i: (rb + s, j)),
        ],
        out_specs=pl.BlockSpec((tm, tn), lambda s, j, i: (s * (T // tm) + i, j)),
        compiler_params=_cp("arbitrary", "arbitrary", "arbitrary"),
        name="fnet_time",
    )(ct, st, a, b)


def _dft_mats(n, scale):
    idx = np.arange(n, dtype=np.int64)
    ang = 2.0 * np.pi * ((idx[:, None] * idx[None, :]) % n).astype(np.float64) / n
    return np.cos(ang) * scale, np.sin(ang) * scale


def _row_copy(src_hbm, row, dst_vmem, r, sem):
    return pltpu.make_async_copy(src_hbm.at[pl.ds(row, 1)], dst_vmem.at[pl.ds(r, 1)], sem)


def _moe_body(be_ref, nu_ref, tok_ref, h_hbm, wg_ref, wu_ref, wd_ref, o_ref, xbuf, sem, wgb, wub, wdb):
    i = pl.program_id(0)
    nu = nu_ref[0]

    def gather(blk, slot):
        base = blk * TME

        for r in range(TME):
            _row_copy(h_hbm, tok_ref[base + r], xbuf.at[slot], r, sem.at[slot]).start()

    @pl.when((i == 0) & (nu > 0))
    def _():
        gather(0, 0)

    @pl.when(i + 1 < nu)
    def _():
        gather(i + 1, (i + 1) % 2)

    e_now = be_ref[i]
    e_prev = be_ref[jnp.maximum(i - 1, 0)]

    @pl.when((i == 0) | (e_now != e_prev))
    def _():
        wgb[...] = wg_ref[...].astype(BF16)
        wub[...] = wu_ref[...].astype(BF16)
        wdb[...] = wd_ref[...].astype(BF16)

    @pl.when(i < nu)
    def _():
        slot = i % 2
        pltpu.make_async_copy(h_hbm.at[pl.ds(0, TME)], xbuf.at[slot], sem.at[slot]).wait()
        x = xbuf[slot].astype(BF16)
        hmid = _silu(_dot(x, wgb[...])) * _dot(x, wub[...])
        o_ref[...] = _dot(hmid.astype(BF16), wdb[...])

    @pl.when(i >= nu)
    def _():
        o_ref[...] = jnp.zeros_like(o_ref)


def _moe_experts(layer, block_e, n_used, buf_tok, h2, w_gate, w_up, w_down):
    nb = buf_tok.shape[0] // TME
    w_in_spec = pl.BlockSpec((None, None, D, MDE), lambda i, be, nu, tk: (layer, be[i], 0, 0))
    return pl.pallas_call(
        _moe_body,
        out_shape=jax.ShapeDtypeStruct((nb * TME, D), F32),
        grid_spec=pltpu.PrefetchScalarGridSpec(
            num_scalar_prefetch=3,
            grid=(nb,),
            in_specs=[
                pl.BlockSpec(memory_space=pl.ANY),
                w_in_spec,
                w_in_spec,
                pl.BlockSpec((None, None, MDE, D), lambda i, be, nu, tk: (layer, be[i], 0, 0)),
            ],
            out_specs=pl.BlockSpec((TME, D), lambda i, be, nu, tk: (i, 0)),
            scratch_shapes=[
                pltpu.VMEM((2, TME, D), F32),
                pltpu.SemaphoreType.DMA((2,)),
                pltpu.VMEM((D, MDE), BF16),
                pltpu.VMEM((D, MDE), BF16),
                pltpu.VMEM((MDE, D), BF16),
            ],
        ),
        compiler_params=_cp("arbitrary"),
        name="moe_experts",
    )(block_e, n_used, buf_tok, h2, w_gate, w_up, w_down)


TMC = 128


def _combine_body(dest_ref, x_ref, gate_ref, wt_ref, y_hbm, o_ref, ybuf, sem, *, nt):
    i = pl.program_id(0)

    def gather(tile, slot):
        base = tile * (2 * TMC)

        for r in range(TMC):
            _row_copy(y_hbm, dest_ref[base + 2 * r], ybuf.at[slot, 0], r, sem.at[slot]).start()
            _row_copy(y_hbm, dest_ref[base + 2 * r + 1], ybuf.at[slot, 1], r, sem.at[slot]).start()

    @pl.when(i == 0)
    def _():
        gather(0, 0)

    @pl.when(i + 1 < nt)
    def _():
        gather(i + 1, (i + 1) % 2)

    slot = i % 2
    for k in range(2):
        pltpu.make_async_copy(y_hbm.at[pl.ds(0, TMC)], ybuf.at[slot, k], sem.at[slot]).wait()
    wt = wt_ref[...]
    y = wt[:, 0:1] * ybuf[slot, 0] + wt[:, 1:2] * ybuf[slot, 1]
    o_ref[...] = x_ref[...] + gate_ref[...] * y


def _combine(dest, xs, mod3, wt, yb, rows):
    nt = rows // TMC
    return pl.pallas_call(
        functools.partial(_combine_body, nt=nt),
        out_shape=jax.ShapeDtypeStruct((rows, D), F32),
        grid_spec=pltpu.PrefetchScalarGridSpec(
            num_scalar_prefetch=1,
            grid=(nt,),
            in_specs=[
                pl.BlockSpec((TMC, D), lambda i, d: (i, 0)),
                pl.BlockSpec((None, 1, D), lambda i, d: (_seg_of(i, TMC) * N_MOD + 5, 0, 0)),
                pl.BlockSpec((TMC, 128), lambda i, d: (i, 0)),
                pl.BlockSpec(memory_space=pl.ANY),
            ],
            out_specs=pl.BlockSpec((TMC, D), lambda i, d: (i, 0)),
            scratch_shapes=[
                pltpu.VMEM((2, 2, TMC, D), F32),
                pltpu.SemaphoreType.DMA((2,)),
            ],
        ),
        compiler_params=_cp("arbitrary"),
        name="moe_combine",
    )(dest, xs, mod3, wt, yb)


def _route_plan(eid, rows):
    a = rows * 2
    nb = -(-a // TME) + ME
    e_flat = eid.reshape(-1)
    onehot = (e_flat[:, None] == jnp.arange(ME, dtype=jnp.int32)[None, :]).astype(jnp.int32)
    csum = jnp.cumsum(onehot, axis=0)
    counts = csum[-1]
    rank = jnp.take_along_axis(csum, e_flat[:, None], axis=1)[:, 0] - 1
    padded = ((counts + TME - 1) // TME) * TME
    pad_end = jnp.cumsum(padded)
    pad_start = pad_end - padded
    dest = pad_start[e_flat] + rank
    tok = jnp.arange(a, dtype=jnp.int32) // 2
    buf_tok = jnp.zeros((nb * TME,), jnp.int32).at[dest].set(tok)
    n_used = (pad_end[-1] // TME).astype(jnp.int32)
    blk = jnp.minimum(jnp.arange(nb, dtype=jnp.int32), n_used - 1)
    block_e = jnp.minimum(jnp.searchsorted(pad_end, blk * TME, side='right'), ME - 1).astype(jnp.int32)
    return dest, buf_tok, block_e, n_used.reshape(1)


def _final_body(x_ref, g_ref, o_ref):
    x = x_ref[...]
    o_ref[...] = x * lax.rsqrt(jnp.mean(x * x, axis=-1, keepdims=True) + EPS) * g_ref[...]


def _final_norm(xs, g):
    tm = 256
    return pl.pallas_call(
        _final_body,
        out_shape=jax.ShapeDtypeStruct((LAT, D), F32),
        grid=(LAT // tm,),
        in_specs=[pl.BlockSpec((tm, D), lambda i: (i, 0)), pl.BlockSpec((1, D), lambda i: (0, 0))],
        out_specs=pl.BlockSpec((tm, D), lambda i: (i, 0)),
        compiler_params=_cp("arbitrary"),
        name="final_norm",
    )(xs, g.reshape(1, D))


def _sincos_2d(rows, cols, d):
    quarter = d // 4
    omega = 1.0 / (POS_BASE ** (jnp.arange(quarter, dtype=F32) / quarter))

    def axis_emb(n):
        p = jnp.arange(n, dtype=F32)[:, None] * omega[None, :]
        return jnp.concatenate([jnp.sin(p), jnp.cos(p)], axis=-1)

    er, ec = axis_emb(rows), axis_emb(cols)
    half = 2 * quarter
    pos = jnp.concatenate([jnp.broadcast_to(er[:, None, :], (rows, cols, half)),
                           jnp.broadcast_to(ec[None, :, :], (rows, cols, half))], axis=-1)
    return pos.reshape(rows * cols, 2 * half)


def _gla_weights(w_in, wg_f, wg_b):
    z = jnp.zeros((D, ZPAD - 2 * RANK), F32)
    w_cat = jnp.concatenate([
        w_in[:, :QK], w_in[:, QK:QK + VV],
        w_in[:, STATE_COLS:STATE_COLS + QK], w_in[:, STATE_COLS + QK:],
        w_in[:, QK + VV:STATE_COLS], z], axis=1).astype(BF16)
    wgf = jnp.zeros((128, QK), F32).at[:RANK].set(wg_f)
    wgb = jnp.zeros((128, QK), F32).at[RANK:2 * RANK].set(wg_b)
    return w_cat, wgf, wgb


def kernel(x, c, ctx, c_ctx, ada_w, ada_b, norm1_g, norm2_g, gla_w_in, gla_wg_f, gla_bg_f, gla_wg_b, gla_bg_b,
           gla_onorm_g, gla_w_out, fnet_w_in, fnet_w_out, moe_rw_group, moe_rb_group, moe_rw_expert,
           moe_rb_expert, moe_w_gate, moe_w_up, moe_w_down, final_g):
    cvec = jnp.zeros((SEGS, D), F32).at[:B].set(c).at[CTX_SEG].set(c_ctx)
    mods = _mod_all(cvec, ada_w, ada_b)
    pos = _sincos_2d(SEQ // GRID_W, GRID_W, D)
    xs = _assemble(x.reshape(LAT, D), pos, ctx.reshape(NCTX, D))

    cc, sc = _dft_mats(FD, 1.0)
    cs = jnp.asarray(np.concatenate([cc, sc], axis=1), BF16)
    ct_l, st_l = _dft_mats(SEQ, (SEQ * FD) ** -0.5)
    ct_c, st_c = _dft_mats(CTX, (CTX * FD) ** -0.5)
    ct_l, st_l = jnp.asarray(ct_l, BF16), jnp.asarray(-st_l, BF16)
    ct_c, st_c = jnp.asarray(ct_c, BF16), jnp.asarray(-st_c, BF16)
    s_zero = jnp.zeros((B, H, DV, DK), F32)
    last_reader = ((DEPTH - 1) // 2) * 2

    for i in range(DEPTH):
        kind, j = i % 2, i // 2
        ctx_live = i < last_reader
        ctx_needed = i <= last_reader
        mod3 = mods[i].reshape(SEGS * N_MOD, 1, D)
        rows_in = ROWS if ctx_needed else LAT
        rows_out = ROWS if ctx_live else LAT
        h = _norm_mod(xs, norm1_g[i], mod3, 0, rows_in)

        if kind == 0:
            w_cat, wgf, wgb = _gla_weights(gla_w_in[j], gla_wg_f[j], gla_wg_b[j])
            bgf, bgb = gla_bg_f[j].reshape(1, QK), gla_bg_b[j].reshape(1, QK)
            onorm = gla_onorm_g[j].reshape(1, DV)
            proj = _mm(h, w_cat, rows_in, 512, 896, F32, "gla_proj")
            o_c, sf, sb = _gla(proj, wgf, bgf, wgb, bgb, onorm, s_zero, s_zero, CTX, LAT, ctx_live)
            o_l, _, _ = _gla(proj, wgf, bgf, wgb, bgb, onorm, sf, sb, SEQ, 0, True)
            xs = _mm_res(o_l, o_c, gla_w_out[j].astype(BF16), xs, mod3, 2, rows_out)
        else:
            a, b = _fnet_in(h, fnet_w_in[j].astype(BF16), cs, rows_in)
            y = _fnet_time(ct_l, st_l, a, b, SEQ, 0)
            y_c = _fnet_time(ct_c, st_c, a, b, CTX, LAT) if ctx_live else y
            xs = _mm_res(y, y_c, fnet_w_out[j].astype(BF16), xs, mod3, 2, rows_out)

        rw = jnp.zeros((D, 128), F32).at[:, :MG].set(moe_rw_group[i])
        rw = rw.at[:, MG:MG + ME].set(jnp.transpose(moe_rw_expert[i], (1, 0, 2)).reshape(D, ME))
        rb = jnp.zeros((1, 128), F32).at[0, :MG].set(moe_rb_group[i]).at[0, MG:MG + ME].set(
            moe_rb_expert[i].reshape(ME))
        rw_hi = rw.astype(BF16)
        rw_lo = (rw - rw_hi.astype(F32)).astype(BF16)
        h2, eid, wt = _norm_route(xs, norm2_g[i], mod3, rw_hi, rw_lo, rb, rows_out)
        dest, buf_tok, block_e, n_used = _route_plan(eid[:, :2], rows_out)
        yb = _moe_experts(i, block_e, n_used, buf_tok, h2, moe_w_gate, moe_w_up, moe_w_down)
        xs = _combine(dest, xs, mod3, wt, yb, rows_out)

    return _final_norm(xs, final_g).reshape(B, SEQ, D)
```

```python
import functools

import jax
import jax.numpy as jnp
import numpy as np
from jax import lax
from jax.experimental import pallas as pl
from jax.experimental.pallas import tpu as pltpu

F32 = jnp.float32
BF16 = jnp.bfloat16

D = 2048
B = 4
SEQ = 2048
CTX = 256
DEPTH = 4
GRID_W = 64
EPS = 1e-6
POS_BASE = 10000.0
N_MOD = 6

H = 4
DK = 256
DV = 512
QK = H * DK
VV = H * DV
RANK = 16
GATE_NORM = 16.0
CHUNK = 64
STATE_COLS = QK + VV + 2 * RANK
ZPAD = 256
PROJ_N = QK + VV + QK + VV + ZPAD
COL_K, COL_V, COL_Q, COL_R, COL_Z = 0, QK, QK + VV, 2 * QK + VV, 2 * QK + 2 * VV

FG = 4
FD = D // FG

MG = 4
MPG = 8
ME = MG * MPG
MDE = D // 4
TME = 256

LAT = B * SEQ
NCTX = B * CTX
ROWS = LAT + NCTX
SEGS = 8
CTX_SEG = B

VMEM_LIMIT = 56 * 1024 * 1024


def _cp(*sem):
    return pltpu.CompilerParams(dimension_semantics=sem, vmem_limit_bytes=VMEM_LIMIT)


def _seg_of(i, tm):
    return jnp.where(i < LAT // tm, i // (SEQ // tm), CTX_SEG)


def _split2(a):
    hi = a.astype(BF16)
    lo = (a - hi.astype(F32)).astype(BF16)
    return hi, lo


def _dot(a, b):
    return jnp.dot(a, b, preferred_element_type=F32)


def _dot3(a, b_hi, b_lo):
    a_hi, a_lo = _split2(a)
    return _dot(a_hi, b_hi) + _dot(a_lo, b_hi) + _dot(a_hi, b_lo)


def _silu(a):
    return a * jax.nn.sigmoid(a)


def _mod_body(c_ref, w_ref, b_ref, o_ref):
    s = _silu(c_ref[...])
    w_hi, w_lo = _split2(w_ref[0])
    o_ref[0] = _dot3(s, w_hi, w_lo) + b_ref[0]


def _mod_all(cvec, ada_w, ada_b):
    tn = 512
    return pl.pallas_call(
        _mod_body,
        out_shape=jax.ShapeDtypeStruct((DEPTH, SEGS, N_MOD * D), F32),
        grid=(DEPTH, N_MOD * D // tn),
        in_specs=[
            pl.BlockSpec((SEGS, D), lambda l, j: (0, 0)),
            pl.BlockSpec((1, D, tn), lambda l, j: (l, 0, j)),
            pl.BlockSpec((1, 1, tn), lambda l, j: (l, 0, j)),
        ],
        out_specs=pl.BlockSpec((1, SEGS, tn), lambda l, j: (l, 0, j)),
        compiler_params=_cp("arbitrary", "arbitrary"),
        name="adaln_mod",
    )(cvec, ada_w, ada_b.reshape(DEPTH, 1, N_MOD * D))


def _assemble_body(x_ref, pos_ref, ctx_ref, o_ref, *, nlat):
    i = pl.program_id(0)

    @pl.when(i < nlat)
    def _():
        o_ref[...] = x_ref[...] + pos_ref[...]

    @pl.when(i >= nlat)
    def _():
        o_ref[...] = ctx_ref[...]


def _assemble(x2, pos, ctx2):
    tm = 256
    nlat = LAT // tm
    return pl.pallas_call(
        functools.partial(_assemble_body, nlat=nlat),
        out_shape=jax.ShapeDtypeStruct((ROWS, D), F32),
        grid=(ROWS // tm,),
        in_specs=[
            pl.BlockSpec((tm, D), lambda i: (jnp.minimum(i, nlat - 1), 0)),
            pl.BlockSpec((tm, D), lambda i: (i % (SEQ // tm), 0)),
            pl.BlockSpec((tm, D), lambda i: (jnp.maximum(i - nlat, 0), 0)),
        ],
        out_specs=pl.BlockSpec((tm, D), lambda i: (i, 0)),
        compiler_params=_cp("arbitrary"),
        name="assemble_stream",
    )(x2, pos, ctx2)


def _norm_mod_val(x, g, sh, sc):
    y = x * lax.rsqrt(jnp.mean(x * x, axis=-1, keepdims=True) + EPS) * g
    return y * (1.0 + sc) + sh


def _norm_mod_body(x_ref, g_ref, sh_ref, sc_ref, o_ref):
    o_ref[...] = _norm_mod_val(x_ref[...], g_ref[...], sh_ref[...], sc_ref[...]).astype(o_ref.dtype)


def _mod_spec(which, tm):
    return pl.BlockSpec((None, 1, D), lambda i: (_seg_of(i, tm) * N_MOD + which, 0, 0))


def _norm_mod(xs, g, mod3, which_shift, rows):
    tm = 256
    return pl.pallas_call(
        _norm_mod_body,
        out_shape=jax.ShapeDtypeStruct((rows, D), BF16),
        grid=(rows // tm,),
        in_specs=[
            pl.BlockSpec((tm, D), lambda i: (i, 0)),
            pl.BlockSpec((1, D), lambda i: (0, 0)),
            _mod_spec(which_shift, tm),
            _mod_spec(which_shift + 1, tm),
        ],
        out_specs=pl.BlockSpec((tm, D), lambda i: (i, 0)),
        compiler_params=_cp("arbitrary"),
        name="norm_mod",
    )(xs, g.reshape(1, D), mod3, mod3)


def _norm_route_body(x_ref, g_ref, sh_ref, sc_ref, rwh_ref, rwl_ref, rb_ref, h_ref, eid_ref, wt_ref):
    h2 = _norm_mod_val(x_ref[...], g_ref[...], sh_ref[...], sc_ref[...])
    h_ref[...] = h2
    lg = _dot3(h2, rwh_ref[...], rwl_ref[...]) + rb_ref[...]
    lane_i = lax.broadcasted_iota(jnp.int32, lg.shape, 1)
    lane = lane_i.astype(F32)
    neg = jnp.float32(-jnp.inf)
    big = jnp.float32(1024.0)

    glog = jnp.where(lane < MG, lg, neg)
    gmax = jnp.max(glog, axis=-1, keepdims=True)
    gidx = jnp.min(jnp.where(glog == gmax, lane, big), axis=-1, keepdims=True)
    g_w = 1.0 / jnp.sum(jnp.exp(glog - gmax), axis=-1, keepdims=True)

    lo = MG + MPG * gidx
    el = jnp.where((lane >= lo) & (lane < lo + MPG), lg, neg)
    m1 = jnp.max(el, axis=-1, keepdims=True)
    i1 = jnp.min(jnp.where(el == m1, lane, big), axis=-1, keepdims=True)
    el2 = jnp.where(lane == i1, neg, el)
    m2 = jnp.max(el2, axis=-1, keepdims=True)
    i2 = jnp.min(jnp.where(el2 == m2, lane, big), axis=-1, keepdims=True)
    e2 = jnp.exp(m2 - m1)
    den = 1.0 / (1.0 + e2)
    w1 = g_w * den
    w2 = g_w * (e2 * den)
    eid_ref[...] = jnp.where(lane_i == 0, i1 - MG, jnp.where(lane_i == 1, i2 - MG, 0.0)).astype(jnp.int32)
    wt_ref[...] = jnp.where(lane_i == 0, w1, jnp.where(lane_i == 1, w2, 0.0))


def _norm_route(xs, g, mod3, rw_hi, rw_lo, rb, rows):
    tm = 256
    return pl.pallas_call(
        _norm_route_body,
        out_shape=(
            jax.ShapeDtypeStruct((rows, D), F32),
            jax.ShapeDtypeStruct((rows, 128), jnp.int32),
            jax.ShapeDtypeStruct((rows, 128), F32),
        ),
        grid=(rows // tm,),
        in_specs=[
            pl.BlockSpec((tm, D), lambda i: (i, 0)),
            pl.BlockSpec((1, D), lambda i: (0, 0)),
            _mod_spec(3, tm),
            _mod_spec(4, tm),
            pl.BlockSpec((D, 128), lambda i: (0, 0)),
            pl.BlockSpec((D, 128), lambda i: (0, 0)),
            pl.BlockSpec((1, 128), lambda i: (0, 0)),
        ],
        out_specs=(
            pl.BlockSpec((tm, D), lambda i: (i, 0)),
            pl.BlockSpec((tm, 128), lambda i: (i, 0)),
            pl.BlockSpec((tm, 128), lambda i: (i, 0)),
        ),
        compiler_params=_cp("arbitrary"),
        name="norm_route",
    )(xs, g.reshape(1, D), mod3, mod3, rw_hi, rw_lo, rb)


def _mm_body(x_ref, w_ref, o_ref):
    o_ref[...] = _dot(x_ref[...], w_ref[...]).astype(o_ref.dtype)


def _mm(x, w, rows, tm, tn, out_dtype, name):
    k = x.shape[1]
    n = w.shape[1]
    return pl.pallas_call(
        _mm_body,
        out_shape=jax.ShapeDtypeStruct((rows, n), out_dtype),
        grid=(n // tn, rows // tm),
        in_specs=[
            pl.BlockSpec((tm, k), lambda j, i: (i, 0)),
            pl.BlockSpec((k, tn), lambda j, i: (0, j)),
        ],
        out_specs=pl.BlockSpec((tm, tn), lambda j, i: (i, j)),
        compiler_params=_cp("arbitrary", "arbitrary"),
        name=name,
    )(x, w)


def _mm_res_body(xl_ref, xc_ref, w_ref, res_ref, gate_ref, o_ref, *, nlat):
    i = pl.program_id(1)

    @pl.when(i < nlat)
    def _():
        o_ref[...] = res_ref[...] + gate_ref[...] * _dot(xl_ref[...], w_ref[...])

    @pl.when(i >= nlat)
    def _():
        o_ref[...] = res_ref[...] + gate_ref[...] * _dot(xc_ref[...], w_ref[...])


def _mm_res(a_lat, a_ctx, w, xs, mod3, which_gate, rows):
    tm, tn = 512, 1024
    k = a_lat.shape[1]
    nlat = LAT // tm
    return pl.pallas_call(
        functools.partial(_mm_res_body, nlat=nlat),
        out_shape=jax.ShapeDtypeStruct((rows, D), F32),
        grid=(D // tn, rows // tm),
        in_specs=[
            pl.BlockSpec((tm, k), lambda j, i: (jnp.minimum(i, nlat - 1), 0)),
            pl.BlockSpec((tm, k), lambda j, i: (jnp.maximum(i - nlat, 0), 0)),
            pl.BlockSpec((k, tn), lambda j, i: (0, j)),
            pl.BlockSpec((tm, tn), lambda j, i: (i, j)),
            pl.BlockSpec((None, 1, tn), lambda j, i: (_seg_of(i, tm) * N_MOD + which_gate, 0, j)),
        ],
        out_specs=pl.BlockSpec((tm, tn), lambda j, i: (i, j)),
        compiler_params=_cp("arbitrary", "arbitrary"),
        name="mm_residual",
    )(a_lat, a_ctx, w, xs, mod3)


def _log_sigmoid(a):
    return jnp.minimum(a, 0.0) - jnp.log1p(jnp.exp(-jnp.abs(a)))


PREP = 256


def _gla_body(k_ref, v_ref, q_ref, r_ref, z_ref, wgf_ref, bgf_ref, wgb_ref, bgb_ref, on_ref, s0f_ref, s0b_ref,
              o_ref, sf_ref, sb_ref,
              qf_s, kf_s, df_s, qb_s, kb_s, db_s, vb_s, dec_s, o_s, stf_s, stb_s, *, T, emit_o):
    nc = T // CHUNK
    nblk = T // PREP
    cpb = PREP // CHUNK
    scale = DK ** -0.5

    row = lax.broadcasted_iota(jnp.int32, (PREP, PREP), 0)
    col = lax.broadcasted_iota(jnp.int32, (PREP, PREP), 1)
    shift = CHUNK.bit_length() - 1
    same = lax.shift_right_logical(row, shift) == lax.shift_right_logical(col, shift)
    tri_f = jnp.where(same & (row >= col), 1.0, 0.0).astype(BF16)
    tri_b = jnp.where(same & (row <= col), 1.0, 0.0).astype(BF16)
    wf_hi, wf_lo = _split2(wgf_ref[...])
    wb_hi, wb_lo = _split2(wgb_ref[...])

    def chunk_sums(tri, g):
        g1 = g.astype(BF16)
        rem = g - g1.astype(F32)
        g2 = rem.astype(BF16)
        g3 = (rem - g2.astype(F32)).astype(BF16)
        return _dot(tri, g1) + _dot(tri, g2) + _dot(tri, g3)

    def edge_rows(G, e):
        return jnp.concatenate(
            [jnp.broadcast_to(G[c * CHUNK + e:c * CHUNK + e + 1, :], (CHUNK, DK)) for c in range(cpb)], axis=0)

    def prep(blk, carry):
        rows = pl.ds(pl.multiple_of(blk * PREP, PREP), PREP)
        z = z_ref[rows, :128]
        gf = _log_sigmoid(_dot3(z, wf_hi, wf_lo) + bgf_ref[...]) / GATE_NORM
        gb = _log_sigmoid(_dot3(z, wb_hi, wb_lo) + bgb_ref[...]) / GATE_NORM
        Gf = chunk_sums(tri_f, gf)
        Gb = chunk_sums(tri_b, gb)
        k = k_ref[rows, :]
        q = q_ref[rows, :] * scale
        qf_s[rows, :] = (q * jnp.exp(Gf)).astype(BF16)
        kf_s[rows, :] = (k * jnp.exp(-Gf)).astype(BF16)
        df_s[rows, :] = (k * jnp.exp(edge_rows(Gf, CHUNK - 1) - Gf)).astype(BF16)
        qb_s[rows, :] = (q * jnp.exp(Gb)).astype(BF16)
        kb_s[rows, :] = (k * jnp.exp(-Gb)).astype(BF16)
        db_s[rows, :] = (k * jnp.exp(edge_rows(Gb, 0) - Gb)).astype(BF16)
        vb_s[rows, :] = v_ref[rows, :].astype(BF16)
        for c in range(cpb):
            ef = Gf[c * CHUNK + CHUNK - 1:c * CHUNK + CHUNK, :]
            eb = Gb[c * CHUNK:c * CHUNK + 1, :]
            dec_s[0, blk * cpb + c] = jnp.broadcast_to(jnp.exp(ef), (8, DK))
            dec_s[1, blk * cpb + c] = jnp.broadcast_to(jnp.exp(eb), (8, DK))
        return carry

    lax.fori_loop(0, nblk, prep, 0)

    stf_s[...] = s0f_ref[0, 0]
    stb_s[...] = s0b_ref[0, 0]
    r64 = lax.broadcasted_iota(jnp.int32, (CHUNK, CHUNK), 0)
    c64 = lax.broadcasted_iota(jnp.int32, (CHUNK, CHUNK), 1)
    nt_dims = (((1,), (1,)), ((), ()))
    tn_dims = (((0,), (0,)), ((), ()))

    def chunk(c, q_s, k_s, d_s, st_s, di, keep):
        rows = pl.ds(pl.multiple_of(c * CHUNK, CHUNK), CHUNK)
        qe = q_s[rows, :]
        vb = vb_s[rows, :]
        st = st_s[...]
        o = None
        if emit_o:
            a = lax.dot_general(qe, k_s[rows, :], nt_dims, preferred_element_type=F32)
            a = jnp.where(keep, a, 0.0).astype(BF16)
            o = _dot(a, vb) + lax.dot_general(qe, st.astype(BF16), nt_dims, preferred_element_type=F32)
        upd = lax.dot_general(vb, d_s[rows, :], tn_dims, preferred_element_type=F32)
        st_s[...] = st * dec_s[di, c][0:1, :] + upd
        return o, rows

    def step(ci, carry, accumulate):
        of, rows_f = chunk(ci, qf_s, kf_s, df_s, stf_s, 0, r64 >= c64)
        ob, rows_b = chunk(nc - 1 - ci, qb_s, kb_s, db_s, stb_s, 1, r64 <= c64)
        if emit_o:
            if accumulate:
                o_s[rows_f, :] = o_s[rows_f, :] + of
                o_s[rows_b, :] = o_s[rows_b, :] + ob
            else:
                o_s[rows_f, :] = of
                o_s[rows_b, :] = ob
        return carry

    lax.fori_loop(0, nc // 2, functools.partial(step, accumulate=False), 0)
    lax.fori_loop(nc // 2, nc, functools.partial(step, accumulate=True), 0)
    sf_ref[0, 0] = stf_s[...]
    sb_ref[0, 0] = stb_s[...]

    if emit_o:
        def fin(blk, carry):
            rows = pl.ds(pl.multiple_of(blk * PREP, PREP), PREP)
            tot = o_s[rows, :]
            y = tot * lax.rsqrt(jnp.mean(tot * tot, axis=-1, keepdims=True) + EPS) * on_ref[...]
            o_ref[rows, :] = (y * _silu(r_ref[rows, :])).astype(o_ref.dtype)
            return carry

        lax.fori_loop(0, nblk, fin, 0)
    else:
        o_ref[...] = jnp.zeros_like(o_ref)


def _gla(proj, wgf, bgf, wgb, bgb, onorm, s0f, s0b, T, row_off, emit_o):
    rb = row_off // T
    st_spec = pl.BlockSpec((1, 1, DV, DK), lambda b, h: (b, h, 0, 0))
    return pl.pallas_call(
        functools.partial(_gla_body, T=T, emit_o=emit_o),
        out_shape=(
            jax.ShapeDtypeStruct((B * T, VV), BF16),
            jax.ShapeDtypeStruct((B, H, DV, DK), F32),
            jax.ShapeDtypeStruct((B, H, DV, DK), F32),
        ),
        grid=(B, H),
        in_specs=[
            pl.BlockSpec((T, DK), lambda b, h: (rb + b, COL_K // DK + h)),
            pl.BlockSpec((T, DV), lambda b, h: (rb + b, COL_V // DV + h)),
            pl.BlockSpec((T, DK), lambda b, h: (rb + b, COL_Q // DK + h)),
            pl.BlockSpec((T, DV), lambda b, h: (rb + b, COL_R // DV + h)),
            pl.BlockSpec((T, ZPAD), lambda b, h: (rb + b, COL_Z // ZPAD)),
            pl.BlockSpec((128, DK), lambda b, h: (0, h)),
            pl.BlockSpec((1, DK), lambda b, h: (0, h)),
            pl.BlockSpec((128, DK), lambda b, h: (0, h)),
            pl.BlockSpec((1, DK), lambda b, h: (0, h)),
            pl.BlockSpec((1, DV), lambda b, h: (0, 0)),
            st_spec,
            st_spec,
        ],
        out_specs=(
            pl.BlockSpec((T, DV), lambda b, h: (b, h)),
            st_spec,
            st_spec,
        ),
        scratch_shapes=[pltpu.VMEM((T, DK), BF16)] * 6 + [
            pltpu.VMEM((T, DV), BF16),
            pltpu.VMEM((2, T // CHUNK, 8, DK), F32),
            pltpu.VMEM((T, DV), F32),
            pltpu.VMEM((DV, DK), F32),
            pltpu.VMEM((DV, DK), F32),
        ],
        compiler_params=_cp("arbitrary", "arbitrary"),
        name="gla_scan",
    )(proj, proj, proj, proj, proj, wgf, bgf, wgb, bgb, onorm, s0f, s0b)


def _fnet_in_body(h_ref, w_ref, cs_ref, a_ref, b_ref):
    u = _dot(h_ref[...], w_ref[...]).astype(BF16)
    for g in range(FG):
        ab = _dot(u[:, g * FD:(g + 1) * FD], cs_ref[...])
        a_ref[:, g * FD:(g + 1) * FD] = ab[:, :FD].astype(BF16)
        b_ref[:, g * FD:(g + 1) * FD] = ab[:, FD:].astype(BF16)


def _fnet_in(h, w, cs, rows):
    tm = 512
    return pl.pallas_call(
        _fnet_in_body,
        out_shape=(jax.ShapeDtypeStruct((rows, D), BF16), jax.ShapeDtypeStruct((rows, D), BF16)),
        grid=(rows // tm,),
        in_specs=[
            pl.BlockSpec((tm, D), lambda i: (i, 0)),
            pl.BlockSpec((D, D), lambda i: (0, 0)),
            pl.BlockSpec((FD, 2 * FD), lambda i: (0, 0)),
        ],
        out_specs=(pl.BlockSpec((tm, D), lambda i: (i, 0)), pl.BlockSpec((tm, D), lambda i: (i, 0))),
        compiler_params=_cp("arbitrary"),
        name="fnet_in",
    )(h, w, cs)


def _fnet_time_body(ct_ref, st_ref, a_ref, b_ref, o_ref):
    o_ref[...] = (_dot(ct_ref[...], a_ref[...]) + _dot(st_ref[...], b_ref[...])).astype(o_ref.dtype)


def _fnet_time(ct, st, a, b, T, row_off):
    tm = min(T, 512)
    tn = 1024
    rb = row_off // T
    return pl.pallas_call(
        _fnet_time_body,
        out_shape=jax.ShapeDtypeStruct((B * T, D), BF16),
        grid=(B, D // tn, T // tm),
        in_specs=[
            pl.BlockSpec((tm, T), lambda s, j, i: (i, 0)),
            pl.BlockSpec((tm, T), lambda s, j, i: (i, 0)),
            pl.BlockSpec((T, tn), lambda s, j, i: (rb + s, j)),
            pl.BlockSpec((T, tn), lambda s, j, i: (rb + s, j)),
        ],
        out_specs=pl.BlockSpec((tm, tn), lambda s, j, i: (s * (T // tm) + i, j)),
        compiler_params=_cp("arbitrary", "arbitrary", "arbitrary"),
        name="fnet_time",
    )(ct, st, a, b)


def _dft_mats(n, scale):
    idx = np.arange(n, dtype=np.int64)
    ang = 2.0 * np.pi * ((idx[:, None] * idx[None, :]) % n).astype(np.float64) / n
    return np.cos(ang) * scale, np.sin(ang) * scale


def _row_copy(src_hbm, row, dst_vmem, r, sem):
    return pltpu.make_async_copy(src_hbm.at[pl.ds(row, 1)], dst_vmem.at[pl.ds(r, 1)], sem)


def _moe_body(bs_ref, tok_ref, h_hbm, wg_ref, wu_ref, wd_ref, y_hbm, xbuf, obuf, gsem, osem, wgb, wub, wdb, *, nb):
    e = pl.program_id(0)
    b0 = bs_ref[e]
    b1 = bs_ref[e + 1]
    total = bs_ref[ME]

    def gather(g):
        slot = g % 2
        base = g * TME
        for r in range(TME):
            _row_copy(h_hbm, tok_ref[base + r], xbuf.at[slot], r, gsem.at[slot]).start(priority=r % 2)

    def out_copy(g):
        slot = g % 2
        dst = y_hbm.at[pl.ds(pl.multiple_of(g * TME, TME), TME)]
        return pltpu.make_async_copy(obuf.at[slot], dst, osem.at[slot])

    @pl.when((e == 0) & (total > 0))
    def _():
        gather(0)

    @pl.when(b1 > b0)
    def _():
        wgb[...] = wg_ref[...].astype(BF16)
        wub[...] = wu_ref[...].astype(BF16)
        wdb[...] = wd_ref[...].astype(BF16)

    def block(g, carry):
        slot = g % 2

        @pl.when(g + 1 < total)
        def _():
            gather(g + 1)

        pltpu.make_async_copy(h_hbm.at[pl.ds(0, TME)], xbuf.at[slot], gsem.at[slot]).wait()
        x = xbuf[slot].astype(BF16)
        hmid = _silu(_dot(x, wgb[...])) * _dot(x, wub[...])
        y = _dot(hmid.astype(BF16), wdb[...])

        @pl.when(g >= 2)
        def _():
            out_copy(g - 2).wait()

        obuf[slot] = y
        out_copy(g).start()
        return carry

    lax.fori_loop(b0, b1, block, 0)

    @pl.when(e == ME - 1)
    def _():
        @pl.when(total >= 2)
        def _():
            out_copy(total - 2).wait()

        @pl.when(total >= 1)
        def _():
            out_copy(total - 1).wait()

        def zero_copy(g):
            dst = y_hbm.at[pl.ds(pl.multiple_of(g * TME, TME), TME)]
            return pltpu.make_async_copy(obuf.at[0], dst, osem.at[0])

        obuf[0] = jnp.zeros((TME, D), F32)
        lax.fori_loop(total, nb, lambda g, c: (zero_copy(g).start(), c)[1], 0)
        lax.fori_loop(total, nb, lambda g, c: (zero_copy(g).wait(), c)[1], 0)


def _moe_experts(layer, blk_start, buf_tok, h2, w_gate, w_up, w_down):
    nb = buf_tok.shape[0] // TME
    w_in_spec = pl.BlockSpec((None, None, D, MDE), lambda e, bs, tk: (layer, e, 0, 0))
    return pl.pallas_call(
        functools.partial(_moe_body, nb=nb),
        out_shape=jax.ShapeDtypeStruct((nb * TME, D), F32),
        grid_spec=pltpu.PrefetchScalarGridSpec(
            num_scalar_prefetch=2,
            grid=(ME,),
            in_specs=[
                pl.BlockSpec(memory_space=pl.ANY),
                w_in_spec,
                w_in_spec,
                pl.BlockSpec((None, None, MDE, D), lambda e, bs, tk: (layer, e, 0, 0)),
            ],
            out_specs=pl.BlockSpec(memory_space=pl.ANY),
            scratch_shapes=[
                pltpu.VMEM((2, TME, D), F32),
                pltpu.VMEM((2, TME, D), F32),
                pltpu.SemaphoreType.DMA((2,)),
                pltpu.SemaphoreType.DMA((2,)),
                pltpu.VMEM((D, MDE), BF16),
                pltpu.VMEM((D, MDE), BF16),
                pltpu.VMEM((MDE, D), BF16),
            ],
        ),
        compiler_params=_cp("arbitrary"),
        name="moe_experts",
    )(blk_start, buf_tok, h2, w_gate, w_up, w_down)


TMC = 128


def _combine_body(dest_ref, x_ref, gate_ref, wt_ref, y_hbm, o_ref, ybuf, sem, *, nt):
    i = pl.program_id(0)

    def gather(tile, slot):
        base = tile * (2 * TMC)

        for r in range(TMC):
            _row_copy(y_hbm, dest_ref[base + 2 * r], ybuf.at[slot, 0], r, sem.at[slot]).start(priority=0)
            _row_copy(y_hbm, dest_ref[base + 2 * r + 1], ybuf.at[slot, 1], r, sem.at[slot]).start(priority=1)

    @pl.when(i == 0)
    def _():
        gather(0, 0)

    @pl.when(i + 1 < nt)
    def _():
        gather(i + 1, (i + 1) % 2)

    slot = i % 2
    for k in range(2):
        pltpu.make_async_copy(y_hbm.at[pl.ds(0, TMC)], ybuf.at[slot, k], sem.at[slot]).wait()
    wt = wt_ref[...]
    y = wt[:, 0:1] * ybuf[slot, 0] + wt[:, 1:2] * ybuf[slot, 1]
    o_ref[...] = x_ref[...] + gate_ref[...] * y


def _combine(dest, xs, mod3, wt, yb, rows):
    nt = rows // TMC
    return pl.pallas_call(
        functools.partial(_combine_body, nt=nt),
        out_shape=jax.ShapeDtypeStruct((rows, D), F32),
        grid_spec=pltpu.PrefetchScalarGridSpec(
            num_scalar_prefetch=1,
            grid=(nt,),
            in_specs=[
                pl.BlockSpec((TMC, D), lambda i, d: (i, 0)),
                pl.BlockSpec((None, 1, D), lambda i, d: (_seg_of(i, TMC) * N_MOD + 5, 0, 0)),
                pl.BlockSpec((TMC, 128), lambda i, d: (i, 0)),
                pl.BlockSpec(memory_space=pl.ANY),
            ],
            out_specs=pl.BlockSpec((TMC, D), lambda i, d: (i, 0)),
            scratch_shapes=[
                pltpu.VMEM((2, 2, TMC, D), F32),
                pltpu.SemaphoreType.DMA((2,)),
            ],
        ),
        compiler_params=_cp("arbitrary"),
        name="moe_combine",
    )(dest, xs, mod3, wt, yb)


def _route_plan(eid, rows):
    a = rows * 2
    nb = -(-a // TME) + ME
    e_flat = eid.reshape(-1)
    onehot = (e_flat[:, None] == jnp.arange(ME, dtype=jnp.int32)[None, :]).astype(jnp.int32)
    csum = jnp.cumsum(onehot, axis=0)
    counts = csum[-1]
    rank = jnp.take_along_axis(csum, e_flat[:, None], axis=1)[:, 0] - 1
    padded = ((counts + TME - 1) // TME) * TME
    pad_end = jnp.cumsum(padded)
    pad_start = pad_end - padded
    dest = pad_start[e_flat] + rank
    tok = jnp.arange(a, dtype=jnp.int32) // 2
    buf_tok = jnp.zeros((nb * TME,), jnp.int32).at[dest].set(tok)
    blk_start = jnp.concatenate([jnp.zeros((1,), jnp.int32), (pad_end // TME).astype(jnp.int32)])
    return dest, buf_tok, blk_start


def _final_body(x_ref, g_ref, o_ref):
    x = x_ref[...]
    o_ref[...] = x * lax.rsqrt(jnp.mean(x * x, axis=-1, keepdims=True) + EPS) * g_ref[...]


def _final_norm(xs, g):
    tm = 256
    return pl.pallas_call(
        _final_body,
        out_shape=jax.ShapeDtypeStruct((LAT, D), F32),
        grid=(LAT // tm,),
        in_specs=[pl.BlockSpec((tm, D), lambda i: (i, 0)), pl.BlockSpec((1, D), lambda i: (0, 0))],
        out_specs=pl.BlockSpec((tm, D), lambda i: (i, 0)),
        compiler_params=_cp("arbitrary"),
        name="final_norm",
    )(xs, g.reshape(1, D))


def _sincos_2d(rows, cols, d):
    quarter = d // 4
    omega = 1.0 / (POS_BASE ** (jnp.arange(quarter, dtype=F32) / quarter))

    def axis_emb(n):
        p = jnp.arange(n, dtype=F32)[:, None] * omega[None, :]
        return jnp.concatenate([jnp.sin(p), jnp.cos(p)], axis=-1)

    er, ec = axis_emb(rows), axis_emb(cols)
    half = 2 * quarter
    pos = jnp.concatenate([jnp.broadcast_to(er[:, None, :], (rows, cols, half)),
                           jnp.broadcast_to(ec[None, :, :], (rows, cols, half))], axis=-1)
    return pos.reshape(rows * cols, 2 * half)


def _gla_weights(w_in, wg_f, wg_b):
    z = jnp.zeros((D, ZPAD - 2 * RANK), F32)
    w_cat = jnp.concatenate([
        w_in[:, :QK], w_in[:, QK:QK + VV],
        w_in[:, STATE_COLS:STATE_COLS + QK], w_in[:, STATE_COLS + QK:],
        w_in[:, QK + VV:STATE_COLS], z], axis=1).astype(BF16)
    wgf = jnp.zeros((128, QK), F32).at[:RANK].set(wg_f)
    wgb = jnp.zeros((128, QK), F32).at[RANK:2 * RANK].set(wg_b)
    return w_cat, wgf, wgb


def kernel(x, c, ctx, c_ctx, ada_w, ada_b, norm1_g, norm2_g, gla_w_in, gla_wg_f, gla_bg_f, gla_wg_b, gla_bg_b,
           gla_onorm_g, gla_w_out, fnet_w_in, fnet_w_out, moe_rw_group, moe_rb_group, moe_rw_expert,
           moe_rb_expert, moe_w_gate, moe_w_up, moe_w_down, final_g):
    cvec = jnp.zeros((SEGS, D), F32).at[:B].set(c).at[CTX_SEG].set(c_ctx)
    mods = _mod_all(cvec, ada_w, ada_b)
    pos = _sincos_2d(SEQ // GRID_W, GRID_W, D)
    xs = _assemble(x.reshape(LAT, D), pos, ctx.reshape(NCTX, D))

    cc, sc = _dft_mats(FD, 1.0)
    cs = jnp.asarray(np.concatenate([cc, sc], axis=1), BF16)
    ct_l, st_l = _dft_mats(SEQ, (SEQ * FD) ** -0.5)
    ct_c, st_c = _dft_mats(CTX, (CTX * FD) ** -0.5)
    ct_l, st_l = jnp.asarray(ct_l, BF16), jnp.asarray(-st_l, BF16)
    ct_c, st_c = jnp.asarray(ct_c, BF16), jnp.asarray(-st_c, BF16)
    s_zero = jnp.zeros((B, H, DV, DK), F32)
    last_reader = ((DEPTH - 1) // 2) * 2

    for i in range(DEPTH):
        kind, j = i % 2, i // 2
        ctx_live = i < last_reader
        ctx_needed = i <= last_reader
        mod3 = mods[i].reshape(SEGS * N_MOD, 1, D)
        rows_in = ROWS if ctx_needed else LAT
        rows_out = ROWS if ctx_live else LAT
        h = _norm_mod(xs, norm1_g[i], mod3, 0, rows_in)

        if kind == 0:
            w_cat, wgf, wgb = _gla_weights(gla_w_in[j], gla_wg_f[j], gla_wg_b[j])
            bgf, bgb = gla_bg_f[j].reshape(1, QK), gla_bg_b[j].reshape(1, QK)
            onorm = gla_onorm_g[j].reshape(1, DV)
            proj = _mm(h, w_cat, rows_in, 512, 1280, F32, "gla_proj")
            o_c, sf, sb = _gla(proj, wgf, bgf, wgb, bgb, onorm, s_zero, s_zero, CTX, LAT, ctx_live)
            o_l, _, _ = _gla(proj, wgf, bgf, wgb, bgb, onorm, sf, sb, SEQ, 0, True)
            xs = _mm_res(o_l, o_c, gla_w_out[j].astype(BF16), xs, mod3, 2, rows_out)
        else:
            a, b = _fnet_in(h, fnet_w_in[j].astype(BF16), cs, rows_in)
            y = _fnet_time(ct_l, st_l, a, b, SEQ, 0)
            y_c = _fnet_time(ct_c, st_c, a, b, CTX, LAT) if ctx_live else y
            xs = _mm_res(y, y_c, fnet_w_out[j].astype(BF16), xs, mod3, 2, rows_out)

        rw = jnp.zeros((D, 128), F32).at[:, :MG].set(moe_rw_group[i])
        rw = rw.at[:, MG:MG + ME].set(jnp.transpose(moe_rw_expert[i], (1, 0, 2)).reshape(D, ME))
        rb = jnp.zeros((1, 128), F32).at[0, :MG].set(moe_rb_group[i]).at[0, MG:MG + ME].set(
            moe_rb_expert[i].reshape(ME))
        rw_hi = rw.astype(BF16)
        rw_lo = (rw - rw_hi.astype(F32)).astype(BF16)
        h2, eid, wt = _norm_route(xs, norm2_g[i], mod3, rw_hi, rw_lo, rb, rows_out)
        dest, buf_tok, blk_start = _route_plan(eid[:, :2], rows_out)
        yb = _moe_experts(i, blk_start, buf_tok, h2, moe_w_gate, moe_w_up, moe_w_down)
        xs = _combine(dest, xs, mod3, wt, yb, rows_out)

    return _final_norm(xs, final_g).reshape(B, SEQ, D)
```

```python
import functools

import jax
import jax.numpy as jnp
import numpy as np
from jax import lax
from jax.experimental import pallas as pl
from jax.experimental.pallas import tpu as pltpu

F32 = jnp.float32
BF16 = jnp.bfloat16

D = 2048
B = 4
SEQ = 2048
CTX = 256
DEPTH = 4
GRID_W = 64
EPS = 1e-6
POS_BASE = 10000.0
N_MOD = 6

H = 4
DK = 256
DV = 512
QK = H * DK
VV = H * DV
RANK = 16
GATE_NORM = 16.0
CHUNK = 64
STATE_COLS = QK + VV + 2 * RANK
ZPAD = 256
PROJ_N = QK + VV + QK + VV + ZPAD
COL_K, COL_V, COL_Q, COL_R, COL_Z = 0, QK, QK + VV, 2 * QK + VV, 2 * QK + 2 * VV

FG = 4
FD = D // FG

MG = 4
MPG = 8
ME = MG * MPG
MDE = D // 4
TME = 256

LAT = B * SEQ
NCTX = B * CTX
ROWS = LAT + NCTX
SEGS = 8
CTX_SEG = B

VMEM_LIMIT = 56 * 1024 * 1024


def _cp(*sem):
    return pltpu.CompilerParams(dimension_semantics=sem, vmem_limit_bytes=VMEM_LIMIT)


def _seg_of(i, tm):
    return jnp.where(i < LAT // tm, i // (SEQ // tm), CTX_SEG)


def _split2(a):
    hi = a.astype(BF16)
    lo = (a - hi.astype(F32)).astype(BF16)
    return hi, lo


def _dot(a, b):
    return jnp.dot(a, b, preferred_element_type=F32)


def _dot3(a, b_hi, b_lo):
    a_hi, a_lo = _split2(a)
    return _dot(a_hi, b_hi) + _dot(a_lo, b_hi) + _dot(a_hi, b_lo)


def _silu(a):
    return a * jax.nn.sigmoid(a)


HALF = D // 2


def _pack_rows(a):
    lo = lax.bitcast_convert_type(a[:, :HALF].astype(BF16).astype(F32), jnp.uint32)
    hi = lax.bitcast_convert_type(a[:, HALF:].astype(BF16).astype(F32), jnp.uint32)
    return (hi & jnp.uint32(0xFFFF0000)) | (lo >> 16)


def _unpack_rows(w):
    lo = lax.bitcast_convert_type(w << 16, F32)
    hi = lax.bitcast_convert_type(w & jnp.uint32(0xFFFF0000), F32)
    return lo, hi


def _mod_body(c_ref, w_ref, b_ref, o_ref):
    s = _silu(c_ref[...])
    w_hi, w_lo = _split2(w_ref[0])
    o_ref[0] = _dot3(s, w_hi, w_lo) + b_ref[0]


def _mod_all(cvec, ada_w, ada_b):
    tn = 512
    return pl.pallas_call(
        _mod_body,
        out_shape=jax.ShapeDtypeStruct((DEPTH, SEGS, N_MOD * D), F32),
        grid=(DEPTH, N_MOD * D // tn),
        in_specs=[
            pl.BlockSpec((SEGS, D), lambda l, j: (0, 0)),
            pl.BlockSpec((1, D, tn), lambda l, j: (l, 0, j)),
            pl.BlockSpec((1, 1, tn), lambda l, j: (l, 0, j)),
        ],
        out_specs=pl.BlockSpec((1, SEGS, tn), lambda l, j: (l, 0, j)),
        compiler_params=_cp("arbitrary", "arbitrary"),
        name="adaln_mod",
    )(cvec, ada_w, ada_b.reshape(DEPTH, 1, N_MOD * D))


def _assemble_body(x_ref, pos_ref, ctx_ref, o_ref, *, nlat):
    i = pl.program_id(0)

    @pl.when(i < nlat)
    def _():
        o_ref[...] = x_ref[...] + pos_ref[...]

    @pl.when(i >= nlat)
    def _():
        o_ref[...] = ctx_ref[...]


def _assemble(x2, pos, ctx2):
    tm = 256
    nlat = LAT // tm
    return pl.pallas_call(
        functools.partial(_assemble_body, nlat=nlat),
        out_shape=jax.ShapeDtypeStruct((ROWS, D), F32),
        grid=(ROWS // tm,),
        in_specs=[
            pl.BlockSpec((tm, D), lambda i: (jnp.minimum(i, nlat - 1), 0)),
            pl.BlockSpec((tm, D), lambda i: (i % (SEQ // tm), 0)),
            pl.BlockSpec((tm, D), lambda i: (jnp.maximum(i - nlat, 0), 0)),
        ],
        out_specs=pl.BlockSpec((tm, D), lambda i: (i, 0)),
        compiler_params=_cp("arbitrary"),
        name="assemble_stream",
    )(x2, pos, ctx2)


def _norm_mod_val(x, g, sh, sc):
    y = x * lax.rsqrt(jnp.mean(x * x, axis=-1, keepdims=True) + EPS) * g
    return y * (1.0 + sc) + sh


def _norm_mod_body(x_ref, g_ref, sh_ref, sc_ref, o_ref):
    o_ref[...] = _norm_mod_val(x_ref[...], g_ref[...], sh_ref[...], sc_ref[...]).astype(o_ref.dtype)


def _mod_spec(which, tm):
    return pl.BlockSpec((None, 1, D), lambda i: (_seg_of(i, tm) * N_MOD + which, 0, 0))


def _norm_mod(xs, g, mod3, which_shift, rows):
    tm = 256
    return pl.pallas_call(
        _norm_mod_body,
        out_shape=jax.ShapeDtypeStruct((rows, D), BF16),
        grid=(rows // tm,),
        in_specs=[
            pl.BlockSpec((tm, D), lambda i: (i, 0)),
            pl.BlockSpec((1, D), lambda i: (0, 0)),
            _mod_spec(which_shift, tm),
            _mod_spec(which_shift + 1, tm),
        ],
        out_specs=pl.BlockSpec((tm, D), lambda i: (i, 0)),
        compiler_params=_cp("arbitrary"),
        name="norm_mod",
    )(xs, g.reshape(1, D), mod3, mod3)


def _norm_route_body(x_ref, g_ref, sh_ref, sc_ref, rwh_ref, rwl_ref, rb_ref, h_ref, eid_ref, wt_ref):
    h2 = _norm_mod_val(x_ref[...], g_ref[...], sh_ref[...], sc_ref[...])
    h_ref[...] = _pack_rows(h2)
    lg = _dot3(h2, rwh_ref[...], rwl_ref[...]) + rb_ref[...]
    lane_i = lax.broadcasted_iota(jnp.int32, lg.shape, 1)
    lane = lane_i.astype(F32)
    neg = jnp.float32(-jnp.inf)
    big = jnp.float32(1024.0)

    glog = jnp.where(lane < MG, lg, neg)
    gmax = jnp.max(glog, axis=-1, keepdims=True)
    gidx = jnp.min(jnp.where(glog == gmax, lane, big), axis=-1, keepdims=True)
    g_w = 1.0 / jnp.sum(jnp.exp(glog - gmax), axis=-1, keepdims=True)

    lo = MG + MPG * gidx
    el = jnp.where((lane >= lo) & (lane < lo + MPG), lg, neg)
    m1 = jnp.max(el, axis=-1, keepdims=True)
    i1 = jnp.min(jnp.where(el == m1, lane, big), axis=-1, keepdims=True)
    el2 = jnp.where(lane == i1, neg, el)
    m2 = jnp.max(el2, axis=-1, keepdims=True)
    i2 = jnp.min(jnp.where(el2 == m2, lane, big), axis=-1, keepdims=True)
    e2 = jnp.exp(m2 - m1)
    den = 1.0 / (1.0 + e2)
    w1 = g_w * den
    w2 = g_w * (e2 * den)
    eid_ref[...] = jnp.where(lane_i == 0, i1 - MG, jnp.where(lane_i == 1, i2 - MG, 0.0)).astype(jnp.int32)
    wt_ref[...] = jnp.where(lane_i == 0, w1, jnp.where(lane_i == 1, w2, 0.0))


def _norm_route(xs, g, mod3, rw_hi, rw_lo, rb, rows):
    tm = 256
    return pl.pallas_call(
        _norm_route_body,
        out_shape=(
            jax.ShapeDtypeStruct((rows, HALF), jnp.uint32),
            jax.ShapeDtypeStruct((rows, 128), jnp.int32),
            jax.ShapeDtypeStruct((rows, 128), F32),
        ),
        grid=(rows // tm,),
        in_specs=[
            pl.BlockSpec((tm, D), lambda i: (i, 0)),
            pl.BlockSpec((1, D), lambda i: (0, 0)),
            _mod_spec(3, tm),
            _mod_spec(4, tm),
            pl.BlockSpec((D, 128), lambda i: (0, 0)),
            pl.BlockSpec((D, 128), lambda i: (0, 0)),
            pl.BlockSpec((1, 128), lambda i: (0, 0)),
        ],
        out_specs=(
            pl.BlockSpec((tm, HALF), lambda i: (i, 0)),
            pl.BlockSpec((tm, 128), lambda i: (i, 0)),
            pl.BlockSpec((tm, 128), lambda i: (i, 0)),
        ),
        compiler_params=_cp("arbitrary"),
        name="norm_route",
    )(xs, g.reshape(1, D), mod3, mod3, rw_hi, rw_lo, rb)


def _mm_body(x_ref, w_ref, o_ref):
    o_ref[...] = _dot(x_ref[...], w_ref[...]).astype(o_ref.dtype)


def _mm(x, w, rows, tm, tn, out_dtype, name):
    k = x.shape[1]
    n = w.shape[1]
    return pl.pallas_call(
        _mm_body,
        out_shape=jax.ShapeDtypeStruct((rows, n), out_dtype),
        grid=(n // tn, rows // tm),
        in_specs=[
            pl.BlockSpec((tm, k), lambda j, i: (i, 0)),
            pl.BlockSpec((k, tn), lambda j, i: (0, j)),
        ],
        out_specs=pl.BlockSpec((tm, tn), lambda j, i: (i, j)),
        compiler_params=_cp("arbitrary", "arbitrary"),
        name=name,
    )(x, w)


def _mm_res_body(xl_ref, xc_ref, w_ref, res_ref, gate_ref, o_ref, *, nlat):
    i = pl.program_id(1)

    @pl.when(i < nlat)
    def _():
        o_ref[...] = res_ref[...] + gate_ref[...] * _dot(xl_ref[...], w_ref[...])

    @pl.when(i >= nlat)
    def _():
        o_ref[...] = res_ref[...] + gate_ref[...] * _dot(xc_ref[...], w_ref[...])


def _mm_res(a_lat, a_ctx, w, xs, mod3, which_gate, rows):
    tm, tn = 512, 1024
    k = a_lat.shape[1]
    nlat = LAT // tm
    return pl.pallas_call(
        functools.partial(_mm_res_body, nlat=nlat),
        out_shape=jax.ShapeDtypeStruct((rows, D), F32),
        grid=(D // tn, rows // tm),
        in_specs=[
            pl.BlockSpec((tm, k), lambda j, i: (jnp.minimum(i, nlat - 1), 0)),
            pl.BlockSpec((tm, k), lambda j, i: (jnp.maximum(i - nlat, 0), 0)),
            pl.BlockSpec((k, tn), lambda j, i: (0, j)),
            pl.BlockSpec((tm, tn), lambda j, i: (i, j)),
            pl.BlockSpec((None, 1, tn), lambda j, i: (_seg_of(i, tm) * N_MOD + which_gate, 0, j)),
        ],
        out_specs=pl.BlockSpec((tm, tn), lambda j, i: (i, j)),
        compiler_params=_cp("arbitrary", "arbitrary"),
        name="mm_residual",
    )(a_lat, a_ctx, w, xs, mod3)


def _log_sigmoid(a):
    return jnp.minimum(a, 0.0) - jnp.log1p(jnp.exp(-jnp.abs(a)))


PREP = 256


def _gla_body(k_ref, v_ref, q_ref, r_ref, z_ref, wgf_ref, bgf_ref, wgb_ref, bgb_ref, on_ref, s0f_ref, s0b_ref,
              o_ref, sf_ref, sb_ref,
              qf_s, kf_s, df_s, qb_s, kb_s, db_s, vb_s, dec_s, o_s, stf_s, stb_s, *, T, emit_o):
    nc = T // CHUNK
    nblk = T // PREP
    cpb = PREP // CHUNK
    scale = DK ** -0.5

    row = lax.broadcasted_iota(jnp.int32, (PREP, PREP), 0)
    col = lax.broadcasted_iota(jnp.int32, (PREP, PREP), 1)
    shift = CHUNK.bit_length() - 1
    same = lax.shift_right_logical(row, shift) == lax.shift_right_logical(col, shift)
    tri_f = jnp.where(same & (row >= col), 1.0, 0.0).astype(BF16)
    tri_b = jnp.where(same & (row <= col), 1.0, 0.0).astype(BF16)
    wf_hi, wf_lo = _split2(wgf_ref[...])
    wb_hi, wb_lo = _split2(wgb_ref[...])

    def chunk_sums(tri, g):
        g1 = g.astype(BF16)
        rem = g - g1.astype(F32)
        g2 = rem.astype(BF16)
        g3 = (rem - g2.astype(F32)).astype(BF16)
        return _dot(tri, g1) + _dot(tri, g2) + _dot(tri, g3)

    def edge_rows(G, e):
        return jnp.concatenate(
            [jnp.broadcast_to(G[c * CHUNK + e:c * CHUNK + e + 1, :], (CHUNK, DK)) for c in range(cpb)], axis=0)

    def prep(blk, carry):
        rows = pl.ds(pl.multiple_of(blk * PREP, PREP), PREP)
        z = z_ref[rows, :128]
        gf = _log_sigmoid(_dot3(z, wf_hi, wf_lo) + bgf_ref[...]) / GATE_NORM
        gb = _log_sigmoid(_dot3(z, wb_hi, wb_lo) + bgb_ref[...]) / GATE_NORM
        Gf = chunk_sums(tri_f, gf)
        Gb = chunk_sums(tri_b, gb)
        k = k_ref[rows, :]
        q = q_ref[rows, :] * scale
        qf_s[rows, :] = (q * jnp.exp(Gf)).astype(BF16)
        kf_s[rows, :] = (k * jnp.exp(-Gf)).astype(BF16)
        df_s[rows, :] = (k * jnp.exp(edge_rows(Gf, CHUNK - 1) - Gf)).astype(BF16)
        qb_s[rows, :] = (q * jnp.exp(Gb)).astype(BF16)
        kb_s[rows, :] = (k * jnp.exp(-Gb)).astype(BF16)
        db_s[rows, :] = (k * jnp.exp(edge_rows(Gb, 0) - Gb)).astype(BF16)
        vb_s[rows, :] = v_ref[rows, :].astype(BF16)
        for c in range(cpb):
            ef = Gf[c * CHUNK + CHUNK - 1:c * CHUNK + CHUNK, :]
            eb = Gb[c * CHUNK:c * CHUNK + 1, :]
            dec_s[0, blk * cpb + c] = jnp.broadcast_to(jnp.exp(ef), (8, DK))
            dec_s[1, blk * cpb + c] = jnp.broadcast_to(jnp.exp(eb), (8, DK))
        return carry

    lax.fori_loop(0, nblk, prep, 0)

    stf_s[...] = s0f_ref[0, 0]
    stb_s[...] = s0b_ref[0, 0]
    r64 = lax.broadcasted_iota(jnp.int32, (CHUNK, CHUNK), 0)
    c64 = lax.broadcasted_iota(jnp.int32, (CHUNK, CHUNK), 1)
    nt_dims = (((1,), (1,)), ((), ()))
    tn_dims = (((0,), (0,)), ((), ()))

    def chunk(c, q_s, k_s, d_s, st_s, di, keep):
        rows = pl.ds(pl.multiple_of(c * CHUNK, CHUNK), CHUNK)
        qe = q_s[rows, :]
        vb = vb_s[rows, :]
        st = st_s[...]
        o = None
        if emit_o:
            a = lax.dot_general(qe, k_s[rows, :], nt_dims, preferred_element_type=F32)
            a = jnp.where(keep, a, 0.0).astype(BF16)
            o = _dot(a, vb) + lax.dot_general(qe, st.astype(BF16), nt_dims, preferred_element_type=F32)
        upd = lax.dot_general(vb, d_s[rows, :], tn_dims, preferred_element_type=F32)
        st_s[...] = st * dec_s[di, c][0:1, :] + upd
        return o, rows

    def step(ci, carry, accumulate):
        of, rows_f = chunk(ci, qf_s, kf_s, df_s, stf_s, 0, r64 >= c64)
        ob, rows_b = chunk(nc - 1 - ci, qb_s, kb_s, db_s, stb_s, 1, r64 <= c64)
        if emit_o:
            if accumulate:
                o_s[rows_f, :] = o_s[rows_f, :] + of
                o_s[rows_b, :] = o_s[rows_b, :] + ob
            else:
                o_s[rows_f, :] = of
                o_s[rows_b, :] = ob
        return carry

    lax.fori_loop(0, nc // 2, functools.partial(step, accumulate=False), 0, unroll=2)
    lax.fori_loop(nc // 2, nc, functools.partial(step, accumulate=True), 0, unroll=2)
    sf_ref[0, 0] = stf_s[...]
    sb_ref[0, 0] = stb_s[...]

    if emit_o:
        def fin(blk, carry):
            rows = pl.ds(pl.multiple_of(blk * PREP, PREP), PREP)
            tot = o_s[rows, :]
            y = tot * lax.rsqrt(jnp.mean(tot * tot, axis=-1, keepdims=True) + EPS) * on_ref[...]
            o_ref[rows, :] = (y * _silu(r_ref[rows, :])).astype(o_ref.dtype)
            return carry

        lax.fori_loop(0, nblk, fin, 0)
    else:
        o_ref[...] = jnp.zeros_like(o_ref)


def _gla(proj, wgf, bgf, wgb, bgb, onorm, s0f, s0b, T, row_off, emit_o):
    rb = row_off // T
    st_spec = pl.BlockSpec((1, 1, DV, DK), lambda b, h: (b, h, 0, 0))
    return pl.pallas_call(
        functools.partial(_gla_body, T=T, emit_o=emit_o),
        out_shape=(
            jax.ShapeDtypeStruct((B * T, VV), BF16),
            jax.ShapeDtypeStruct((B, H, DV, DK), F32),
            jax.ShapeDtypeStruct((B, H, DV, DK), F32),
        ),
        grid=(B, H),
        in_specs=[
            pl.BlockSpec((T, DK), lambda b, h: (rb + b, COL_K // DK + h)),
            pl.BlockSpec((T, DV), lambda b, h: (rb + b, COL_V // DV + h)),
            pl.BlockSpec((T, DK), lambda b, h: (rb + b, COL_Q // DK + h)),
            pl.BlockSpec((T, DV), lambda b, h: (rb + b, COL_R // DV + h)),
            pl.BlockSpec((T, ZPAD), lambda b, h: (rb + b, COL_Z // ZPAD)),
            pl.BlockSpec((128, DK), lambda b, h: (0, h)),
            pl.BlockSpec((1, DK), lambda b, h: (0, h)),
            pl.BlockSpec((128, DK), lambda b, h: (0, h)),
            pl.BlockSpec((1, DK), lambda b, h: (0, h)),
            pl.BlockSpec((1, DV), lambda b, h: (0, 0)),
            st_spec,
            st_spec,
        ],
        out_specs=(
            pl.BlockSpec((T, DV), lambda b, h: (b, h)),
            st_spec,
            st_spec,
        ),
        scratch_shapes=[pltpu.VMEM((T, DK), BF16)] * 6 + [
            pltpu.VMEM((T, DV), BF16),
            pltpu.VMEM((2, T // CHUNK, 8, DK), F32),
            pltpu.VMEM((T, DV), F32),
            pltpu.VMEM((DV, DK), F32),
            pltpu.VMEM((DV, DK), F32),
        ],
        compiler_params=_cp("arbitrary", "arbitrary"),
        name="gla_scan",
    )(proj, proj, proj, proj, proj, wgf, bgf, wgb, bgb, onorm, s0f, s0b)


def _fnet_in_body(h_ref, w_ref, cs_ref, a_ref, b_ref):
    u = _dot(h_ref[...], w_ref[...]).astype(BF16)
    for g in range(FG):
        ab = _dot(u[:, g * FD:(g + 1) * FD], cs_ref[...])
        a_ref[:, g * FD:(g + 1) * FD] = ab[:, :FD].astype(BF16)
        b_ref[:, g * FD:(g + 1) * FD] = ab[:, FD:].astype(BF16)


def _fnet_in(h, w, cs, rows):
    tm = 512
    return pl.pallas_call(
        _fnet_in_body,
        out_shape=(jax.ShapeDtypeStruct((rows, D), BF16), jax.ShapeDtypeStruct((rows, D), BF16)),
        grid=(rows // tm,),
        in_specs=[
            pl.BlockSpec((tm, D), lambda i: (i, 0)),
            pl.BlockSpec((D, D), lambda i: (0, 0)),
            pl.BlockSpec((FD, 2 * FD), lambda i: (0, 0)),
        ],
        out_specs=(pl.BlockSpec((tm, D), lambda i: (i, 0)), pl.BlockSpec((tm, D), lambda i: (i, 0))),
        compiler_params=_cp("arbitrary"),
        name="fnet_in",
    )(h, w, cs)


def _fnet_time_body(ct_ref, st_ref, a_ref, b_ref, o_ref):
    o_ref[...] = (_dot(ct_ref[...], a_ref[...]) + _dot(st_ref[...], b_ref[...])).astype(o_ref.dtype)


def _fnet_time(ct, st, a, b, T, row_off):
    tm = min(T, 512)
    tn = 1024
    rb = row_off // T
    return pl.pallas_call(
        _fnet_time_body,
        out_shape=jax.ShapeDtypeStruct((B * T, D), BF16),
        grid=(B, D // tn, T // tm),
        in_specs=[
            pl.BlockSpec((tm, T), lambda s, j, i: (i, 0)),
            pl.BlockSpec((tm, T), lambda s, j, i: (i, 0)),
            pl.BlockSpec((T, tn), lambda s, j, i: (rb + s, j)),
            pl.BlockSpec((T, tn), lambda s, j, i: (rb + s, j)),
        ],
        out_specs=pl.BlockSpec((tm, tn), lambda s, j, i: (s * (T // tm) + i, j)),
        compiler_params=_cp("arbitrary", "arbitrary", "arbitrary"),
        name="fnet_time",
    )(ct, st, a, b)


def _dft_mats(n, scale):
    idx = np.arange(n, dtype=np.int64)
    ang = 2.0 * np.pi * ((idx[:, None] * idx[None, :]) % n).astype(np.float64) / n
    return np.cos(ang) * scale, np.sin(ang) * scale


def _row_copy(src_hbm, row, dst_vmem, r, sem):
    return pltpu.make_async_copy(src_hbm.at[pl.ds(row, 1)], dst_vmem.at[pl.ds(r, 1)], sem)


def _moe_body(bs_ref, tok_ref, h_hbm, wg_ref, wu_ref, wd_ref, y_hbm, xbuf, obuf, gsem, osem, wgb, wub, wdb, *, nb):
    e = pl.program_id(0)
    b0 = bs_ref[e]
    b1 = bs_ref[e + 1]
    total = bs_ref[ME]

    def gather(g):
        slot = g % 2
        base = g * TME
        for r in range(TME):
            _row_copy(h_hbm, tok_ref[base + r], xbuf.at[slot], r, gsem.at[slot]).start(priority=r % 2)

    def out_copy(g):
        slot = g % 2
        dst = y_hbm.at[pl.ds(pl.multiple_of(g * TME, TME), TME)]
        return pltpu.make_async_copy(obuf.at[slot], dst, osem.at[slot])

    @pl.when((e == 0) & (total > 0))
    def _():
        gather(0)

    @pl.when(b1 > b0)
    def _():
        wgb[...] = wg_ref[...].astype(BF16)
        wub[...] = wu_ref[...].astype(BF16)
        wdb[...] = wd_ref[...].astype(BF16)

    def block(g, carry):
        slot = g % 2

        @pl.when(g + 1 < total)
        def _():
            gather(g + 1)

        pltpu.make_async_copy(h_hbm.at[pl.ds(0, TME)], xbuf.at[slot], gsem.at[slot]).wait()
        x_lo, x_hi = _unpack_rows(xbuf[slot])
        x = jnp.concatenate([x_lo.astype(BF16), x_hi.astype(BF16)], axis=1)
        hmid = _silu(_dot(x, wgb[...])) * _dot(x, wub[...])
        y = _pack_rows(_dot(hmid.astype(BF16), wdb[...]))

        @pl.when(g >= 2)
        def _():
            out_copy(g - 2).wait()

        obuf[slot] = y
        out_copy(g).start()
        return carry

    lax.fori_loop(b0, b1, block, 0)

    @pl.when(e == ME - 1)
    def _():
        @pl.when(total >= 2)
        def _():
            out_copy(total - 2).wait()

        @pl.when(total >= 1)
        def _():
            out_copy(total - 1).wait()

        def zero_copy(g):
            dst = y_hbm.at[pl.ds(pl.multiple_of(g * TME, TME), TME)]
            return pltpu.make_async_copy(obuf.at[0], dst, osem.at[0])

        obuf[0] = jnp.zeros((TME, HALF), jnp.uint32)
        lax.fori_loop(total, nb, lambda g, c: (zero_copy(g).start(), c)[1], 0)
        lax.fori_loop(total, nb, lambda g, c: (zero_copy(g).wait(), c)[1], 0)


def _moe_experts(layer, blk_start, buf_tok, h2, w_gate, w_up, w_down):
    nb = buf_tok.shape[0] // TME
    w_in_spec = pl.BlockSpec((None, None, D, MDE), lambda e, bs, tk: (layer, e, 0, 0))
    return pl.pallas_call(
        functools.partial(_moe_body, nb=nb),
        out_shape=jax.ShapeDtypeStruct((nb * TME, HALF), jnp.uint32),
        grid_spec=pltpu.PrefetchScalarGridSpec(
            num_scalar_prefetch=2,
            grid=(ME,),
            in_specs=[
                pl.BlockSpec(memory_space=pl.ANY),
                w_in_spec,
                w_in_spec,
                pl.BlockSpec((None, None, MDE, D), lambda e, bs, tk: (layer, e, 0, 0)),
            ],
            out_specs=pl.BlockSpec(memory_space=pl.ANY),
            scratch_shapes=[
                pltpu.VMEM((2, TME, HALF), jnp.uint32),
                pltpu.VMEM((2, TME, HALF), jnp.uint32),
                pltpu.SemaphoreType.DMA((2,)),
                pltpu.SemaphoreType.DMA((2,)),
                pltpu.VMEM((D, MDE), BF16),
                pltpu.VMEM((D, MDE), BF16),
                pltpu.VMEM((MDE, D), BF16),
            ],
        ),
        compiler_params=_cp("arbitrary"),
        name="moe_experts",
    )(blk_start, buf_tok, h2, w_gate, w_up, w_down)


TMC = 128


def _combine_body(dest_ref, x_ref, gate_ref, wt_ref, y_hbm, o_ref, ybuf, sem, *, nt):
    i = pl.program_id(0)

    def gather(tile, slot):
        base = tile * (2 * TMC)

        for r in range(TMC):
            _row_copy(y_hbm, dest_ref[base + 2 * r], ybuf.at[slot, 0], r, sem.at[slot]).start(priority=0)
            _row_copy(y_hbm, dest_ref[base + 2 * r + 1], ybuf.at[slot, 1], r, sem.at[slot]).start(priority=1)

    @pl.when(i == 0)
    def _():
        gather(0, 0)

    @pl.when(i + 1 < nt)
    def _():
        gather(i + 1, (i + 1) % 2)

    slot = i % 2
    for k in range(2):
        pltpu.make_async_copy(y_hbm.at[pl.ds(0, TMC)], ybuf.at[slot, k], sem.at[slot]).wait()
    wt = wt_ref[...]
    w0, w1 = wt[:, 0:1], wt[:, 1:2]
    lo0, hi0 = _unpack_rows(ybuf[slot, 0])
    lo1, hi1 = _unpack_rows(ybuf[slot, 1])
    o_ref[:, :HALF] = x_ref[:, :HALF] + gate_ref[:, :HALF] * (w0 * lo0 + w1 * lo1)
    o_ref[:, HALF:] = x_ref[:, HALF:] + gate_ref[:, HALF:] * (w0 * hi0 + w1 * hi1)


def _combine(dest, xs, mod3, wt, yb, rows):
    nt = rows // TMC
    return pl.pallas_call(
        functools.partial(_combine_body, nt=nt),
        out_shape=jax.ShapeDtypeStruct((rows, D), F32),
        grid_spec=pltpu.PrefetchScalarGridSpec(
            num_scalar_prefetch=1,
            grid=(nt,),
            in_specs=[
                pl.BlockSpec((TMC, D), lambda i, d: (i, 0)),
                pl.BlockSpec((None, 1, D), lambda i, d: (_seg_of(i, TMC) * N_MOD + 5, 0, 0)),
                pl.BlockSpec((TMC, 128), lambda i, d: (i, 0)),
                pl.BlockSpec(memory_space=pl.ANY),
            ],
            out_specs=pl.BlockSpec((TMC, D), lambda i, d: (i, 0)),
            scratch_shapes=[
                pltpu.VMEM((2, 2, TMC, HALF), jnp.uint32),
                pltpu.SemaphoreType.DMA((2,)),
            ],
        ),
        compiler_params=_cp("arbitrary"),
        name="moe_combine",
    )(dest, xs, mod3, wt, yb)


def _route_plan(eid, rows):
    a = rows * 2
    nb = -(-a // TME) + ME
    e_flat = eid.reshape(-1)
    onehot = (e_flat[:, None] == jnp.arange(ME, dtype=jnp.int32)[None, :]).astype(jnp.int32)
    csum = jnp.cumsum(onehot, axis=0)
    counts = csum[-1]
    rank = jnp.take_along_axis(csum, e_flat[:, None], axis=1)[:, 0] - 1
    padded = ((counts + TME - 1) // TME) * TME
    pad_end = jnp.cumsum(padded)
    pad_start = pad_end - padded
    dest = pad_start[e_flat] + rank
    tok = jnp.arange(a, dtype=jnp.int32) // 2
    buf_tok = jnp.zeros((nb * TME,), jnp.int32).at[dest].set(tok)
    blk_start = jnp.concatenate([jnp.zeros((1,), jnp.int32), (pad_end // TME).astype(jnp.int32)])
    return dest, buf_tok, blk_start


def _final_body(x_ref, g_ref, o_ref):
    x = x_ref[...]
    o_ref[...] = x * lax.rsqrt(jnp.mean(x * x, axis=-1, keepdims=True) + EPS) * g_ref[...]


def _final_norm(xs, g):
    tm = 256
    return pl.pallas_call(
        _final_body,
        out_shape=jax.ShapeDtypeStruct((LAT, D), F32),
        grid=(LAT // tm,),
        in_specs=[pl.BlockSpec((tm, D), lambda i: (i, 0)), pl.BlockSpec((1, D), lambda i: (0, 0))],
        out_specs=pl.BlockSpec((tm, D), lambda i: (i, 0)),
        compiler_params=_cp("arbitrary"),
        name="final_norm",
    )(xs, g.reshape(1, D))


def _sincos_2d(rows, cols, d):
    quarter = d // 4
    omega = 1.0 / (POS_BASE ** (jnp.arange(quarter, dtype=F32) / quarter))

    def axis_emb(n):
        p = jnp.arange(n, dtype=F32)[:, None] * omega[None, :]
        return jnp.concatenate([jnp.sin(p), jnp.cos(p)], axis=-1)

    er, ec = axis_emb(rows), axis_emb(cols)
    half = 2 * quarter
    pos = jnp.concatenate([jnp.broadcast_to(er[:, None, :], (rows, cols, half)),
                           jnp.broadcast_to(ec[None, :, :], (rows, cols, half))], axis=-1)
    return pos.reshape(rows * cols, 2 * half)


def _gla_weights(w_in, wg_f, wg_b):
    z = jnp.zeros((D, ZPAD - 2 * RANK), F32)
    w_cat = jnp.concatenate([
        w_in[:, :QK], w_in[:, QK:QK + VV],
        w_in[:, STATE_COLS:STATE_COLS + QK], w_in[:, STATE_COLS + QK:],
        w_in[:, QK + VV:STATE_COLS], z], axis=1).astype(BF16)
    wgf = jnp.zeros((128, QK), F32).at[:RANK].set(wg_f)
    wgb = jnp.zeros((128, QK), F32).at[RANK:2 * RANK].set(wg_b)
    return w_cat, wgf, wgb


def kernel(x, c, ctx, c_ctx, ada_w, ada_b, norm1_g, norm2_g, gla_w_in, gla_wg_f, gla_bg_f, gla_wg_b, gla_bg_b,
           gla_onorm_g, gla_w_out, fnet_w_in, fnet_w_out, moe_rw_group, moe_rb_group, moe_rw_expert,
           moe_rb_expert, moe_w_gate, moe_w_up, moe_w_down, final_g):
    cvec = jnp.zeros((SEGS, D), F32).at[:B].set(c).at[CTX_SEG].set(c_ctx)
    mods = _mod_all(cvec, ada_w, ada_b)
    pos = _sincos_2d(SEQ // GRID_W, GRID_W, D)
    xs = _assemble(x.reshape(LAT, D), pos, ctx.reshape(NCTX, D))

    cc, sc = _dft_mats(FD, 1.0)
    cs = jnp.asarray(np.concatenate([cc, sc], axis=1), BF16)
    ct_l, st_l = _dft_mats(SEQ, (SEQ * FD) ** -0.5)
    ct_c, st_c = _dft_mats(CTX, (CTX * FD) ** -0.5)
    ct_l, st_l = jnp.asarray(ct_l, BF16), jnp.asarray(-st_l, BF16)
    ct_c, st_c = jnp.asarray(ct_c, BF16), jnp.asarray(-st_c, BF16)
    s_zero = jnp.zeros((B, H, DV, DK), F32)
    last_reader = ((DEPTH - 1) // 2) * 2

    for i in range(DEPTH):
        kind, j = i % 2, i // 2
        ctx_live = i < last_reader
        ctx_needed = i <= last_reader
        mod3 = mods[i].reshape(SEGS * N_MOD, 1, D)
        rows_in = ROWS if ctx_needed else LAT
        rows_out = ROWS if ctx_live else LAT
        h = _norm_mod(xs, norm1_g[i], mod3, 0, rows_in)

        if kind == 0:
            w_cat, wgf, wgb = _gla_weights(gla_w_in[j], gla_wg_f[j], gla_wg_b[j])
            bgf, bgb = gla_bg_f[j].reshape(1, QK), gla_bg_b[j].reshape(1, QK)
            onorm = gla_onorm_g[j].reshape(1, DV)
            proj = _mm(h, w_cat, rows_in, 512, 1280, F32, "gla_proj")
            o_c, sf, sb = _gla(proj, wgf, bgf, wgb, bgb, onorm, s_zero, s_zero, CTX, LAT, ctx_live)
            o_l, _, _ = _gla(proj, wgf, bgf, wgb, bgb, onorm, sf, sb, SEQ, 0, True)
            xs = _mm_res(o_l, o_c, gla_w_out[j].astype(BF16), xs, mod3, 2, rows_out)
        else:
            a, b = _fnet_in(h, fnet_w_in[j].astype(BF16), cs, rows_in)
            y = _fnet_time(ct_l, st_l, a, b, SEQ, 0)
            y_c = _fnet_time(ct_c, st_c, a, b, CTX, LAT) if ctx_live else y
            xs = _mm_res(y, y_c, fnet_w_out[j].astype(BF16), xs, mod3, 2, rows_out)

        rw = jnp.zeros((D, 128), F32).at[:, :MG].set(moe_rw_group[i])
        rw = rw.at[:, MG:MG + ME].set(jnp.transpose(moe_rw_expert[i], (1, 0, 2)).reshape(D, ME))
        rb = jnp.zeros((1, 128), F32).at[0, :MG].set(moe_rb_group[i]).at[0, MG:MG + ME].set(
            moe_rb_expert[i].reshape(ME))
        rw_hi = rw.astype(BF16)
        rw_lo = (rw - rw_hi.astype(F32)).astype(BF16)
        h2, eid, wt = _norm_route(xs, norm2_g[i], mod3, rw_hi, rw_lo, rb, rows_out)
        dest, buf_tok, blk_start = _route_plan(eid[:, :2], rows_out)
        yb = _moe_experts(i, blk_start, buf_tok, h2, moe_w_gate, moe_w_up, moe_w_down)
        xs = _combine(dest, xs, mod3, wt, yb, rows_out)

    return _final_norm(xs, final_g).reshape(B, SEQ, D)
```

```python
import functools

import jax
import jax.numpy as jnp
import numpy as np
from jax import lax
from jax.experimental import pallas as pl
from jax.experimental.pallas import tpu as pltpu

F32 = jnp.float32
BF16 = jnp.bfloat16

D = 2048
B = 4
SEQ = 2048
CTX = 256
DEPTH = 4
GRID_W = 64
EPS = 1e-6
POS_BASE = 10000.0
N_MOD = 6

H = 4
DK = 256
DV = 512
QK = H * DK
VV = H * DV
RANK = 16
GATE_NORM = 16.0
CHUNK = 64
STATE_COLS = QK + VV + 2 * RANK
ZPAD = 256
PROJ_N = QK + VV + QK + VV + ZPAD
COL_K, COL_V, COL_Q, COL_R, COL_Z = 0, QK, QK + VV, 2 * QK + VV, 2 * QK + 2 * VV

FG = 4
FD = D // FG

MG = 4
MPG = 8
ME = MG * MPG
MDE = D // 4
TME = 256

LAT = B * SEQ
NCTX = B * CTX
ROWS = LAT + NCTX
SEGS = 8
CTX_SEG = B

VMEM_LIMIT = 56 * 1024 * 1024


def _cp(*sem):
    return pltpu.CompilerParams(dimension_semantics=sem, vmem_limit_bytes=VMEM_LIMIT)


def _seg_of(i, tm):
    return jnp.where(i < LAT // tm, i // (SEQ // tm), CTX_SEG)


def _split2(a):
    hi = a.astype(BF16)
    lo = (a - hi.astype(F32)).astype(BF16)
    return hi, lo


def _dot(a, b):
    return jnp.dot(a, b, preferred_element_type=F32)


def _dot3(a, b_hi, b_lo):
    a_hi, a_lo = _split2(a)
    return _dot(a_hi, b_hi) + _dot(a_lo, b_hi) + _dot(a_hi, b_lo)


def _silu(a):
    return a * jax.nn.sigmoid(a)


HALF = D // 2


def _pack_rows(a):
    lo = lax.bitcast_convert_type(a[:, :HALF].astype(BF16).astype(F32), jnp.uint32)
    hi = lax.bitcast_convert_type(a[:, HALF:].astype(BF16).astype(F32), jnp.uint32)
    return (hi & jnp.uint32(0xFFFF0000)) | (lo >> 16)


def _unpack_rows(w):
    lo = lax.bitcast_convert_type(w << 16, F32)
    hi = lax.bitcast_convert_type(w & jnp.uint32(0xFFFF0000), F32)
    return lo, hi


def _mod_body(c_ref, w_ref, b_ref, o_ref):
    s = _silu(c_ref[...])
    w_hi, w_lo = _split2(w_ref[0])
    o_ref[0] = _dot3(s, w_hi, w_lo) + b_ref[0]


def _mod_all(cvec, ada_w, ada_b):
    tn = 512
    return pl.pallas_call(
        _mod_body,
        out_shape=jax.ShapeDtypeStruct((DEPTH, SEGS, N_MOD * D), F32),
        grid=(DEPTH, N_MOD * D // tn),
        in_specs=[
            pl.BlockSpec((SEGS, D), lambda l, j: (0, 0)),
            pl.BlockSpec((1, D, tn), lambda l, j: (l, 0, j)),
            pl.BlockSpec((1, 1, tn), lambda l, j: (l, 0, j)),
        ],
        out_specs=pl.BlockSpec((1, SEGS, tn), lambda l, j: (l, 0, j)),
        compiler_params=_cp("arbitrary", "arbitrary"),
        name="adaln_mod",
    )(cvec, ada_w, ada_b.reshape(DEPTH, 1, N_MOD * D))


def _assemble_body(x_ref, pos_ref, ctx_ref, o_ref, *, nlat):
    i = pl.program_id(0)

    @pl.when(i < nlat)
    def _():
        o_ref[...] = x_ref[...] + pos_ref[...]

    @pl.when(i >= nlat)
    def _():
        o_ref[...] = ctx_ref[...]


def _assemble(x2, pos, ctx2):
    tm = 256
    nlat = LAT // tm
    return pl.pallas_call(
        functools.partial(_assemble_body, nlat=nlat),
        out_shape=jax.ShapeDtypeStruct((ROWS, D), F32),
        grid=(ROWS // tm,),
        in_specs=[
            pl.BlockSpec((tm, D), lambda i: (jnp.minimum(i, nlat - 1), 0)),
            pl.BlockSpec((tm, D), lambda i: (i % (SEQ // tm), 0)),
            pl.BlockSpec((tm, D), lambda i: (jnp.maximum(i - nlat, 0), 0)),
        ],
        out_specs=pl.BlockSpec((tm, D), lambda i: (i, 0)),
        compiler_params=_cp("arbitrary"),
        name="assemble_stream",
    )(x2, pos, ctx2)


def _norm_mod_val(x, g, sh, sc):
    y = x * lax.rsqrt(jnp.mean(x * x, axis=-1, keepdims=True) + EPS) * g
    return y * (1.0 + sc) + sh


def _norm_mod_body(x_ref, g_ref, sh_ref, sc_ref, o_ref):
    o_ref[...] = _norm_mod_val(x_ref[...], g_ref[...], sh_ref[...], sc_ref[...]).astype(o_ref.dtype)


def _mod_spec(which, tm):
    return pl.BlockSpec((None, 1, D), lambda i: (_seg_of(i, tm) * N_MOD + which, 0, 0))


def _norm_mod(xs, g, mod3, which_shift, rows):
    tm = 256
    return pl.pallas_call(
        _norm_mod_body,
        out_shape=jax.ShapeDtypeStruct((rows, D), BF16),
        grid=(rows // tm,),
        in_specs=[
            pl.BlockSpec((tm, D), lambda i: (i, 0)),
            pl.BlockSpec((1, D), lambda i: (0, 0)),
            _mod_spec(which_shift, tm),
            _mod_spec(which_shift + 1, tm),
        ],
        out_specs=pl.BlockSpec((tm, D), lambda i: (i, 0)),
        compiler_params=_cp("arbitrary"),
        name="norm_mod",
    )(xs, g.reshape(1, D), mod3, mod3)


def _norm_route_body(x_ref, g_ref, sh_ref, sc_ref, rwh_ref, rwl_ref, rb_ref, h_ref, eid_ref, wt_ref, cnt_ref, run_s):
    i = pl.program_id(0)

    @pl.when(i == 0)
    def _():
        run_s[...] = jnp.zeros_like(run_s)

    h2 = _norm_mod_val(x_ref[...], g_ref[...], sh_ref[...], sc_ref[...])
    h_ref[...] = _pack_rows(h2)
    lg = _dot3(h2, rwh_ref[...], rwl_ref[...]) + rb_ref[...]
    lane_i = lax.broadcasted_iota(jnp.int32, lg.shape, 1)
    lane = lane_i.astype(F32)
    neg = jnp.float32(-jnp.inf)
    big = jnp.float32(1024.0)

    glog = jnp.where(lane < MG, lg, neg)
    gmax = jnp.max(glog, axis=-1, keepdims=True)
    gidx = jnp.min(jnp.where(glog == gmax, lane, big), axis=-1, keepdims=True)
    g_w = 1.0 / jnp.sum(jnp.exp(glog - gmax), axis=-1, keepdims=True)

    lo = MG + MPG * gidx
    el = jnp.where((lane >= lo) & (lane < lo + MPG), lg, neg)
    m1 = jnp.max(el, axis=-1, keepdims=True)
    i1 = jnp.min(jnp.where(el == m1, lane, big), axis=-1, keepdims=True)
    el2 = jnp.where(lane == i1, neg, el)
    m2 = jnp.max(el2, axis=-1, keepdims=True)
    i2 = jnp.min(jnp.where(el2 == m2, lane, big), axis=-1, keepdims=True)
    e2 = jnp.exp(m2 - m1)
    den = 1.0 / (1.0 + e2)
    w1 = g_w * den
    w2 = g_w * (e2 * den)
    e_a = i1 - MG
    e_b = i2 - MG

    tm = lg.shape[0]
    r_i = lax.broadcasted_iota(jnp.int32, (tm, tm), 0)
    c_i = lax.broadcasted_iota(jnp.int32, (tm, tm), 1)
    earlier = jnp.where(r_i > c_i, 1.0, 0.0).astype(BF16)
    oh_a = jnp.where(lane == e_a, 1.0, 0.0)
    oh_b = jnp.where(lane == e_b, 1.0, 0.0)
    run = run_s[...]
    rank_a = jnp.sum(oh_a * (_dot(earlier, oh_a.astype(BF16)) + run[0:1, :]), axis=-1, keepdims=True)
    rank_b = jnp.sum(oh_b * (_dot(earlier, oh_b.astype(BF16)) + run[1:2, :]), axis=-1, keepdims=True)
    run_s[0:1, :] = run[0:1, :] + jnp.sum(oh_a, axis=0, keepdims=True)
    run_s[1:2, :] = run[1:2, :] + jnp.sum(oh_b, axis=0, keepdims=True)
    cnt_ref[...] = run_s[...]

    ids = jnp.where(lane_i == 0, e_a, jnp.where(lane_i == 1, e_b, jnp.where(lane_i == 2, rank_a,
                                                                           jnp.where(lane_i == 3, rank_b, 0.0))))
    eid_ref[...] = ids.astype(jnp.int32)
    wt_ref[...] = jnp.where(lane_i == 0, w1, jnp.where(lane_i == 1, w2, 0.0))


def _norm_route(xs, g, mod3, rw_hi, rw_lo, rb, rows):
    tm = 256
    return pl.pallas_call(
        _norm_route_body,
        out_shape=(
            jax.ShapeDtypeStruct((rows, HALF), jnp.uint32),
            jax.ShapeDtypeStruct((rows, 128), jnp.int32),
            jax.ShapeDtypeStruct((rows, 128), F32),
            jax.ShapeDtypeStruct((8, 128), F32),
        ),
        grid=(rows // tm,),
        in_specs=[
            pl.BlockSpec((tm, D), lambda i: (i, 0)),
            pl.BlockSpec((1, D), lambda i: (0, 0)),
            _mod_spec(3, tm),
            _mod_spec(4, tm),
            pl.BlockSpec((D, 128), lambda i: (0, 0)),
            pl.BlockSpec((D, 128), lambda i: (0, 0)),
            pl.BlockSpec((1, 128), lambda i: (0, 0)),
        ],
        out_specs=(
            pl.BlockSpec((tm, HALF), lambda i: (i, 0)),
            pl.BlockSpec((tm, 128), lambda i: (i, 0)),
            pl.BlockSpec((tm, 128), lambda i: (i, 0)),
            pl.BlockSpec((8, 128), lambda i: (0, 0)),
        ),
        scratch_shapes=[pltpu.VMEM((8, 128), F32)],
        compiler_params=_cp("arbitrary"),
        name="norm_route",
    )(xs, g.reshape(1, D), mod3, mod3, rw_hi, rw_lo, rb)


def _mm_body(x_ref, w_ref, o_ref):
    o_ref[...] = _dot(x_ref[...], w_ref[...]).astype(o_ref.dtype)


def _mm(x, w, rows, tm, tn, out_dtype, name):
    k = x.shape[1]
    n = w.shape[1]
    return pl.pallas_call(
        _mm_body,
        out_shape=jax.ShapeDtypeStruct((rows, n), out_dtype),
        grid=(n // tn, rows // tm),
        in_specs=[
            pl.BlockSpec((tm, k), lambda j, i: (i, 0)),
            pl.BlockSpec((k, tn), lambda j, i: (0, j)),
        ],
        out_specs=pl.BlockSpec((tm, tn), lambda j, i: (i, j)),
        compiler_params=_cp("arbitrary", "arbitrary"),
        name=name,
    )(x, w)


def _mm_res_body(xl_ref, xc_ref, w_ref, res_ref, gate_ref, o_ref, *, nlat):
    i = pl.program_id(1)

    @pl.when(i < nlat)
    def _():
        o_ref[...] = res_ref[...] + gate_ref[...] * _dot(xl_ref[...], w_ref[...])

    @pl.when(i >= nlat)
    def _():
        o_ref[...] = res_ref[...] + gate_ref[...] * _dot(xc_ref[...], w_ref[...])


def _mm_res(a_lat, a_ctx, w, xs, mod3, which_gate, rows):
    tm, tn = 512, 1024
    k = a_lat.shape[1]
    nlat = LAT // tm
    return pl.pallas_call(
        functools.partial(_mm_res_body, nlat=nlat),
        out_shape=jax.ShapeDtypeStruct((rows, D), F32),
        grid=(D // tn, rows // tm),
        in_specs=[
            pl.BlockSpec((tm, k), lambda j, i: (jnp.minimum(i, nlat - 1), 0)),
            pl.BlockSpec((tm, k), lambda j, i: (jnp.maximum(i - nlat, 0), 0)),
            pl.BlockSpec((k, tn), lambda j, i: (0, j)),
            pl.BlockSpec((tm, tn), lambda j, i: (i, j)),
            pl.BlockSpec((None, 1, tn), lambda j, i: (_seg_of(i, tm) * N_MOD + which_gate, 0, j)),
        ],
        out_specs=pl.BlockSpec((tm, tn), lambda j, i: (i, j)),
        compiler_params=_cp("arbitrary", "arbitrary"),
        name="mm_residual",
    )(a_lat, a_ctx, w, xs, mod3)


def _log_sigmoid(a):
    return jnp.minimum(a, 0.0) - jnp.log1p(jnp.exp(-jnp.abs(a)))


PREP = 256


def _gla_body(k_ref, v_ref, q_ref, r_ref, z_ref, wgf_ref, bgf_ref, wgb_ref, bgb_ref, on_ref, s0f_ref, s0b_ref,
              o_ref, sf_ref, sb_ref,
              qf_s, kf_s, df_s, qb_s, kb_s, db_s, vb_s, dec_s, o_s, stf_s, stb_s, *, T, emit_o):
    nc = T // CHUNK
    nblk = T // PREP
    cpb = PREP // CHUNK
    scale = DK ** -0.5

    row = lax.broadcasted_iota(jnp.int32, (PREP, PREP), 0)
    col = lax.broadcasted_iota(jnp.int32, (PREP, PREP), 1)
    shift = CHUNK.bit_length() - 1
    same = lax.shift_right_logical(row, shift) == lax.shift_right_logical(col, shift)
    tri_f = jnp.where(same & (row >= col), 1.0, 0.0).astype(BF16)
    tri_b = jnp.where(same & (row <= col), 1.0, 0.0).astype(BF16)
    wf_hi, wf_lo = _split2(wgf_ref[...])
    wb_hi, wb_lo = _split2(wgb_ref[...])

    def chunk_sums(tri, g):
        g1 = g.astype(BF16)
        rem = g - g1.astype(F32)
        g2 = rem.astype(BF16)
        g3 = (rem - g2.astype(F32)).astype(BF16)
        return _dot(tri, g1) + _dot(tri, g2) + _dot(tri, g3)

    def edge_rows(G, e):
        return jnp.concatenate(
            [jnp.broadcast_to(G[c * CHUNK + e:c * CHUNK + e + 1, :], (CHUNK, DK)) for c in range(cpb)], axis=0)

    def prep(blk, carry):
        rows = pl.ds(pl.multiple_of(blk * PREP, PREP), PREP)
        z = z_ref[rows, :128]
        gf = _log_sigmoid(_dot3(z, wf_hi, wf_lo) + bgf_ref[...]) / GATE_NORM
        gb = _log_sigmoid(_dot3(z, wb_hi, wb_lo) + bgb_ref[...]) / GATE_NORM
        Gf = chunk_sums(tri_f, gf)
        Gb = chunk_sums(tri_b, gb)
        k = k_ref[rows, :]
        q = q_ref[rows, :] * scale
        qf_s[rows, :] = (q * jnp.exp(Gf)).astype(BF16)
        kf_s[rows, :] = (k * jnp.exp(-Gf)).astype(BF16)
        df_s[rows, :] = (k * jnp.exp(edge_rows(Gf, CHUNK - 1) - Gf)).astype(BF16)
        qb_s[rows, :] = (q * jnp.exp(Gb)).astype(BF16)
        kb_s[rows, :] = (k * jnp.exp(-Gb)).astype(BF16)
        db_s[rows, :] = (k * jnp.exp(edge_rows(Gb, 0) - Gb)).astype(BF16)
        vb_s[rows, :] = v_ref[rows, :].astype(BF16)
        for c in range(cpb):
            ef = Gf[c * CHUNK + CHUNK - 1:c * CHUNK + CHUNK, :]
            eb = Gb[c * CHUNK:c * CHUNK + 1, :]
            dec_s[0, blk * cpb + c] = jnp.broadcast_to(jnp.exp(ef), (8, DK))
            dec_s[1, blk * cpb + c] = jnp.broadcast_to(jnp.exp(eb), (8, DK))
        return carry

    lax.fori_loop(0, nblk, prep, 0)

    stf_s[...] = s0f_ref[0, 0]
    stb_s[...] = s0b_ref[0, 0]
    r64 = lax.broadcasted_iota(jnp.int32, (CHUNK, CHUNK), 0)
    c64 = lax.broadcasted_iota(jnp.int32, (CHUNK, CHUNK), 1)
    nt_dims = (((1,), (1,)), ((), ()))
    tn_dims = (((0,), (0,)), ((), ()))

    def chunk(c, q_s, k_s, d_s, st_s, di, keep):
        rows = pl.ds(pl.multiple_of(c * CHUNK, CHUNK), CHUNK)
        qe = q_s[rows, :]
        vb = vb_s[rows, :]
        st = st_s[...]
        o = None
        if emit_o:
            a = lax.dot_general(qe, k_s[rows, :], nt_dims, preferred_element_type=F32)
            a = jnp.where(keep, a, 0.0).astype(BF16)
            o = _dot(a, vb) + lax.dot_general(qe, st.astype(BF16), nt_dims, preferred_element_type=F32)
        upd = lax.dot_general(vb, d_s[rows, :], tn_dims, preferred_element_type=F32)
        st_s[...] = st * dec_s[di, c][0:1, :] + upd
        return o, rows

    def step(ci, carry, accumulate):
        of, rows_f = chunk(ci, qf_s, kf_s, df_s, stf_s, 0, r64 >= c64)
        ob, rows_b = chunk(nc - 1 - ci, qb_s, kb_s, db_s, stb_s, 1, r64 <= c64)
        if emit_o:
            if accumulate:
                o_s[rows_f, :] = o_s[rows_f, :] + of
                o_s[rows_b, :] = o_s[rows_b, :] + ob
            else:
                o_s[rows_f, :] = of
                o_s[rows_b, :] = ob
        return carry

    lax.fori_loop(0, nc // 2, functools.partial(step, accumulate=False), 0, unroll=2)
    lax.fori_loop(nc // 2, nc, functools.partial(step, accumulate=True), 0, unroll=2)
    sf_ref[0, 0] = stf_s[...]
    sb_ref[0, 0] = stb_s[...]

    if emit_o:
        def fin(blk, carry):
            rows = pl.ds(pl.multiple_of(blk * PREP, PREP), PREP)
            tot = o_s[rows, :]
            y = tot * lax.rsqrt(jnp.mean(tot * tot, axis=-1, keepdims=True) + EPS) * on_ref[...]
            o_ref[rows, :] = (y * _silu(r_ref[rows, :])).astype(o_ref.dtype)
            return carry

        lax.fori_loop(0, nblk, fin, 0)
    else:
        o_ref[...] = jnp.zeros_like(o_ref)


def _gla(proj, wgf, bgf, wgb, bgb, onorm, s0f, s0b, T, row_off, emit_o):
    rb = row_off // T
    st_spec = pl.BlockSpec((1, 1, DV, DK), lambda b, h: (b, h, 0, 0))
    return pl.pallas_call(
        functools.partial(_gla_body, T=T, emit_o=emit_o),
        out_shape=(
            jax.ShapeDtypeStruct((B * T, VV), BF16),
            jax.ShapeDtypeStruct((B, H, DV, DK), F32),
            jax.ShapeDtypeStruct((B, H, DV, DK), F32),
        ),
        grid=(B, H),
        in_specs=[
            pl.BlockSpec((T, DK), lambda b, h: (rb + b, COL_K // DK + h)),
            pl.BlockSpec((T, DV), lambda b, h: (rb + b, COL_V // DV + h)),
            pl.BlockSpec((T, DK), lambda b, h: (rb + b, COL_Q // DK + h)),
            pl.BlockSpec((T, DV), lambda b, h: (rb + b, COL_R // DV + h)),
            pl.BlockSpec((T, ZPAD), lambda b, h: (rb + b, COL_Z // ZPAD)),
            pl.BlockSpec((128, DK), lambda b, h: (0, h)),
            pl.BlockSpec((1, DK), lambda b, h: (0, h)),
            pl.BlockSpec((128, DK), lambda b, h: (0, h)),
            pl.BlockSpec((1, DK), lambda b, h: (0, h)),
            pl.BlockSpec((1, DV), lambda b, h: (0, 0)),
            st_spec,
            st_spec,
        ],
        out_specs=(
            pl.BlockSpec((T, DV), lambda b, h: (b, h)),
            st_spec,
            st_spec,
        ),
        scratch_shapes=[pltpu.VMEM((T, DK), BF16)] * 6 + [
            pltpu.VMEM((T, DV), BF16),
            pltpu.VMEM((2, T // CHUNK, 8, DK), F32),
            pltpu.VMEM((T, DV), F32),
            pltpu.VMEM((DV, DK), F32),
            pltpu.VMEM((DV, DK), F32),
        ],
        compiler_params=_cp("arbitrary", "arbitrary"),
        name="gla_scan",
    )(proj, proj, proj, proj, proj, wgf, bgf, wgb, bgb, onorm, s0f, s0b)


def _fnet_in_body(h_ref, w_ref, cs_ref, a_ref, b_ref):
    u = _dot(h_ref[...], w_ref[...]).astype(BF16)
    for g in range(FG):
        ab = _dot(u[:, g * FD:(g + 1) * FD], cs_ref[...])
        a_ref[:, g * FD:(g + 1) * FD] = ab[:, :FD].astype(BF16)
        b_ref[:, g * FD:(g + 1) * FD] = ab[:, FD:].astype(BF16)


def _fnet_in(h, w, cs, rows):
    tm = 512
    return pl.pallas_call(
        _fnet_in_body,
        out_shape=(jax.ShapeDtypeStruct((rows, D), BF16), jax.ShapeDtypeStruct((rows, D), BF16)),
        grid=(rows // tm,),
        in_specs=[
            pl.BlockSpec((tm, D), lambda i: (i, 0)),
            pl.BlockSpec((D, D), lambda i: (0, 0)),
            pl.BlockSpec((FD, 2 * FD), lambda i: (0, 0)),
        ],
        out_specs=(pl.BlockSpec((tm, D), lambda i: (i, 0)), pl.BlockSpec((tm, D), lambda i: (i, 0))),
        compiler_params=_cp("arbitrary"),
        name="fnet_in",
    )(h, w, cs)


def _fnet_time_body(ct_ref, st_ref, a_ref, b_ref, o_ref):
    o_ref[...] = (_dot(ct_ref[...], a_ref[...]) + _dot(st_ref[...], b_ref[...])).astype(o_ref.dtype)


def _fnet_time(ct, st, a, b, T, row_off):
    tm = min(T, 512)
    tn = 1024
    rb = row_off // T
    return pl.pallas_call(
        _fnet_time_body,
        out_shape=jax.ShapeDtypeStruct((B * T, D), BF16),
        grid=(B, D // tn, T // tm),
        in_specs=[
            pl.BlockSpec((tm, T), lambda s, j, i: (i, 0)),
            pl.BlockSpec((tm, T), lambda s, j, i: (i, 0)),
            pl.BlockSpec((T, tn), lambda s, j, i: (rb + s, j)),
            pl.BlockSpec((T, tn), lambda s, j, i: (rb + s, j)),
        ],
        out_specs=pl.BlockSpec((tm, tn), lambda s, j, i: (s * (T // tm) + i, j)),
        compiler_params=_cp("arbitrary", "arbitrary", "arbitrary"),
        name="fnet_time",
    )(ct, st, a, b)


def _dft_mats(n, scale):
    idx = np.arange(n, dtype=np.int64)
    ang = 2.0 * np.pi * ((idx[:, None] * idx[None, :]) % n).astype(np.float64) / n
    return np.cos(ang) * scale, np.sin(ang) * scale


def _row_copy(src_hbm, row, dst_vmem, r, sem):
    return pltpu.make_async_copy(src_hbm.at[pl.ds(row, 1)], dst_vmem.at[pl.ds(r, 1)], sem)


def _moe_body(bs_ref, tok_ref, h_hbm, wg_ref, wu_ref, wd_ref, y_hbm, xbuf, obuf, gsem, osem, wgb, wub, wdb, *, nb):
    e = pl.program_id(0)
    b0 = bs_ref[e]
    b1 = bs_ref[e + 1]
    total = bs_ref[ME]

    def gather(g):
        slot = g % 2
        base = g * TME
        for r in range(TME):
            _row_copy(h_hbm, tok_ref[base + r], xbuf.at[slot], r, gsem.at[slot]).start(priority=r % 2)

    def out_copy(g):
        slot = g % 2
        dst = y_hbm.at[pl.ds(pl.multiple_of(g * TME, TME), TME)]
        return pltpu.make_async_copy(obuf.at[slot], dst, osem.at[slot])

    @pl.when((e == 0) & (total > 0))
    def _():
        gather(0)

    @pl.when(b1 > b0)
    def _():
        wgb[...] = wg_ref[...].astype(BF16)
        wub[...] = wu_ref[...].astype(BF16)
        wdb[...] = wd_ref[...].astype(BF16)

    def block(g, carry):
        slot = g % 2

        @pl.when(g + 1 < total)
        def _():
            gather(g + 1)

        pltpu.make_async_copy(h_hbm.at[pl.ds(0, TME)], xbuf.at[slot], gsem.at[slot]).wait()
        x_lo, x_hi = _unpack_rows(xbuf[slot])
        x = jnp.concatenate([x_lo.astype(BF16), x_hi.astype(BF16)], axis=1)
        hmid = _silu(_dot(x, wgb[...])) * _dot(x, wub[...])
        y = _pack_rows(_dot(hmid.astype(BF16), wdb[...]))

        @pl.when(g >= 2)
        def _():
            out_copy(g - 2).wait()

        obuf[slot] = y
        out_copy(g).start()
        return carry

    lax.fori_loop(b0, b1, block, 0)

    @pl.when(e == ME - 1)
    def _():
        @pl.when(total >= 2)
        def _():
            out_copy(total - 2).wait()

        @pl.when(total >= 1)
        def _():
            out_copy(total - 1).wait()

        def zero_copy(g):
            dst = y_hbm.at[pl.ds(pl.multiple_of(g * TME, TME), TME)]
            return pltpu.make_async_copy(obuf.at[0], dst, osem.at[0])

        obuf[0] = jnp.zeros((TME, HALF), jnp.uint32)
        lax.fori_loop(total, nb, lambda g, c: (zero_copy(g).start(), c)[1], 0)
        lax.fori_loop(total, nb, lambda g, c: (zero_copy(g).wait(), c)[1], 0)


def _moe_experts(layer, blk_start, buf_tok, h2, w_gate, w_up, w_down):
    nb = buf_tok.shape[0] // TME
    w_in_spec = pl.BlockSpec((None, None, D, MDE), lambda e, bs, tk: (layer, e, 0, 0))
    return pl.pallas_call(
        functools.partial(_moe_body, nb=nb),
        out_shape=jax.ShapeDtypeStruct((nb * TME, HALF), jnp.uint32),
        grid_spec=pltpu.PrefetchScalarGridSpec(
            num_scalar_prefetch=2,
            grid=(ME,),
            in_specs=[
                pl.BlockSpec(memory_space=pl.ANY),
                w_in_spec,
                w_in_spec,
                pl.BlockSpec((None, None, MDE, D), lambda e, bs, tk: (layer, e, 0, 0)),
            ],
            out_specs=pl.BlockSpec(memory_space=pl.ANY),
            scratch_shapes=[
                pltpu.VMEM((2, TME, HALF), jnp.uint32),
                pltpu.VMEM((2, TME, HALF), jnp.uint32),
                pltpu.SemaphoreType.DMA((2,)),
                pltpu.SemaphoreType.DMA((2,)),
                pltpu.VMEM((D, MDE), BF16),
                pltpu.VMEM((D, MDE), BF16),
                pltpu.VMEM((MDE, D), BF16),
            ],
        ),
        compiler_params=_cp("arbitrary"),
        name="moe_experts",
    )(blk_start, buf_tok, h2, w_gate, w_up, w_down)


TMC = 128


def _combine_body(dest_ref, x_ref, gate_ref, wt_ref, y_hbm, o_ref, ybuf, sem, *, nt):
    i = pl.program_id(0)

    def gather(tile, slot):
        base = tile * (2 * TMC)

        for r in range(TMC):
            _row_copy(y_hbm, dest_ref[base + 2 * r], ybuf.at[slot, 0], r, sem.at[slot]).start(priority=0)
            _row_copy(y_hbm, dest_ref[base + 2 * r + 1], ybuf.at[slot, 1], r, sem.at[slot]).start(priority=1)

    @pl.when(i == 0)
    def _():
        gather(0, 0)

    @pl.when(i + 1 < nt)
    def _():
        gather(i + 1, (i + 1) % 2)

    slot = i % 2
    for k in range(2):
        pltpu.make_async_copy(y_hbm.at[pl.ds(0, TMC)], ybuf.at[slot, k], sem.at[slot]).wait()
    wt = wt_ref[...]
    w0, w1 = wt[:, 0:1], wt[:, 1:2]
    lo0, hi0 = _unpack_rows(ybuf[slot, 0])
    lo1, hi1 = _unpack_rows(ybuf[slot, 1])
    o_ref[:, :HALF] = x_ref[:, :HALF] + gate_ref[:, :HALF] * (w0 * lo0 + w1 * lo1)
    o_ref[:, HALF:] = x_ref[:, HALF:] + gate_ref[:, HALF:] * (w0 * hi0 + w1 * hi1)


def _combine(dest, xs, mod3, wt, yb, rows):
    nt = rows // TMC
    return pl.pallas_call(
        functools.partial(_combine_body, nt=nt),
        out_shape=jax.ShapeDtypeStruct((rows, D), F32),
        grid_spec=pltpu.PrefetchScalarGridSpec(
            num_scalar_prefetch=1,
            grid=(nt,),
            in_specs=[
                pl.BlockSpec((TMC, D), lambda i, d: (i, 0)),
                pl.BlockSpec((None, 1, D), lambda i, d: (_seg_of(i, TMC) * N_MOD + 5, 0, 0)),
                pl.BlockSpec((TMC, 128), lambda i, d: (i, 0)),
                pl.BlockSpec(memory_space=pl.ANY),
            ],
            out_specs=pl.BlockSpec((TMC, D), lambda i, d: (i, 0)),
            scratch_shapes=[
                pltpu.VMEM((2, 2, TMC, HALF), jnp.uint32),
                pltpu.SemaphoreType.DMA((2,)),
            ],
        ),
        compiler_params=_cp("arbitrary"),
        name="moe_combine",
    )(dest, xs, mod3, wt, yb)


PLAN_CHUNK = 1024


def _plan_body(base_ref, er_ref, dest_ref, tok_ref, *, nslots):
    i = pl.program_id(0)

    @pl.when(i == 0)
    def _():
        def clear(s, carry):
            tok_ref[s] = 0
            return carry

        lax.fori_loop(0, nslots, clear, 0, unroll=8)

    def place(t, carry):
        d0 = base_ref[er_ref[0, 4 * t]] + er_ref[0, 4 * t + 2]
        d1 = base_ref[ME + er_ref[0, 4 * t + 1]] + er_ref[0, 4 * t + 3]
        dest_ref[0, 2 * t] = d0
        dest_ref[0, 2 * t + 1] = d1
        tok = i * PLAN_CHUNK + t
        tok_ref[d0] = tok
        tok_ref[d1] = tok
        return carry

    lax.fori_loop(0, PLAN_CHUNK, place, 0, unroll=8)


def _route_plan(eid, cnt, rows):
    a = rows * 2
    nb = -(-a // TME) + ME
    steps = rows // PLAN_CHUNK
    c0 = cnt[0, :ME].astype(jnp.int32)
    c1 = cnt[1, :ME].astype(jnp.int32)
    padded = ((c0 + c1 + TME - 1) // TME) * TME
    pad_end = jnp.cumsum(padded)
    pad_start = pad_end - padded
    base = jnp.concatenate([pad_start, pad_start + c0]).astype(jnp.int32)
    blk_start = jnp.concatenate([jnp.zeros((1,), jnp.int32), (pad_end // TME).astype(jnp.int32)])
    er = eid[:, :4].reshape(steps, 1, 4 * PLAN_CHUNK)
    dest, buf_tok = pl.pallas_call(
        functools.partial(_plan_body, nslots=nb * TME),
        out_shape=(
            jax.ShapeDtypeStruct((steps, 1, 2 * PLAN_CHUNK), jnp.int32),
            jax.ShapeDtypeStruct((nb * TME,), jnp.int32),
        ),
        grid_spec=pltpu.PrefetchScalarGridSpec(
            num_scalar_prefetch=1,
            grid=(steps,),
            in_specs=[pl.BlockSpec((None, 1, 4 * PLAN_CHUNK), lambda i, b: (i, 0, 0), memory_space=pltpu.SMEM)],
            out_specs=(
                pl.BlockSpec((None, 1, 2 * PLAN_CHUNK), lambda i, b: (i, 0, 0), memory_space=pltpu.SMEM),
                pl.BlockSpec(memory_space=pltpu.SMEM),
            ),
        ),
        compiler_params=_cp("arbitrary"),
        name="route_plan",
    )(base, er)
    return dest.reshape(a), buf_tok, blk_start


def _final_body(x_ref, g_ref, o_ref):
    x = x_ref[...]
    o_ref[...] = x * lax.rsqrt(jnp.mean(x * x, axis=-1, keepdims=True) + EPS) * g_ref[...]


def _final_norm(xs, g):
    tm = 256
    return pl.pallas_call(
        _final_body,
        out_shape=jax.ShapeDtypeStruct((LAT, D), F32),
        grid=(LAT // tm,),
        in_specs=[pl.BlockSpec((tm, D), lambda i: (i, 0)), pl.BlockSpec((1, D), lambda i: (0, 0))],
        out_specs=pl.BlockSpec((tm, D), lambda i: (i, 0)),
        compiler_params=_cp("arbitrary"),
        name="final_norm",
    )(xs, g.reshape(1, D))


def _sincos_2d(rows, cols, d):
    quarter = d // 4
    omega = 1.0 / (POS_BASE ** (jnp.arange(quarter, dtype=F32) / quarter))

    def axis_emb(n):
        p = jnp.arange(n, dtype=F32)[:, None] * omega[None, :]
        return jnp.concatenate([jnp.sin(p), jnp.cos(p)], axis=-1)

    er, ec = axis_emb(rows), axis_emb(cols)
    half = 2 * quarter
    pos = jnp.concatenate([jnp.broadcast_to(er[:, None, :], (rows, cols, half)),
                           jnp.broadcast_to(ec[None, :, :], (rows, cols, half))], axis=-1)
    return pos.reshape(rows * cols, 2 * half)


def _gla_weights(w_in, wg_f, wg_b):
    z = jnp.zeros((D, ZPAD - 2 * RANK), F32)
    w_cat = jnp.concatenate([
        w_in[:, :QK], w_in[:, QK:QK + VV],
        w_in[:, STATE_COLS:STATE_COLS + QK], w_in[:, STATE_COLS + QK:],
        w_in[:, QK + VV:STATE_COLS], z], axis=1).astype(BF16)
    wgf = jnp.zeros((128, QK), F32).at[:RANK].set(wg_f)
    wgb = jnp.zeros((128, QK), F32).at[RANK:2 * RANK].set(wg_b)
    return w_cat, wgf, wgb


def kernel(x, c, ctx, c_ctx, ada_w, ada_b, norm1_g, norm2_g, gla_w_in, gla_wg_f, gla_bg_f, gla_wg_b, gla_bg_b,
           gla_onorm_g, gla_w_out, fnet_w_in, fnet_w_out, moe_rw_group, moe_rb_group, moe_rw_expert,
           moe_rb_expert, moe_w_gate, moe_w_up, moe_w_down, final_g):
    cvec = jnp.zeros((SEGS, D), F32).at[:B].set(c).at[CTX_SEG].set(c_ctx)
    mods = _mod_all(cvec, ada_w, ada_b)
    pos = _sincos_2d(SEQ // GRID_W, GRID_W, D)
    xs = _assemble(x.reshape(LAT, D), pos, ctx.reshape(NCTX, D))

    cc, sc = _dft_mats(FD, 1.0)
    cs = jnp.asarray(np.concatenate([cc, sc], axis=1), BF16)
    ct_l, st_l = _dft_mats(SEQ, (SEQ * FD) ** -0.5)
    ct_c, st_c = _dft_mats(CTX, (CTX * FD) ** -0.5)
    ct_l, st_l = jnp.asarray(ct_l, BF16), jnp.asarray(-st_l, BF16)
    ct_c, st_c = jnp.asarray(ct_c, BF16), jnp.asarray(-st_c, BF16)
    s_zero = jnp.zeros((B, H, DV, DK), F32)
    last_reader = ((DEPTH - 1) // 2) * 2

    for i in range(DEPTH):
        kind, j = i % 2, i // 2
        ctx_live = i < last_reader
        ctx_needed = i <= last_reader
        mod3 = mods[i].reshape(SEGS * N_MOD, 1, D)
        rows_in = ROWS if ctx_needed else LAT
        rows_out = ROWS if ctx_live else LAT
        h = _norm_mod(xs, norm1_g[i], mod3, 0, rows_in)

        if kind == 0:
            w_cat, wgf, wgb = _gla_weights(gla_w_in[j], gla_wg_f[j], gla_wg_b[j])
            bgf, bgb = gla_bg_f[j].reshape(1, QK), gla_bg_b[j].reshape(1, QK)
            onorm = gla_onorm_g[j].reshape(1, DV)
            proj = _mm(h, w_cat, rows_in, 512, 1280, F32, "gla_proj")
            o_c, sf, sb = _gla(proj, wgf, bgf, wgb, bgb, onorm, s_zero, s_zero, CTX, LAT, ctx_live)
            o_l, _, _ = _gla(proj, wgf, bgf, wgb, bgb, onorm, sf, sb, SEQ, 0, True)
            xs = _mm_res(o_l, o_c, gla_w_out[j].astype(BF16), xs, mod3, 2, rows_out)
        else:
            a, b = _fnet_in(h, fnet_w_in[j].astype(BF16), cs, rows_in)
            y = _fnet_time(ct_l, st_l, a, b, SEQ, 0)
            y_c = _fnet_time(ct_c, st_c, a, b, CTX, LAT) if ctx_live else y
            xs = _mm_res(y, y_c, fnet_w_out[j].astype(BF16), xs, mod3, 2, rows_out)

        rw = jnp.zeros((D, 128), F32).at[:, :MG].set(moe_rw_group[i])
        rw = rw.at[:, MG:MG + ME].set(jnp.transpose(moe_rw_expert[i], (1, 0, 2)).reshape(D, ME))
        rb = jnp.zeros((1, 128), F32).at[0, :MG].set(moe_rb_group[i]).at[0, MG:MG + ME].set(
            moe_rb_expert[i].reshape(ME))
        rw_hi = rw.astype(BF16)
        rw_lo = (rw - rw_hi.astype(F32)).astype(BF16)
        h2, eid, wt, cnt = _norm_route(xs, norm2_g[i], mod3, rw_hi, rw_lo, rb, rows_out)
        dest, buf_tok, blk_start = _route_plan(eid, cnt, rows_out)
        yb = _moe_experts(i, blk_start, buf_tok, h2, moe_w_gate, moe_w_up, moe_w_down)
        xs = _combine(dest, xs, mod3, wt, yb, rows_out)

    return _final_norm(xs, final_g).reshape(B, SEQ, D)
```

```python
import functools

import jax
import jax.numpy as jnp
import numpy as np
from jax import lax
from jax.experimental import pallas as pl
from jax.experimental.pallas import tpu as pltpu

F32 = jnp.float32
BF16 = jnp.bfloat16

D = 2048
B = 4
SEQ = 2048
CTX = 256
DEPTH = 4
GRID_W = 64
EPS = 1e-6
POS_BASE = 10000.0
N_MOD = 6

H = 4
DK = 256
DV = 512
QK = H * DK
VV = H * DV
RANK = 16
GATE_NORM = 16.0
CHUNK = 64
STATE_COLS = QK + VV + 2 * RANK
ZPAD = 256
PROJ_N = QK + VV + QK + VV + ZPAD
COL_K, COL_V, COL_Q, COL_R, COL_Z = 0, QK, QK + VV, 2 * QK + VV, 2 * QK + 2 * VV

FG = 4
FD = D // FG

MG = 4
MPG = 8
ME = MG * MPG
MDE = D // 4
TME = 256

LAT = B * SEQ
NCTX = B * CTX
ROWS = LAT + NCTX
SEGS = 8
CTX_SEG = B

VMEM_LIMIT = 56 * 1024 * 1024


def _cp(*sem):
    return pltpu.CompilerParams(dimension_semantics=sem, vmem_limit_bytes=VMEM_LIMIT)


def _seg_of(i, tm):
    return jnp.where(i < LAT // tm, i // (SEQ // tm), CTX_SEG)


def _split2(a):
    hi = a.astype(BF16)
    lo = (a - hi.astype(F32)).astype(BF16)
    return hi, lo


def _dot(a, b):
    return jnp.dot(a, b, preferred_element_type=F32)


def _dot3(a, b_hi, b_lo):
    a_hi, a_lo = _split2(a)
    return _dot(a_hi, b_hi) + _dot(a_lo, b_hi) + _dot(a_hi, b_lo)


def _silu(a):
    return a * jax.nn.sigmoid(a)


HALF = D // 2


def _pack_rows(a):
    lo = lax.bitcast_convert_type(a[:, :HALF].astype(BF16).astype(F32), jnp.uint32)
    hi = lax.bitcast_convert_type(a[:, HALF:].astype(BF16).astype(F32), jnp.uint32)
    return (hi & jnp.uint32(0xFFFF0000)) | (lo >> 16)


def _unpack_rows(w):
    lo = lax.bitcast_convert_type(w << 16, F32)
    hi = lax.bitcast_convert_type(w & jnp.uint32(0xFFFF0000), F32)
    return lo, hi


def _mod_body(c_ref, w_ref, b_ref, o_ref):
    s = _silu(c_ref[...])
    w_hi, w_lo = _split2(w_ref[0])
    o_ref[0] = _dot3(s, w_hi, w_lo) + b_ref[0]


def _mod_all(cvec, ada_w, ada_b):
    tn = 512
    return pl.pallas_call(
        _mod_body,
        out_shape=jax.ShapeDtypeStruct((DEPTH, SEGS, N_MOD * D), F32),
        grid=(DEPTH, N_MOD * D // tn),
        in_specs=[
            pl.BlockSpec((SEGS, D), lambda l, j: (0, 0)),
            pl.BlockSpec((1, D, tn), lambda l, j: (l, 0, j)),
            pl.BlockSpec((1, 1, tn), lambda l, j: (l, 0, j)),
        ],
        out_specs=pl.BlockSpec((1, SEGS, tn), lambda l, j: (l, 0, j)),
        compiler_params=_cp("arbitrary", "arbitrary"),
        name="adaln_mod",
    )(cvec, ada_w, ada_b.reshape(DEPTH, 1, N_MOD * D))


def _assemble_body(x_ref, pos_ref, ctx_ref, g_ref, sh_ref, sc_ref, o_ref, h_ref, *, nlat):
    i = pl.program_id(0)

    @pl.when(i < nlat)
    def _():
        o_ref[...] = x_ref[...] + pos_ref[...]

    @pl.when(i >= nlat)
    def _():
        o_ref[...] = ctx_ref[...]

    h_ref[...] = _norm_mod_val(o_ref[...], g_ref[...], sh_ref[...], sc_ref[...]).astype(h_ref.dtype)


def _assemble(x2, pos, ctx2, g, mod3):
    tm = 256
    nlat = LAT // tm
    return pl.pallas_call(
        functools.partial(_assemble_body, nlat=nlat),
        out_shape=(jax.ShapeDtypeStruct((ROWS, D), F32), jax.ShapeDtypeStruct((ROWS, D), BF16)),
        grid=(ROWS // tm,),
        in_specs=[
            pl.BlockSpec((tm, D), lambda i: (jnp.minimum(i, nlat - 1), 0)),
            pl.BlockSpec((tm, D), lambda i: (i % (SEQ // tm), 0)),
            pl.BlockSpec((tm, D), lambda i: (jnp.maximum(i - nlat, 0), 0)),
            pl.BlockSpec((1, D), lambda i: (0, 0)),
            _mod_spec(0, tm),
            _mod_spec(1, tm),
        ],
        out_specs=(pl.BlockSpec((tm, D), lambda i: (i, 0)), pl.BlockSpec((tm, D), lambda i: (i, 0))),
        compiler_params=_cp("arbitrary"),
        name="assemble_stream",
    )(x2, pos, ctx2, g.reshape(1, D), mod3, mod3)


def _norm_mod_val(x, g, sh, sc):
    y = x * lax.rsqrt(jnp.mean(x * x, axis=-1, keepdims=True) + EPS) * g
    return y * (1.0 + sc) + sh


def _mod_spec(which, tm):
    return pl.BlockSpec((None, 1, D), lambda i: (_seg_of(i, tm) * N_MOD + which, 0, 0))


def _route_tile(h2, rwh_ref, rwl_ref, rb_ref, run_s):
    lg = _dot3(h2, rwh_ref[...], rwl_ref[...]) + rb_ref[...]
    lane_i = lax.broadcasted_iota(jnp.int32, lg.shape, 1)
    lane = lane_i.astype(F32)
    neg = jnp.float32(-jnp.inf)
    big = jnp.float32(1024.0)

    glog = jnp.where(lane < MG, lg, neg)
    gmax = jnp.max(glog, axis=-1, keepdims=True)
    gidx = jnp.min(jnp.where(glog == gmax, lane, big), axis=-1, keepdims=True)
    g_w = 1.0 / jnp.sum(jnp.exp(glog - gmax), axis=-1, keepdims=True)

    lo = MG + MPG * gidx
    el = jnp.where((lane >= lo) & (lane < lo + MPG), lg, neg)
    m1 = jnp.max(el, axis=-1, keepdims=True)
    i1 = jnp.min(jnp.where(el == m1, lane, big), axis=-1, keepdims=True)
    el2 = jnp.where(lane == i1, neg, el)
    m2 = jnp.max(el2, axis=-1, keepdims=True)
    i2 = jnp.min(jnp.where(el2 == m2, lane, big), axis=-1, keepdims=True)
    e2 = jnp.exp(m2 - m1)
    den = 1.0 / (1.0 + e2)
    w1 = g_w * den
    w2 = g_w * (e2 * den)
    e_a = i1 - MG
    e_b = i2 - MG

    tm = lg.shape[0]
    r_i = lax.broadcasted_iota(jnp.int32, (tm, tm), 0)
    c_i = lax.broadcasted_iota(jnp.int32, (tm, tm), 1)
    earlier = jnp.where(r_i > c_i, 1.0, 0.0).astype(BF16)
    oh_a = jnp.where(lane == e_a, 1.0, 0.0)
    oh_b = jnp.where(lane == e_b, 1.0, 0.0)
    run = run_s[...]
    rank_a = jnp.sum(oh_a * (_dot(earlier, oh_a.astype(BF16)) + run[0:1, :]), axis=-1, keepdims=True)
    rank_b = jnp.sum(oh_b * (_dot(earlier, oh_b.astype(BF16)) + run[1:2, :]), axis=-1, keepdims=True)
    run_s[0:1, :] = run[0:1, :] + jnp.sum(oh_a, axis=0, keepdims=True)
    run_s[1:2, :] = run[1:2, :] + jnp.sum(oh_b, axis=0, keepdims=True)

    ids = jnp.where(lane_i == 0, e_a, jnp.where(lane_i == 1, e_b, jnp.where(lane_i == 2, rank_a,
                                                                           jnp.where(lane_i == 3, rank_b, 0.0))))
    wts = jnp.where(lane_i == 0, w1, jnp.where(lane_i == 1, w2, 0.0))
    return ids.astype(jnp.int32), wts


def _mix_route_body(al_ref, ac_ref, w_ref, res_ref, gate_ref, g_ref, sh_ref, sc_ref, rwh_ref, rwl_ref, rb_ref,
                    o_ref, h_ref, eid_ref, wt_ref, cnt_ref, acc_s, run_s, *, nlat):
    i = pl.program_id(0)

    @pl.when(i == 0)
    def _():
        run_s[...] = jnp.zeros_like(run_s)

    @pl.when(i < nlat)
    def _():
        acc_s[...] = _dot(al_ref[...], w_ref[...])

    @pl.when(i >= nlat)
    def _():
        acc_s[...] = _dot(ac_ref[...], w_ref[...])

    xn = res_ref[...] + gate_ref[...] * acc_s[...]
    o_ref[...] = xn
    h2 = _norm_mod_val(xn, g_ref[...], sh_ref[...], sc_ref[...])
    h_ref[...] = _pack_rows(h2)
    ids, wts = _route_tile(h2, rwh_ref, rwl_ref, rb_ref, run_s)
    eid_ref[...] = ids
    wt_ref[...] = wts
    cnt_ref[...] = run_s[...]


def _mix_route(a_lat, a_ctx, w, xs, g, mod3, rw_hi, rw_lo, rb, rows):
    tm = 256
    k = a_lat.shape[1]
    nlat = LAT // tm
    return pl.pallas_call(
        functools.partial(_mix_route_body, nlat=nlat),
        out_shape=(
            jax.ShapeDtypeStruct((rows, D), F32),
            jax.ShapeDtypeStruct((rows, HALF), jnp.uint32),
            jax.ShapeDtypeStruct((rows, 128), jnp.int32),
            jax.ShapeDtypeStruct((rows, 128), F32),
            jax.ShapeDtypeStruct((8, 128), F32),
        ),
        grid=(rows // tm,),
        in_specs=[
            pl.BlockSpec((tm, k), lambda i: (jnp.minimum(i, nlat - 1), 0)),
            pl.BlockSpec((tm, k), lambda i: (jnp.maximum(i - nlat, 0), 0)),
            pl.BlockSpec((k, D), lambda i: (0, 0)),
            pl.BlockSpec((tm, D), lambda i: (i, 0)),
            _mod_spec(2, tm),
            pl.BlockSpec((1, D), lambda i: (0, 0)),
            _mod_spec(3, tm),
            _mod_spec(4, tm),
            pl.BlockSpec((D, 128), lambda i: (0, 0)),
            pl.BlockSpec((D, 128), lambda i: (0, 0)),
            pl.BlockSpec((1, 128), lambda i: (0, 0)),
        ],
        out_specs=(
            pl.BlockSpec((tm, D), lambda i: (i, 0)),
            pl.BlockSpec((tm, HALF), lambda i: (i, 0)),
            pl.BlockSpec((tm, 128), lambda i: (i, 0)),
            pl.BlockSpec((tm, 128), lambda i: (i, 0)),
            pl.BlockSpec((8, 128), lambda i: (0, 0)),
        ),
        scratch_shapes=[pltpu.VMEM((tm, D), F32), pltpu.VMEM((8, 128), F32)],
        compiler_params=_cp("arbitrary"),
        name="mix_route",
    )(a_lat, a_ctx, w, xs, mod3, g.reshape(1, D), mod3, mod3, rw_hi, rw_lo, rb)


def _mm_body(x_ref, w_ref, o_ref):
    o_ref[...] = _dot(x_ref[...], w_ref[...]).astype(o_ref.dtype)


def _mm(x, w, rows, tm, tn, out_dtype, name):
    k = x.shape[1]
    n = w.shape[1]
    return pl.pallas_call(
        _mm_body,
        out_shape=jax.ShapeDtypeStruct((rows, n), out_dtype),
        grid=(n // tn, rows // tm),
        in_specs=[
            pl.BlockSpec((tm, k), lambda j, i: (i, 0)),
            pl.BlockSpec((k, tn), lambda j, i: (0, j)),
        ],
        out_specs=pl.BlockSpec((tm, tn), lambda j, i: (i, j)),
        compiler_params=_cp("arbitrary", "arbitrary"),
        name=name,
    )(x, w)


def _log_sigmoid(a):
    return jnp.minimum(a, 0.0) - jnp.log1p(jnp.exp(-jnp.abs(a)))


PREP = 256


def _gla_body(k_ref, v_ref, q_ref, r_ref, z_ref, wgf_ref, bgf_ref, wgb_ref, bgb_ref, on_ref, s0f_ref, s0b_ref,
              o_ref, sf_ref, sb_ref,
              qf_s, kf_s, df_s, qb_s, kb_s, db_s, vb_s, dec_s, o_s, stf_s, stb_s, *, T, emit_o):
    nc = T // CHUNK
    nblk = T // PREP
    cpb = PREP // CHUNK
    scale = DK ** -0.5

    row = lax.broadcasted_iota(jnp.int32, (PREP, PREP), 0)
    col = lax.broadcasted_iota(jnp.int32, (PREP, PREP), 1)
    shift = CHUNK.bit_length() - 1
    same = lax.shift_right_logical(row, shift) == lax.shift_right_logical(col, shift)
    tri_f = jnp.where(same & (row >= col), 1.0, 0.0).astype(BF16)
    tri_b = jnp.where(same & (row <= col), 1.0, 0.0).astype(BF16)
    wf_hi, wf_lo = _split2(wgf_ref[...])
    wb_hi, wb_lo = _split2(wgb_ref[...])

    def chunk_sums(tri, g):
        g1 = g.astype(BF16)
        rem = g - g1.astype(F32)
        g2 = rem.astype(BF16)
        g3 = (rem - g2.astype(F32)).astype(BF16)
        return _dot(tri, g1) + _dot(tri, g2) + _dot(tri, g3)

    def edge_rows(G, e):
        return jnp.concatenate(
            [jnp.broadcast_to(G[c * CHUNK + e:c * CHUNK + e + 1, :], (CHUNK, DK)) for c in range(cpb)], axis=0)

    def prep(blk, carry):
        rows = pl.ds(pl.multiple_of(blk * PREP, PREP), PREP)
        z = z_ref[rows, :128]
        gf = _log_sigmoid(_dot3(z, wf_hi, wf_lo) + bgf_ref[...]) / GATE_NORM
        gb = _log_sigmoid(_dot3(z, wb_hi, wb_lo) + bgb_ref[...]) / GATE_NORM
        Gf = chunk_sums(tri_f, gf)
        Gb = chunk_sums(tri_b, gb)
        k = k_ref[rows, :]
        q = q_ref[rows, :] * scale
        qf_s[rows, :] = (q * jnp.exp(Gf)).astype(BF16)
        kf_s[rows, :] = (k * jnp.exp(-Gf)).astype(BF16)
        df_s[rows, :] = (k * jnp.exp(edge_rows(Gf, CHUNK - 1) - Gf)).astype(BF16)
        qb_s[rows, :] = (q * jnp.exp(Gb)).astype(BF16)
        kb_s[rows, :] = (k * jnp.exp(-Gb)).astype(BF16)
        db_s[rows, :] = (k * jnp.exp(edge_rows(Gb, 0) - Gb)).astype(BF16)
        vb_s[rows, :] = v_ref[rows, :].astype(BF16)
        for c in range(cpb):
            ef = Gf[c * CHUNK + CHUNK - 1:c * CHUNK + CHUNK, :]
            eb = Gb[c * CHUNK:c * CHUNK + 1, :]
            dec_s[0, blk * cpb + c] = jnp.broadcast_to(jnp.exp(ef), (8, DK))
            dec_s[1, blk * cpb + c] = jnp.broadcast_to(jnp.exp(eb), (8, DK))
        return carry

    lax.fori_loop(0, nblk, prep, 0)

    stf_s[...] = s0f_ref[0, 0]
    stb_s[...] = s0b_ref[0, 0]
    r64 = lax.broadcasted_iota(jnp.int32, (CHUNK, CHUNK), 0)
    c64 = lax.broadcasted_iota(jnp.int32, (CHUNK, CHUNK), 1)
    nt_dims = (((1,), (1,)), ((), ()))
    tn_dims = (((0,), (0,)), ((), ()))

    def chunk(c, q_s, k_s, d_s, st_s, di, keep):
        rows = pl.ds(pl.multiple_of(c * CHUNK, CHUNK), CHUNK)
        qe = q_s[rows, :]
        vb = vb_s[rows, :]
        st = st_s[...]
        o = None
        if emit_o:
            a = lax.dot_general(qe, k_s[rows, :], nt_dims, preferred_element_type=F32)
            a = jnp.where(keep, a, 0.0).astype(BF16)
            o = _dot(a, vb) + lax.dot_general(qe, st.astype(BF16), nt_dims, preferred_element_type=F32)
        upd = lax.dot_general(vb, d_s[rows, :], tn_dims, preferred_element_type=F32)
        st_s[...] = st * dec_s[di, c][0:1, :] + upd
        return o, rows

    def step(ci, carry, accumulate):
        of, rows_f = chunk(ci, qf_s, kf_s, df_s, stf_s, 0, r64 >= c64)
        ob, rows_b = chunk(nc - 1 - ci, qb_s, kb_s, db_s, stb_s, 1, r64 <= c64)
        if emit_o:
            if accumulate:
                o_s[rows_f, :] = o_s[rows_f, :] + of
                o_s[rows_b, :] = o_s[rows_b, :] + ob
            else:
                o_s[rows_f, :] = of
                o_s[rows_b, :] = ob
        return carry

    lax.fori_loop(0, nc // 2, functools.partial(step, accumulate=False), 0, unroll=2)
    lax.fori_loop(nc // 2, nc, functools.partial(step, accumulate=True), 0, unroll=2)
    sf_ref[0, 0] = stf_s[...]
    sb_ref[0, 0] = stb_s[...]

    if emit_o:
        def fin(blk, carry):
            rows = pl.ds(pl.multiple_of(blk * PREP, PREP), PREP)
            tot = o_s[rows, :]
            y = tot * lax.rsqrt(jnp.mean(tot * tot, axis=-1, keepdims=True) + EPS) * on_ref[...]
            o_ref[rows, :] = (y * _silu(r_ref[rows, :])).astype(o_ref.dtype)
            return carry

        lax.fori_loop(0, nblk, fin, 0)
    else:
        o_ref[...] = jnp.zeros_like(o_ref)


def _gla(proj, wgf, bgf, wgb, bgb, onorm, s0f, s0b, T, row_off, emit_o):
    rb = row_off // T
    st_spec = pl.BlockSpec((1, 1, DV, DK), lambda b, h: (b, h, 0, 0))
    return pl.pallas_call(
        functools.partial(_gla_body, T=T, emit_o=emit_o),
        out_shape=(
            jax.ShapeDtypeStruct((B * T, VV), BF16),
            jax.ShapeDtypeStruct((B, H, DV, DK), F32),
            jax.ShapeDtypeStruct((B, H, DV, DK), F32),
        ),
        grid=(B, H),
        in_specs=[
            pl.BlockSpec((T, DK), lambda b, h: (rb + b, COL_K // DK + h)),
            pl.BlockSpec((T, DV), lambda b, h: (rb + b, COL_V // DV + h)),
            pl.BlockSpec((T, DK), lambda b, h: (rb + b, COL_Q // DK + h)),
            pl.BlockSpec((T, DV), lambda b, h: (rb + b, COL_R // DV + h)),
            pl.BlockSpec((T, ZPAD), lambda b, h: (rb + b, COL_Z // ZPAD)),
            pl.BlockSpec((128, DK), lambda b, h: (0, h)),
            pl.BlockSpec((1, DK), lambda b, h: (0, h)),
            pl.BlockSpec((128, DK), lambda b, h: (0, h)),
            pl.BlockSpec((1, DK), lambda b, h: (0, h)),
            pl.BlockSpec((1, DV), lambda b, h: (0, 0)),
            st_spec,
            st_spec,
        ],
        out_specs=(
            pl.BlockSpec((T, DV), lambda b, h: (b, h)),
            st_spec,
            st_spec,
        ),
        scratch_shapes=[pltpu.VMEM((T, DK), BF16)] * 6 + [
            pltpu.VMEM((T, DV), BF16),
            pltpu.VMEM((2, T // CHUNK, 8, DK), F32),
            pltpu.VMEM((T, DV), F32),
            pltpu.VMEM((DV, DK), F32),
            pltpu.VMEM((DV, DK), F32),
        ],
        compiler_params=_cp("arbitrary", "arbitrary"),
        name="gla_scan",
    )(proj, proj, proj, proj, proj, wgf, bgf, wgb, bgb, onorm, s0f, s0b)


def _fnet_in_body(h_ref, w_ref, cs_ref, a_ref, b_ref):
    u = _dot(h_ref[...], w_ref[...]).astype(BF16)
    for g in range(FG):
        ab = _dot(u[:, g * FD:(g + 1) * FD], cs_ref[...])
        a_ref[:, g * FD:(g + 1) * FD] = ab[:, :FD].astype(BF16)
        b_ref[:, g * FD:(g + 1) * FD] = ab[:, FD:].astype(BF16)


def _fnet_in(h, w, cs, rows):
    tm = 512
    return pl.pallas_call(
        _fnet_in_body,
        out_shape=(jax.ShapeDtypeStruct((rows, D), BF16), jax.ShapeDtypeStruct((rows, D), BF16)),
        grid=(rows // tm,),
        in_specs=[
            pl.BlockSpec((tm, D), lambda i: (i, 0)),
            pl.BlockSpec((D, D), lambda i: (0, 0)),
            pl.BlockSpec((FD, 2 * FD), lambda i: (0, 0)),
        ],
        out_specs=(pl.BlockSpec((tm, D), lambda i: (i, 0)), pl.BlockSpec((tm, D), lambda i: (i, 0))),
        compiler_params=_cp("arbitrary"),
        name="fnet_in",
    )(h, w, cs)


def _fnet_time_body(ct_ref, st_ref, a_ref, b_ref, o_ref):
    o_ref[...] = (_dot(ct_ref[...], a_ref[...]) + _dot(st_ref[...], b_ref[...])).astype(o_ref.dtype)


def _fnet_time(ct, st, a, b, T, row_off):
    tm = min(T, 512)
    tn = 1024
    rb = row_off // T
    return pl.pallas_call(
        _fnet_time_body,
        out_shape=jax.ShapeDtypeStruct((B * T, D), BF16),
        grid=(B, D // tn, T // tm),
        in_specs=[
            pl.BlockSpec((tm, T), lambda s, j, i: (i, 0)),
            pl.BlockSpec((tm, T), lambda s, j, i: (i, 0)),
            pl.BlockSpec((T, tn), lambda s, j, i: (rb + s, j)),
            pl.BlockSpec((T, tn), lambda s, j, i: (rb + s, j)),
        ],
        out_specs=pl.BlockSpec((tm, tn), lambda s, j, i: (s * (T // tm) + i, j)),
        compiler_params=_cp("arbitrary", "arbitrary", "arbitrary"),
        name="fnet_time",
    )(ct, st, a, b)


def _dft_mats(n, scale):
    idx = np.arange(n, dtype=np.int64)
    ang = 2.0 * np.pi * ((idx[:, None] * idx[None, :]) % n).astype(np.float64) / n
    return np.cos(ang) * scale, np.sin(ang) * scale


def _row_copy(src_hbm, row, dst_vmem, r, sem):
    return pltpu.make_async_copy(src_hbm.at[pl.ds(row, 1)], dst_vmem.at[pl.ds(r, 1)], sem)


def _moe_body(bs_ref, tok_ref, h_hbm, wg_ref, wu_ref, wd_ref, y_hbm, xbuf, obuf, gsem, osem, wgb, wub, wdb, *, nb):
    e = pl.program_id(0)
    b0 = bs_ref[e]
    b1 = bs_ref[e + 1]
    total = bs_ref[ME]

    def gather(g):
        slot = g % 2
        base = g * TME
        for r in range(TME):
            _row_copy(h_hbm, tok_ref[base + r], xbuf.at[slot], r, gsem.at[slot]).start(priority=r % 2)

    def out_copy(g):
        slot = g % 2
        dst = y_hbm.at[pl.ds(pl.multiple_of(g * TME, TME), TME)]
        return pltpu.make_async_copy(obuf.at[slot], dst, osem.at[slot])

    @pl.when((e == 0) & (total > 0))
    def _():
        gather(0)

    @pl.when(b1 > b0)
    def _():
        wgb[...] = wg_ref[...].astype(BF16)
        wub[...] = wu_ref[...].astype(BF16)
        wdb[...] = wd_ref[...].astype(BF16)

    def block(g, carry):
        slot = g % 2

        @pl.when(g + 1 < total)
        def _():
            gather(g + 1)

        pltpu.make_async_copy(h_hbm.at[pl.ds(0, TME)], xbuf.at[slot], gsem.at[slot]).wait()
        x_lo, x_hi = _unpack_rows(xbuf[slot])
        x = jnp.concatenate([x_lo.astype(BF16), x_hi.astype(BF16)], axis=1)
        hmid = _silu(_dot(x, wgb[...])) * _dot(x, wub[...])
        y = _pack_rows(_dot(hmid.astype(BF16), wdb[...]))

        @pl.when(g >= 2)
        def _():
            out_copy(g - 2).wait()

        obuf[slot] = y
        out_copy(g).start()
        return carry

    lax.fori_loop(b0, b1, block, 0)

    @pl.when(e == ME - 1)
    def _():
        @pl.when(total >= 2)
        def _():
            out_copy(total - 2).wait()

        @pl.when(total >= 1)
        def _():
            out_copy(total - 1).wait()

        def zero_copy(g):
            dst = y_hbm.at[pl.ds(pl.multiple_of(g * TME, TME), TME)]
            return pltpu.make_async_copy(obuf.at[0], dst, osem.at[0])

        obuf[0] = jnp.zeros((TME, HALF), jnp.uint32)
        lax.fori_loop(total, nb, lambda g, c: (zero_copy(g).start(), c)[1], 0)
        lax.fori_loop(total, nb, lambda g, c: (zero_copy(g).wait(), c)[1], 0)


def _moe_experts(layer, blk_start, buf_tok, h2, w_gate, w_up, w_down):
    nb = buf_tok.shape[0] // TME
    w_in_spec = pl.BlockSpec((None, None, D, MDE), lambda e, bs, tk: (layer, e, 0, 0))
    return pl.pallas_call(
        functools.partial(_moe_body, nb=nb),
        out_shape=jax.ShapeDtypeStruct((nb * TME, HALF), jnp.uint32),
        grid_spec=pltpu.PrefetchScalarGridSpec(
            num_scalar_prefetch=2,
            grid=(ME,),
            in_specs=[
                pl.BlockSpec(memory_space=pl.ANY),
                w_in_spec,
                w_in_spec,
                pl.BlockSpec((None, None, MDE, D), lambda e, bs, tk: (layer, e, 0, 0)),
            ],
            out_specs=pl.BlockSpec(memory_space=pl.ANY),
            scratch_shapes=[
                pltpu.VMEM((2, TME, HALF), jnp.uint32),
                pltpu.VMEM((2, TME, HALF), jnp.uint32),
                pltpu.SemaphoreType.DMA((2,)),
                pltpu.SemaphoreType.DMA((2,)),
                pltpu.VMEM((D, MDE), BF16),
                pltpu.VMEM((D, MDE), BF16),
                pltpu.VMEM((MDE, D), BF16),
            ],
        ),
        compiler_params=_cp("arbitrary"),
        name="moe_experts",
    )(blk_start, buf_tok, h2, w_gate, w_up, w_down)


TMC = 128


def _combine_body(dest_ref, x_ref, gate_ref, wt_ref, g_ref, sh_ref, sc_ref, y_hbm, *rest, nt, last):
    if last:
        h_ref, ybuf, sem = rest
    else:
        o_ref, h_ref, ybuf, sem = rest
    i = pl.program_id(0)

    def gather(tile, slot):
        base = tile * (2 * TMC)

        for r in range(TMC):
            _row_copy(y_hbm, dest_ref[base + 2 * r], ybuf.at[slot, 0], r, sem.at[slot]).start(priority=0)
            _row_copy(y_hbm, dest_ref[base + 2 * r + 1], ybuf.at[slot, 1], r, sem.at[slot]).start(priority=1)

    @pl.when(i == 0)
    def _():
        gather(0, 0)

    @pl.when(i + 1 < nt)
    def _():
        gather(i + 1, (i + 1) % 2)

    slot = i % 2
    for k in range(2):
        pltpu.make_async_copy(y_hbm.at[pl.ds(0, TMC)], ybuf.at[slot, k], sem.at[slot]).wait()
    wt = wt_ref[...]
    w0, w1 = wt[:, 0:1], wt[:, 1:2]
    lo0, hi0 = _unpack_rows(ybuf[slot, 0])
    lo1, hi1 = _unpack_rows(ybuf[slot, 1])
    y = jnp.concatenate([w0 * lo0 + w1 * lo1, w0 * hi0 + w1 * hi1], axis=1)
    xn = x_ref[...] + gate_ref[...] * y
    if last:
        h_ref[...] = xn * lax.rsqrt(jnp.mean(xn * xn, axis=-1, keepdims=True) + EPS) * g_ref[...]
    else:
        o_ref[...] = xn
        h_ref[...] = _norm_mod_val(xn, g_ref[...], sh_ref[...], sc_ref[...]).astype(h_ref.dtype)


def _combine(dest, xs, mod3, wt, yb, rows, g_next, mod3_next, last):
    nt = rows // TMC
    row_spec = pl.BlockSpec((TMC, D), lambda i, d: (i, 0))

    def mod_spec(which):
        return pl.BlockSpec((None, 1, D), lambda i, d: (_seg_of(i, TMC) * N_MOD + which, 0, 0))

    if last:
        out_shape = jax.ShapeDtypeStruct((rows, D), F32)
        out_specs = row_spec
    else:
        out_shape = (jax.ShapeDtypeStruct((rows, D), F32), jax.ShapeDtypeStruct((rows, D), BF16))
        out_specs = (row_spec, row_spec)
    return pl.pallas_call(
        functools.partial(_combine_body, nt=nt, last=last),
        out_shape=out_shape,
        grid_spec=pltpu.PrefetchScalarGridSpec(
            num_scalar_prefetch=1,
            grid=(nt,),
            in_specs=[
                row_spec,
                mod_spec(5),
                pl.BlockSpec((TMC, 128), lambda i, d: (i, 0)),
                pl.BlockSpec((1, D), lambda i, d: (0, 0)),
                mod_spec(0),
                mod_spec(1),
                pl.BlockSpec(memory_space=pl.ANY),
            ],
            out_specs=out_specs,
            scratch_shapes=[
                pltpu.VMEM((2, 2, TMC, HALF), jnp.uint32),
                pltpu.SemaphoreType.DMA((2,)),
            ],
        ),
        compiler_params=_cp("arbitrary"),
        name="moe_combine",
    )(dest, xs, mod3, wt, g_next.reshape(1, D), mod3_next, mod3_next, yb)


PLAN_CHUNK = 1024


def _plan_body(base_ref, er_ref, dest_ref, tok_ref, *, nslots):
    i = pl.program_id(0)

    @pl.when(i == 0)
    def _():
        def clear(s, carry):
            tok_ref[s] = 0
            return carry

        lax.fori_loop(0, nslots, clear, 0, unroll=8)

    def place(t, carry):
        d0 = base_ref[er_ref[0, 4 * t]] + er_ref[0, 4 * t + 2]
        d1 = base_ref[ME + er_ref[0, 4 * t + 1]] + er_ref[0, 4 * t + 3]
        dest_ref[0, 2 * t] = d0
        dest_ref[0, 2 * t + 1] = d1
        tok = i * PLAN_CHUNK + t
        tok_ref[d0] = tok
        tok_ref[d1] = tok
        return carry

    lax.fori_loop(0, PLAN_CHUNK, place, 0, unroll=8)


def _route_plan(eid, cnt, rows):
    a = rows * 2
    nb = -(-a // TME) + ME
    steps = rows // PLAN_CHUNK
    c0 = cnt[0, :ME].astype(jnp.int32)
    c1 = cnt[1, :ME].astype(jnp.int32)
    padded = ((c0 + c1 + TME - 1) // TME) * TME
    pad_end = jnp.cumsum(padded)
    pad_start = pad_end - padded
    base = jnp.concatenate([pad_start, pad_start + c0]).astype(jnp.int32)
    blk_start = jnp.concatenate([jnp.zeros((1,), jnp.int32), (pad_end // TME).astype(jnp.int32)])
    er = eid[:, :4].reshape(steps, 1, 4 * PLAN_CHUNK)
    dest, buf_tok = pl.pallas_call(
        functools.partial(_plan_body, nslots=nb * TME),
        out_shape=(
            jax.ShapeDtypeStruct((steps, 1, 2 * PLAN_CHUNK), jnp.int32),
            jax.ShapeDtypeStruct((nb * TME,), jnp.int32),
        ),
        grid_spec=pltpu.PrefetchScalarGridSpec(
            num_scalar_prefetch=1,
            grid=(steps,),
            in_specs=[pl.BlockSpec((None, 1, 4 * PLAN_CHUNK), lambda i, b: (i, 0, 0), memory_space=pltpu.SMEM)],
            out_specs=(
                pl.BlockSpec((None, 1, 2 * PLAN_CHUNK), lambda i, b: (i, 0, 0), memory_space=pltpu.SMEM),
                pl.BlockSpec(memory_space=pltpu.SMEM),
            ),
        ),
        compiler_params=_cp("arbitrary"),
        name="route_plan",
    )(base, er)
    return dest.reshape(a), buf_tok, blk_start


def _sincos_2d(rows, cols, d):
    quarter = d // 4
    omega = 1.0 / (POS_BASE ** (jnp.arange(quarter, dtype=F32) / quarter))

    def axis_emb(n):
        p = jnp.arange(n, dtype=F32)[:, None] * omega[None, :]
        return jnp.concatenate([jnp.sin(p), jnp.cos(p)], axis=-1)

    er, ec = axis_emb(rows), axis_emb(cols)
    half = 2 * quarter
    pos = jnp.concatenate([jnp.broadcast_to(er[:, None, :], (rows, cols, half)),
                           jnp.broadcast_to(ec[None, :, :], (rows, cols, half))], axis=-1)
    return pos.reshape(rows * cols, 2 * half)


def _gla_weights(w_in, wg_f, wg_b):
    z = jnp.zeros((D, ZPAD - 2 * RANK), F32)
    w_cat = jnp.concatenate([
        w_in[:, :QK], w_in[:, QK:QK + VV],
        w_in[:, STATE_COLS:STATE_COLS + QK], w_in[:, STATE_COLS + QK:],
        w_in[:, QK + VV:STATE_COLS], z], axis=1).astype(BF16)
    wgf = jnp.zeros((128, QK), F32).at[:RANK].set(wg_f)
    wgb = jnp.zeros((128, QK), F32).at[RANK:2 * RANK].set(wg_b)
    return w_cat, wgf, wgb


def kernel(x, c, ctx, c_ctx, ada_w, ada_b, norm1_g, norm2_g, gla_w_in, gla_wg_f, gla_bg_f, gla_wg_b, gla_bg_b,
           gla_onorm_g, gla_w_out, fnet_w_in, fnet_w_out, moe_rw_group, moe_rb_group, moe_rw_expert,
           moe_rb_expert, moe_w_gate, moe_w_up, moe_w_down, final_g):
    cvec = jnp.zeros((SEGS, D), F32).at[:B].set(c).at[CTX_SEG].set(c_ctx)
    mods = _mod_all(cvec, ada_w, ada_b)
    pos = _sincos_2d(SEQ // GRID_W, GRID_W, D)
    mod3s = [mods[i].reshape(SEGS * N_MOD, 1, D) for i in range(DEPTH)]
    xs, h = _assemble(x.reshape(LAT, D), pos, ctx.reshape(NCTX, D), norm1_g[0], mod3s[0])

    cc, sc = _dft_mats(FD, 1.0)
    cs = jnp.asarray(np.concatenate([cc, sc], axis=1), BF16)
    ct_l, st_l = _dft_mats(SEQ, (SEQ * FD) ** -0.5)
    ct_c, st_c = _dft_mats(CTX, (CTX * FD) ** -0.5)
    ct_l, st_l = jnp.asarray(ct_l, BF16), jnp.asarray(-st_l, BF16)
    ct_c, st_c = jnp.asarray(ct_c, BF16), jnp.asarray(-st_c, BF16)
    s_zero = jnp.zeros((B, H, DV, DK), F32)
    last_reader = ((DEPTH - 1) // 2) * 2

    for i in range(DEPTH):
        kind, j = i % 2, i // 2
        ctx_live = i < last_reader
        ctx_needed = i <= last_reader
        mod3 = mod3s[i]
        rows_in = ROWS if ctx_needed else LAT
        rows_out = ROWS if ctx_live else LAT

        if kind == 0:
            w_cat, wgf, wgb = _gla_weights(gla_w_in[j], gla_wg_f[j], gla_wg_b[j])
            bgf, bgb = gla_bg_f[j].reshape(1, QK), gla_bg_b[j].reshape(1, QK)
            onorm = gla_onorm_g[j].reshape(1, DV)
            proj = _mm(h, w_cat, rows_in, 512, 1280, F32, "gla_proj")
            o_c, sf, sb = _gla(proj, wgf, bgf, wgb, bgb, onorm, s_zero, s_zero, CTX, LAT, ctx_live)
            o_l, _, _ = _gla(proj, wgf, bgf, wgb, bgb, onorm, sf, sb, SEQ, 0, True)
            mix_l, mix_c, w_out = o_l, o_c, gla_w_out[j].astype(BF16)
        else:
            a, b = _fnet_in(h, fnet_w_in[j].astype(BF16), cs, rows_in)
            y = _fnet_time(ct_l, st_l, a, b, SEQ, 0)
            y_c = _fnet_time(ct_c, st_c, a, b, CTX, LAT) if ctx_live else y
            mix_l, mix_c, w_out = y, y_c, fnet_w_out[j].astype(BF16)

        rw = jnp.zeros((D, 128), F32).at[:, :MG].set(moe_rw_group[i])
        rw = rw.at[:, MG:MG + ME].set(jnp.transpose(moe_rw_expert[i], (1, 0, 2)).reshape(D, ME))
        rb = jnp.zeros((1, 128), F32).at[0, :MG].set(moe_rb_group[i]).at[0, MG:MG + ME].set(
            moe_rb_expert[i].reshape(ME))
        rw_hi = rw.astype(BF16)
        rw_lo = (rw - rw_hi.astype(F32)).astype(BF16)
        xs, h2, eid, wt, cnt = _mix_route(mix_l, mix_c, w_out, xs, norm2_g[i], mod3, rw_hi, rw_lo, rb, rows_out)
        dest, buf_tok, blk_start = _route_plan(eid, cnt, rows_out)
        yb = _moe_experts(i, blk_start, buf_tok, h2, moe_w_gate, moe_w_up, moe_w_down)
        if i + 1 < DEPTH:
            xs, h = _combine(dest, xs, mod3, wt, yb, rows_out, norm1_g[i + 1], mod3s[i + 1], False)
        else:
            out = _combine(dest, xs, mod3, wt, yb, rows_out, final_g, mod3, True)

    return out.reshape(B, SEQ, D)
```

```python
import functools

import jax
import jax.numpy as jnp
import numpy as np
from jax import lax
from jax.experimental import pallas as pl
from jax.experimental.pallas import tpu as pltpu

F32 = jnp.float32
BF16 = jnp.bfloat16

D = 2048
B = 4
SEQ = 2048
CTX = 256
DEPTH = 4
GRID_W = 64
EPS = 1e-6
POS_BASE = 10000.0
N_MOD = 6

H = 4
DK = 256
DV = 512
QK = H * DK
VV = H * DV
RANK = 16
GATE_NORM = 16.0
CHUNK = 64
STATE_COLS = QK + VV + 2 * RANK
ZPAD = 256
PROJ_N = QK + VV + QK + VV + ZPAD
COL_K, COL_V, COL_Q, COL_R, COL_Z = 0, QK, QK + VV, 2 * QK + VV, 2 * QK + 2 * VV

FG = 4
FD = D // FG

MG = 4
MPG = 8
ME = MG * MPG
MDE = D // 4
TME = 256

LAT = B * SEQ
NCTX = B * CTX
ROWS = LAT + NCTX
SEGS = 8
CTX_SEG = B

VMEM_LIMIT = 56 * 1024 * 1024


def _cp(*sem):
    return pltpu.CompilerParams(dimension_semantics=sem, vmem_limit_bytes=VMEM_LIMIT)


def _seg_of(i, tm):
    return jnp.where(i < LAT // tm, i // (SEQ // tm), CTX_SEG)


def _split2(a):
    hi = a.astype(BF16)
    lo = (a - hi.astype(F32)).astype(BF16)
    return hi, lo


def _dot(a, b):
    return jnp.dot(a, b, preferred_element_type=F32)


def _dot3(a, b_hi, b_lo):
    a_hi, a_lo = _split2(a)
    return _dot(a_hi, b_hi) + _dot(a_lo, b_hi) + _dot(a_hi, b_lo)


def _silu(a):
    return a * jax.nn.sigmoid(a)


HALF = D // 2


def _pack_rows(a):
    lo = lax.bitcast_convert_type(a[:, :HALF].astype(BF16).astype(F32), jnp.uint32)
    hi = lax.bitcast_convert_type(a[:, HALF:].astype(BF16).astype(F32), jnp.uint32)
    return (hi & jnp.uint32(0xFFFF0000)) | (lo >> 16)


def _unpack_rows(w):
    lo = lax.bitcast_convert_type(w << 16, F32)
    hi = lax.bitcast_convert_type(w & jnp.uint32(0xFFFF0000), F32)
    return lo, hi


def _mod_body(c_ref, w_ref, b_ref, o_ref):
    s = _silu(c_ref[...])
    w_hi, w_lo = _split2(w_ref[0])
    o_ref[0] = _dot3(s, w_hi, w_lo) + b_ref[0]


def _mod_all(cvec, ada_w, ada_b):
    tn = 512
    return pl.pallas_call(
        _mod_body,
        out_shape=jax.ShapeDtypeStruct((DEPTH, SEGS, N_MOD * D), F32),
        grid=(DEPTH, N_MOD * D // tn),
        in_specs=[
            pl.BlockSpec((SEGS, D), lambda l, j: (0, 0)),
            pl.BlockSpec((1, D, tn), lambda l, j: (l, 0, j)),
            pl.BlockSpec((1, 1, tn), lambda l, j: (l, 0, j)),
        ],
        out_specs=pl.BlockSpec((1, SEGS, tn), lambda l, j: (l, 0, j)),
        compiler_params=_cp("arbitrary", "arbitrary"),
        name="adaln_mod",
    )(cvec, ada_w, ada_b.reshape(DEPTH, 1, N_MOD * D))


def _assemble_body(x_ref, pos_ref, ctx_ref, g_ref, sh_ref, sc_ref, o_ref, h_ref, *, nlat):
    i = pl.program_id(0)

    @pl.when(i < nlat)
    def _():
        o_ref[...] = x_ref[...] + pos_ref[...]

    @pl.when(i >= nlat)
    def _():
        o_ref[...] = ctx_ref[...]

    h_ref[...] = _norm_mod_val(o_ref[...], g_ref[...], sh_ref[...], sc_ref[...]).astype(h_ref.dtype)


def _assemble(x2, pos, ctx2, g, mod3):
    tm = 256
    nlat = LAT // tm
    return pl.pallas_call(
        functools.partial(_assemble_body, nlat=nlat),
        out_shape=(jax.ShapeDtypeStruct((ROWS, D), F32), jax.ShapeDtypeStruct((ROWS, D), BF16)),
        grid=(ROWS // tm,),
        in_specs=[
            pl.BlockSpec((tm, D), lambda i: (jnp.minimum(i, nlat - 1), 0)),
            pl.BlockSpec((tm, D), lambda i: (i % (SEQ // tm), 0)),
            pl.BlockSpec((tm, D), lambda i: (jnp.maximum(i - nlat, 0), 0)),
            pl.BlockSpec((1, D), lambda i: (0, 0)),
            _mod_spec(0, tm),
            _mod_spec(1, tm),
        ],
        out_specs=(pl.BlockSpec((tm, D), lambda i: (i, 0)), pl.BlockSpec((tm, D), lambda i: (i, 0))),
        compiler_params=_cp("arbitrary"),
        name="assemble_stream",
    )(x2, pos, ctx2, g.reshape(1, D), mod3, mod3)


def _norm_mod_val(x, g, sh, sc):
    y = x * lax.rsqrt(jnp.mean(x * x, axis=-1, keepdims=True) + EPS) * g
    return y * (1.0 + sc) + sh


def _mod_spec(which, tm):
    return pl.BlockSpec((None, 1, D), lambda i: (_seg_of(i, tm) * N_MOD + which, 0, 0))


def _route_tile(h2, rwh_ref, rwl_ref, rb_ref, run_s):
    lg = _dot3(h2, rwh_ref[...], rwl_ref[...]) + rb_ref[...]
    lane_i = lax.broadcasted_iota(jnp.int32, lg.shape, 1)
    lane = lane_i.astype(F32)
    neg = jnp.float32(-jnp.inf)
    big = jnp.float32(1024.0)

    glog = jnp.where(lane < MG, lg, neg)
    gmax = jnp.max(glog, axis=-1, keepdims=True)
    gidx = jnp.min(jnp.where(glog == gmax, lane, big), axis=-1, keepdims=True)
    g_w = 1.0 / jnp.sum(jnp.exp(glog - gmax), axis=-1, keepdims=True)

    lo = MG + MPG * gidx
    el = jnp.where((lane >= lo) & (lane < lo + MPG), lg, neg)
    m1 = jnp.max(el, axis=-1, keepdims=True)
    i1 = jnp.min(jnp.where(el == m1, lane, big), axis=-1, keepdims=True)
    el2 = jnp.where(lane == i1, neg, el)
    m2 = jnp.max(el2, axis=-1, keepdims=True)
    i2 = jnp.min(jnp.where(el2 == m2, lane, big), axis=-1, keepdims=True)
    e2 = jnp.exp(m2 - m1)
    den = 1.0 / (1.0 + e2)
    w1 = g_w * den
    w2 = g_w * (e2 * den)
    e_a = i1 - MG
    e_b = i2 - MG

    tm = lg.shape[0]
    r_i = lax.broadcasted_iota(jnp.int32, (tm, tm), 0)
    c_i = lax.broadcasted_iota(jnp.int32, (tm, tm), 1)
    earlier = jnp.where(r_i > c_i, 1.0, 0.0).astype(BF16)
    oh_a = jnp.where(lane == e_a, 1.0, 0.0)
    oh_b = jnp.where(lane == e_b, 1.0, 0.0)
    run = run_s[...]
    rank_a = jnp.sum(oh_a * (_dot(earlier, oh_a.astype(BF16)) + run[0:1, :]), axis=-1, keepdims=True)
    rank_b = jnp.sum(oh_b * (_dot(earlier, oh_b.astype(BF16)) + run[1:2, :]), axis=-1, keepdims=True)
    run_s[0:1, :] = run[0:1, :] + jnp.sum(oh_a, axis=0, keepdims=True)
    run_s[1:2, :] = run[1:2, :] + jnp.sum(oh_b, axis=0, keepdims=True)

    ids = jnp.where(lane_i == 0, e_a, jnp.where(lane_i == 1, e_b, jnp.where(lane_i == 2, rank_a,
                                                                           jnp.where(lane_i == 3, rank_b, 0.0))))
    wts = jnp.where(lane_i == 0, w1, jnp.where(lane_i == 1, w2, 0.0))
    return ids.astype(jnp.int32), wts


def _mix_route_body(al_ref, ac_ref, w_ref, res_ref, gate_ref, g_ref, sh_ref, sc_ref, rwh_ref, rwl_ref, rb_ref,
                    o_ref, h_ref, eid_ref, wt_ref, cnt_ref, acc_s, run_s, *, nlat):
    i = pl.program_id(0)

    @pl.when(i == 0)
    def _():
        run_s[...] = jnp.zeros_like(run_s)

    @pl.when(i < nlat)
    def _():
        acc_s[...] = _dot(al_ref[...], w_ref[...])

    @pl.when(i >= nlat)
    def _():
        acc_s[...] = _dot(ac_ref[...], w_ref[...])

    xn = res_ref[...] + gate_ref[...] * acc_s[...]
    o_ref[...] = xn
    h2 = _norm_mod_val(xn, g_ref[...], sh_ref[...], sc_ref[...])
    h_ref[...] = _pack_rows(h2)
    ids, wts = _route_tile(h2, rwh_ref, rwl_ref, rb_ref, run_s)
    eid_ref[...] = ids
    wt_ref[...] = wts
    cnt_ref[...] = run_s[...]


def _mix_route(a_lat, a_ctx, w, xs, g, mod3, rw_hi, rw_lo, rb, rows):
    tm = 256
    k = a_lat.shape[1]
    nlat = LAT // tm
    return pl.pallas_call(
        functools.partial(_mix_route_body, nlat=nlat),
        out_shape=(
            jax.ShapeDtypeStruct((rows, D), F32),
            jax.ShapeDtypeStruct((rows, HALF), jnp.uint32),
            jax.ShapeDtypeStruct((rows, 128), jnp.int32),
            jax.ShapeDtypeStruct((rows, 128), F32),
            jax.ShapeDtypeStruct((8, 128), F32),
        ),
        grid=(rows // tm,),
        in_specs=[
            pl.BlockSpec((tm, k), lambda i: (jnp.minimum(i, nlat - 1), 0)),
            pl.BlockSpec((tm, k), lambda i: (jnp.maximum(i - nlat, 0), 0)),
            pl.BlockSpec((k, D), lambda i: (0, 0)),
            pl.BlockSpec((tm, D), lambda i: (i, 0)),
            _mod_spec(2, tm),
            pl.BlockSpec((1, D), lambda i: (0, 0)),
            _mod_spec(3, tm),
            _mod_spec(4, tm),
            pl.BlockSpec((D, 128), lambda i: (0, 0)),
            pl.BlockSpec((D, 128), lambda i: (0, 0)),
            pl.BlockSpec((1, 128), lambda i: (0, 0)),
        ],
        out_specs=(
            pl.BlockSpec((tm, D), lambda i: (i, 0)),
            pl.BlockSpec((tm, HALF), lambda i: (i, 0)),
            pl.BlockSpec((tm, 128), lambda i: (i, 0)),
            pl.BlockSpec((tm, 128), lambda i: (i, 0)),
            pl.BlockSpec((8, 128), lambda i: (0, 0)),
        ),
        scratch_shapes=[pltpu.VMEM((tm, D), F32), pltpu.VMEM((8, 128), F32)],
        compiler_params=_cp("arbitrary"),
        name="mix_route",
    )(a_lat, a_ctx, w, xs, mod3, g.reshape(1, D), mod3, mod3, rw_hi, rw_lo, rb)


def _mm_body(x_ref, w_ref, o_ref):
    o_ref[...] = _dot(x_ref[...], w_ref[...]).astype(o_ref.dtype)


def _mm(x, w, rows, tm, tn, out_dtype, name):
    k = x.shape[1]
    n = w.shape[1]
    return pl.pallas_call(
        _mm_body,
        out_shape=jax.ShapeDtypeStruct((rows, n), out_dtype),
        grid=(n // tn, rows // tm),
        in_specs=[
            pl.BlockSpec((tm, k), lambda j, i: (i, 0)),
            pl.BlockSpec((k, tn), lambda j, i: (0, j)),
        ],
        out_specs=pl.BlockSpec((tm, tn), lambda j, i: (i, j)),
        compiler_params=_cp("arbitrary", "arbitrary"),
        name=name,
    )(x, w)


def _log_sigmoid(a):
    return jnp.minimum(a, 0.0) - jnp.log1p(jnp.exp(-jnp.abs(a)))


PREP = 256


def _gla_body(k_ref, v_ref, q_ref, r_ref, z_ref, wgf_ref, bgf_ref, wgb_ref, bgb_ref, on_ref, s0f_ref, s0b_ref,
              o_ref, sf_ref, sb_ref,
              qf_s, kf_s, df_s, qb_s, kb_s, db_s, vb_s, dec_s, o_s, stf_s, stb_s, *, T, emit_o):
    nc = T // CHUNK
    nblk = T // PREP
    cpb = PREP // CHUNK
    scale = DK ** -0.5

    row = lax.broadcasted_iota(jnp.int32, (PREP, PREP), 0)
    col = lax.broadcasted_iota(jnp.int32, (PREP, PREP), 1)
    shift = CHUNK.bit_length() - 1
    same = lax.shift_right_logical(row, shift) == lax.shift_right_logical(col, shift)
    tri_f = jnp.where(same & (row >= col), 1.0, 0.0).astype(BF16)
    tri_b = jnp.where(same & (row <= col), 1.0, 0.0).astype(BF16)
    wf_hi, wf_lo = _split2(wgf_ref[...])
    wb_hi, wb_lo = _split2(wgb_ref[...])

    def chunk_sums(tri, g):
        g1 = g.astype(BF16)
        rem = g - g1.astype(F32)
        g2 = rem.astype(BF16)
        g3 = (rem - g2.astype(F32)).astype(BF16)
        return _dot(tri, g1) + _dot(tri, g2) + _dot(tri, g3)

    def edge_rows(G, e):
        return jnp.concatenate(
            [jnp.broadcast_to(G[c * CHUNK + e:c * CHUNK + e + 1, :], (CHUNK, DK)) for c in range(cpb)], axis=0)

    def prep(blk, carry):
        rows = pl.ds(pl.multiple_of(blk * PREP, PREP), PREP)
        z = z_ref[rows, :128]
        gf = _log_sigmoid(_dot3(z, wf_hi, wf_lo) + bgf_ref[...]) / GATE_NORM
        gb = _log_sigmoid(_dot3(z, wb_hi, wb_lo) + bgb_ref[...]) / GATE_NORM
        Gf = chunk_sums(tri_f, gf)
        Gb = chunk_sums(tri_b, gb)
        k = k_ref[rows, :]
        q = q_ref[rows, :] * scale
        qf_s[rows, :] = (q * jnp.exp(Gf)).astype(BF16)
        kf_s[rows, :] = (k * jnp.exp(-Gf)).astype(BF16)
        df_s[rows, :] = (k * jnp.exp(edge_rows(Gf, CHUNK - 1) - Gf)).astype(BF16)
        qb_s[rows, :] = (q * jnp.exp(Gb)).astype(BF16)
        kb_s[rows, :] = (k * jnp.exp(-Gb)).astype(BF16)
        db_s[rows, :] = (k * jnp.exp(edge_rows(Gb, 0) - Gb)).astype(BF16)
        vb_s[rows, :] = v_ref[rows, :].astype(BF16)
        for c in range(cpb):
            ef = Gf[c * CHUNK + CHUNK - 1:c * CHUNK + CHUNK, :]
            eb = Gb[c * CHUNK:c * CHUNK + 1, :]
            dec_s[0, blk * cpb + c] = jnp.broadcast_to(jnp.exp(ef), (8, DK))
            dec_s[1, blk * cpb + c] = jnp.broadcast_to(jnp.exp(eb), (8, DK))
        return carry

    lax.fori_loop(0, nblk, prep, 0)

    stf_s[...] = s0f_ref[0, 0]
    stb_s[...] = s0b_ref[0, 0]
    r64 = lax.broadcasted_iota(jnp.int32, (CHUNK, CHUNK), 0)
    c64 = lax.broadcasted_iota(jnp.int32, (CHUNK, CHUNK), 1)
    nt_dims = (((1,), (1,)), ((), ()))
    tn_dims = (((0,), (0,)), ((), ()))

    def chunk(c, q_s, k_s, d_s, st_s, di, keep):
        rows = pl.ds(pl.multiple_of(c * CHUNK, CHUNK), CHUNK)
        qe = q_s[rows, :]
        vb = vb_s[rows, :]
        st = st_s[...]
        o = None
        if emit_o:
            a = lax.dot_general(qe, k_s[rows, :], nt_dims, preferred_element_type=F32)
            a = jnp.where(keep, a, 0.0).astype(BF16)
            o = _dot(a, vb) + lax.dot_general(qe, st.astype(BF16), nt_dims, preferred_element_type=F32)
        upd = lax.dot_general(vb, d_s[rows, :], tn_dims, preferred_element_type=F32)
        st_s[...] = st * dec_s[di, c][0:1, :] + upd
        return o, rows

    def step(ci, carry, accumulate):
        of, rows_f = chunk(ci, qf_s, kf_s, df_s, stf_s, 0, r64 >= c64)
        ob, rows_b = chunk(nc - 1 - ci, qb_s, kb_s, db_s, stb_s, 1, r64 <= c64)
        if emit_o:
            if accumulate:
                o_s[rows_f, :] = o_s[rows_f, :] + of
                o_s[rows_b, :] = o_s[rows_b, :] + ob
            else:
                o_s[rows_f, :] = of
                o_s[rows_b, :] = ob
        return carry

    lax.fori_loop(0, nc // 2, functools.partial(step, accumulate=False), 0, unroll=2)
    lax.fori_loop(nc // 2, nc, functools.partial(step, accumulate=True), 0, unroll=2)
    sf_ref[0, 0] = stf_s[...]
    sb_ref[0, 0] = stb_s[...]

    if emit_o:
        def fin(blk, carry):
            rows = pl.ds(pl.multiple_of(blk * PREP, PREP), PREP)
            tot = o_s[rows, :]
            y = tot * lax.rsqrt(jnp.mean(tot * tot, axis=-1, keepdims=True) + EPS) * on_ref[...]
            o_ref[rows, :] = (y * _silu(r_ref[rows, :])).astype(o_ref.dtype)
            return carry

        lax.fori_loop(0, nblk, fin, 0)
    else:
        o_ref[...] = jnp.zeros_like(o_ref)


def _gla(proj, wgf, bgf, wgb, bgb, onorm, s0f, s0b, T, row_off, emit_o):
    rb = row_off // T
    st_spec = pl.BlockSpec((1, 1, DV, DK), lambda b, h: (b, h, 0, 0))
    return pl.pallas_call(
        functools.partial(_gla_body, T=T, emit_o=emit_o),
        out_shape=(
            jax.ShapeDtypeStruct((B * T, VV), BF16),
            jax.ShapeDtypeStruct((B, H, DV, DK), F32),
            jax.ShapeDtypeStruct((B, H, DV, DK), F32),
        ),
        grid=(B, H),
        in_specs=[
            pl.BlockSpec((T, DK), lambda b, h: (rb + b, COL_K // DK + h)),
            pl.BlockSpec((T, DV), lambda b, h: (rb + b, COL_V // DV + h)),
            pl.BlockSpec((T, DK), lambda b, h: (rb + b, COL_Q // DK + h)),
            pl.BlockSpec((T, DV), lambda b, h: (rb + b, COL_R // DV + h)),
            pl.BlockSpec((T, ZPAD), lambda b, h: (rb + b, COL_Z // ZPAD)),
            pl.BlockSpec((128, DK), lambda b, h: (0, h)),
            pl.BlockSpec((1, DK), lambda b, h: (0, h)),
            pl.BlockSpec((128, DK), lambda b, h: (0, h)),
            pl.BlockSpec((1, DK), lambda b, h: (0, h)),
            pl.BlockSpec((1, DV), lambda b, h: (0, 0)),
            st_spec,
            st_spec,
        ],
        out_specs=(
            pl.BlockSpec((T, DV), lambda b, h: (b, h)),
            st_spec,
            st_spec,
        ),
        scratch_shapes=[pltpu.VMEM((T, DK), BF16)] * 6 + [
            pltpu.VMEM((T, DV), BF16),
            pltpu.VMEM((2, T // CHUNK, 8, DK), F32),
            pltpu.VMEM((T, DV), F32),
            pltpu.VMEM((DV, DK), F32),
            pltpu.VMEM((DV, DK), F32),
        ],
        compiler_params=_cp("arbitrary", "arbitrary"),
        name="gla_scan",
    )(proj, proj, proj, proj, proj, wgf, bgf, wgb, bgb, onorm, s0f, s0b)


def _fnet_in_body(h_ref, w_ref, cs_ref, a_ref, b_ref):
    u = _dot(h_ref[...], w_ref[...]).astype(BF16)
    for g in range(FG):
        ab = _dot(u[:, g * FD:(g + 1) * FD], cs_ref[...])
        a_ref[:, g * FD:(g + 1) * FD] = ab[:, :FD].astype(BF16)
        b_ref[:, g * FD:(g + 1) * FD] = ab[:, FD:].astype(BF16)


def _fnet_in(h, w, cs, rows):
    tm = 512
    return pl.pallas_call(
        _fnet_in_body,
        out_shape=(jax.ShapeDtypeStruct((rows, D), BF16), jax.ShapeDtypeStruct((rows, D), BF16)),
        grid=(rows // tm,),
        in_specs=[
            pl.BlockSpec((tm, D), lambda i: (i, 0)),
            pl.BlockSpec((D, D), lambda i: (0, 0)),
            pl.BlockSpec((FD, 2 * FD), lambda i: (0, 0)),
        ],
        out_specs=(pl.BlockSpec((tm, D), lambda i: (i, 0)), pl.BlockSpec((tm, D), lambda i: (i, 0))),
        compiler_params=_cp("arbitrary"),
        name="fnet_in",
    )(h, w, cs)


def _fnet_time_body(ct_ref, st_ref, a_ref, b_ref, o_ref):
    o_ref[...] = (_dot(ct_ref[...], a_ref[...]) + _dot(st_ref[...], b_ref[...])).astype(o_ref.dtype)


def _fnet_time(ct, st, a, b, T, row_off):
    tm = min(T, 512)
    tn = 1024
    rb = row_off // T
    return pl.pallas_call(
        _fnet_time_body,
        out_shape=jax.ShapeDtypeStruct((B * T, D), BF16),
        grid=(B, D // tn, T // tm),
        in_specs=[
            pl.BlockSpec((tm, T), lambda s, j, i: (i, 0)),
            pl.BlockSpec((tm, T), lambda s, j, i: (i, 0)),
            pl.BlockSpec((T, tn), lambda s, j, i: (rb + s, j)),
            pl.BlockSpec((T, tn), lambda s, j, i: (rb + s, j)),
        ],
        out_specs=pl.BlockSpec((tm, tn), lambda s, j, i: (s * (T // tm) + i, j)),
        compiler_params=_cp("arbitrary", "arbitrary", "arbitrary"),
        name="fnet_time",
    )(ct, st, a, b)


def _dft_mats(n, scale):
    idx = np.arange(n, dtype=np.int64)
    ang = 2.0 * np.pi * ((idx[:, None] * idx[None, :]) % n).astype(np.float64) / n
    return np.cos(ang) * scale, np.sin(ang) * scale


def _row_copy(src_hbm, row, dst_vmem, r, sem):
    return pltpu.make_async_copy(src_hbm.at[pl.ds(row, 1)], dst_vmem.at[pl.ds(r, 1)], sem)


def _moe_body(bs_ref, tok_ref, h_hbm, wg_ref, wu_ref, wd_ref, y_hbm, xbuf, obuf, gsem, osem, wgb, wub, wdb, *, nb):
    e = pl.program_id(0)
    b0 = bs_ref[e]
    b1 = bs_ref[e + 1]
    total = bs_ref[ME]

    def gather(g):
        slot = g % 2
        base = g * TME
        for r in range(TME):
            _row_copy(h_hbm, tok_ref[base + r], xbuf.at[slot], r, gsem.at[slot]).start(priority=1)

    def out_copy(g):
        slot = g % 2
        dst = y_hbm.at[pl.ds(pl.multiple_of(g * TME, TME), TME)]
        return pltpu.make_async_copy(obuf.at[slot], dst, osem.at[slot])

    @pl.when((e == 0) & (total > 0))
    def _():
        gather(0)

    @pl.when(b1 > b0)
    def _():
        wgb[...] = wg_ref[...].astype(BF16)
        wub[...] = wu_ref[...].astype(BF16)
        wdb[...] = wd_ref[...].astype(BF16)

    def block(g, carry):
        slot = g % 2

        @pl.when(g + 1 < total)
        def _():
            gather(g + 1)

        pltpu.make_async_copy(h_hbm.at[pl.ds(0, TME)], xbuf.at[slot], gsem.at[slot]).wait()
        x_lo, x_hi = _unpack_rows(xbuf[slot])
        x = jnp.concatenate([x_lo.astype(BF16), x_hi.astype(BF16)], axis=1)
        hmid = _silu(_dot(x, wgb[...])) * _dot(x, wub[...])
        y = _pack_rows(_dot(hmid.astype(BF16), wdb[...]))

        @pl.when(g >= 2)
        def _():
            out_copy(g - 2).wait()

        obuf[slot] = y
        out_copy(g).start()
        return carry

    lax.fori_loop(b0, b1, block, 0)

    @pl.when(e == ME - 1)
    def _():
        @pl.when(total >= 2)
        def _():
            out_copy(total - 2).wait()

        @pl.when(total >= 1)
        def _():
            out_copy(total - 1).wait()

        def zero_copy(g):
            dst = y_hbm.at[pl.ds(pl.multiple_of(g * TME, TME), TME)]
            return pltpu.make_async_copy(obuf.at[0], dst, osem.at[0])

        obuf[0] = jnp.zeros((TME, HALF), jnp.uint32)
        lax.fori_loop(total, nb, lambda g, c: (zero_copy(g).start(), c)[1], 0)
        lax.fori_loop(total, nb, lambda g, c: (zero_copy(g).wait(), c)[1], 0)


def _moe_experts(layer, blk_start, buf_tok, h2, w_gate, w_up, w_down):
    nb = buf_tok.shape[0] // TME
    w_in_spec = pl.BlockSpec((None, None, D, MDE), lambda e, bs, tk: (layer, e, 0, 0))
    return pl.pallas_call(
        functools.partial(_moe_body, nb=nb),
        out_shape=jax.ShapeDtypeStruct((nb * TME, HALF), jnp.uint32),
        grid_spec=pltpu.PrefetchScalarGridSpec(
            num_scalar_prefetch=2,
            grid=(ME,),
            in_specs=[
                pl.BlockSpec(memory_space=pl.ANY),
                w_in_spec,
                w_in_spec,
                pl.BlockSpec((None, None, MDE, D), lambda e, bs, tk: (layer, e, 0, 0)),
            ],
            out_specs=pl.BlockSpec(memory_space=pl.ANY),
            scratch_shapes=[
                pltpu.VMEM((2, TME, HALF), jnp.uint32),
                pltpu.VMEM((2, TME, HALF), jnp.uint32),
                pltpu.SemaphoreType.DMA((2,)),
                pltpu.SemaphoreType.DMA((2,)),
                pltpu.VMEM((D, MDE), BF16),
                pltpu.VMEM((D, MDE), BF16),
                pltpu.VMEM((MDE, D), BF16),
            ],
        ),
        compiler_params=_cp("arbitrary"),
        name="moe_experts",
    )(blk_start, buf_tok, h2, w_gate, w_up, w_down)


TMC = 128


def _combine_body(dest_ref, x_ref, gate_ref, wt_ref, g_ref, sh_ref, sc_ref, y_hbm, *rest, nt, last):
    if last:
        h_ref, ybuf, sem = rest
    else:
        o_ref, h_ref, ybuf, sem = rest
    i = pl.program_id(0)

    def gather(tile, slot):
        base = tile * (2 * TMC)

        for r in range(TMC):
            _row_copy(y_hbm, dest_ref[base + 2 * r], ybuf.at[slot, 0], r, sem.at[slot]).start(priority=0)
            _row_copy(y_hbm, dest_ref[base + 2 * r + 1], ybuf.at[slot, 1], r, sem.at[slot]).start(priority=1)

    @pl.when(i == 0)
    def _():
        gather(0, 0)

    @pl.when(i + 1 < nt)
    def _():
        gather(i + 1, (i + 1) % 2)

    slot = i % 2
    for k in range(2):
        pltpu.make_async_copy(y_hbm.at[pl.ds(0, TMC)], ybuf.at[slot, k], sem.at[slot]).wait()
    wt = wt_ref[...]
    w0, w1 = wt[:, 0:1], wt[:, 1:2]
    lo0, hi0 = _unpack_rows(ybuf[slot, 0])
    lo1, hi1 = _unpack_rows(ybuf[slot, 1])
    y = jnp.concatenate([w0 * lo0 + w1 * lo1, w0 * hi0 + w1 * hi1], axis=1)
    xn = x_ref[...] + gate_ref[...] * y
    if last:
        h_ref[...] = xn * lax.rsqrt(jnp.mean(xn * xn, axis=-1, keepdims=True) + EPS) * g_ref[...]
    else:
        o_ref[...] = xn
        h_ref[...] = _norm_mod_val(xn, g_ref[...], sh_ref[...], sc_ref[...]).astype(h_ref.dtype)


def _combine(dest, xs, mod3, wt, yb, rows, g_next, mod3_next, last):
    nt = rows // TMC
    row_spec = pl.BlockSpec((TMC, D), lambda i, d: (i, 0))

    def mod_spec(which):
        return pl.BlockSpec((None, 1, D), lambda i, d: (_seg_of(i, TMC) * N_MOD + which, 0, 0))

    if last:
        out_shape = jax.ShapeDtypeStruct((rows, D), F32)
        out_specs = row_spec
    else:
        out_shape = (jax.ShapeDtypeStruct((rows, D), F32), jax.ShapeDtypeStruct((rows, D), BF16))
        out_specs = (row_spec, row_spec)
    return pl.pallas_call(
        functools.partial(_combine_body, nt=nt, last=last),
        out_shape=out_shape,
        grid_spec=pltpu.PrefetchScalarGridSpec(
            num_scalar_prefetch=1,
            grid=(nt,),
            in_specs=[
                row_spec,
                mod_spec(5),
                pl.BlockSpec((TMC, 128), lambda i, d: (i, 0)),
                pl.BlockSpec((1, D), lambda i, d: (0, 0)),
                mod_spec(0),
                mod_spec(1),
                pl.BlockSpec(memory_space=pl.ANY),
            ],
            out_specs=out_specs,
            scratch_shapes=[
                pltpu.VMEM((2, 2, TMC, HALF), jnp.uint32),
                pltpu.SemaphoreType.DMA((2,)),
            ],
        ),
        compiler_params=_cp("arbitrary"),
        name="moe_combine",
    )(dest, xs, mod3, wt, g_next.reshape(1, D), mod3_next, mod3_next, yb)


PLAN_CHUNK = 1024


def _dest_body(ids_ref, base_ref, o_ref):
    ids = ids_ref[...].astype(F32)
    lane_i = lax.broadcasted_iota(jnp.int32, ids.shape, 1)
    lane = lane_i.astype(F32)

    def pick(k):
        return jnp.sum(jnp.where(lane_i == k, ids, 0.0), axis=-1, keepdims=True)

    base = base_ref[...]
    d0 = jnp.sum(jnp.where(lane == pick(0), base[0:1, :], 0.0), axis=-1, keepdims=True) + pick(2)
    d1 = jnp.sum(jnp.where(lane == pick(1), base[1:2, :], 0.0), axis=-1, keepdims=True) + pick(3)
    o_ref[...] = jnp.where(lane_i == 0, d0, jnp.where(lane_i == 1, d1, 0.0)).astype(jnp.int32)


def _slot_body(dest_ref, tok_ref, *, nslots):
    i = pl.program_id(0)

    @pl.when(i == 0)
    def _():
        def clear(s, carry):
            tok_ref[s] = 0
            return carry

        lax.fori_loop(0, nslots, clear, 0, unroll=32)

    def place(t, carry):
        tok = i * PLAN_CHUNK + t
        tok_ref[dest_ref[0, 2 * t]] = tok
        tok_ref[dest_ref[0, 2 * t + 1]] = tok
        return carry

    lax.fori_loop(0, PLAN_CHUNK, place, 0, unroll=16)


def _route_plan(eid, cnt, rows):
    a = rows * 2
    nb = -(-a // TME) + ME
    steps = rows // PLAN_CHUNK
    c0 = cnt[0, :ME].astype(jnp.int32)
    c1 = cnt[1, :ME].astype(jnp.int32)
    padded = ((c0 + c1 + TME - 1) // TME) * TME
    pad_end = jnp.cumsum(padded)
    pad_start = pad_end - padded
    blk_start = jnp.concatenate([jnp.zeros((1,), jnp.int32), (pad_end // TME).astype(jnp.int32)])
    base = jnp.zeros((8, 128), F32).at[0, :ME].set(pad_start.astype(F32)).at[1, :ME].set((pad_start + c0).astype(F32))
    dest_t = pl.pallas_call(
        _dest_body,
        out_shape=jax.ShapeDtypeStruct((rows, 128), jnp.int32),
        grid=(steps,),
        in_specs=[pl.BlockSpec((PLAN_CHUNK, 128), lambda i: (i, 0)), pl.BlockSpec((8, 128), lambda i: (0, 0))],
        out_specs=pl.BlockSpec((PLAN_CHUNK, 128), lambda i: (i, 0)),
        compiler_params=_cp("arbitrary"),
        name="route_dest",
    )(eid, base)
    dest = dest_t[:, :2].reshape(a)
    buf_tok = pl.pallas_call(
        functools.partial(_slot_body, nslots=nb * TME),
        out_shape=jax.ShapeDtypeStruct((nb * TME,), jnp.int32),
        grid=(steps,),
        in_specs=[pl.BlockSpec((None, 1, 2 * PLAN_CHUNK), lambda i: (i, 0, 0), memory_space=pltpu.SMEM)],
        out_specs=pl.BlockSpec(memory_space=pltpu.SMEM),
        compiler_params=_cp("arbitrary"),
        name="route_slots",
    )(dest.reshape(steps, 1, 2 * PLAN_CHUNK))
    return dest, buf_tok, blk_start


def _sincos_2d(rows, cols, d):
    quarter = d // 4
    omega = 1.0 / (POS_BASE ** (jnp.arange(quarter, dtype=F32) / quarter))

    def axis_emb(n):
        p = jnp.arange(n, dtype=F32)[:, None] * omega[None, :]
        return jnp.concatenate([jnp.sin(p), jnp.cos(p)], axis=-1)

    er, ec = axis_emb(rows), axis_emb(cols)
    half = 2 * quarter
    pos = jnp.concatenate([jnp.broadcast_to(er[:, None, :], (rows, cols, half)),
                           jnp.broadcast_to(ec[None, :, :], (rows, cols, half))], axis=-1)
    return pos.reshape(rows * cols, 2 * half)


def _gla_weights(w_in, wg_f, wg_b):
    z = jnp.zeros((D, ZPAD - 2 * RANK), F32)
    w_cat = jnp.concatenate([
        w_in[:, :QK], w_in[:, QK:QK + VV],
        w_in[:, STATE_COLS:STATE_COLS + QK], w_in[:, STATE_COLS + QK:],
        w_in[:, QK + VV:STATE_COLS], z], axis=1).astype(BF16)
    wgf = jnp.zeros((128, QK), F32).at[:RANK].set(wg_f)
    wgb = jnp.zeros((128, QK), F32).at[RANK:2 * RANK].set(wg_b)
    return w_cat, wgf, wgb


def kernel(x, c, ctx, c_ctx, ada_w, ada_b, norm1_g, norm2_g, gla_w_in, gla_wg_f, gla_bg_f, gla_wg_b, gla_bg_b,
           gla_onorm_g, gla_w_out, fnet_w_in, fnet_w_out, moe_rw_group, moe_rb_group, moe_rw_expert,
           moe_rb_expert, moe_w_gate, moe_w_up, moe_w_down, final_g):
    cvec = jnp.zeros((SEGS, D), F32).at[:B].set(c).at[CTX_SEG].set(c_ctx)
    mods = _mod_all(cvec, ada_w, ada_b)
    pos = _sincos_2d(SEQ // GRID_W, GRID_W, D)
    mod3s = [mods[i].reshape(SEGS * N_MOD, 1, D) for i in range(DEPTH)]
    xs, h = _assemble(x.reshape(LAT, D), pos, ctx.reshape(NCTX, D), norm1_g[0], mod3s[0])

    cc, sc = _dft_mats(FD, 1.0)
    cs = jnp.asarray(np.concatenate([cc, sc], axis=1), BF16)
    ct_l, st_l = _dft_mats(SEQ, (SEQ * FD) ** -0.5)
    ct_c, st_c = _dft_mats(CTX, (CTX * FD) ** -0.5)
    ct_l, st_l = jnp.asarray(ct_l, BF16), jnp.asarray(-st_l, BF16)
    ct_c, st_c = jnp.asarray(ct_c, BF16), jnp.asarray(-st_c, BF16)
    s_zero = jnp.zeros((B, H, DV, DK), F32)
    last_reader = ((DEPTH - 1) // 2) * 2

    for i in range(DEPTH):
        kind, j = i % 2, i // 2
        ctx_live = i < last_reader
        ctx_needed = i <= last_reader
        mod3 = mod3s[i]
        rows_in = ROWS if ctx_needed else LAT
        rows_out = ROWS if ctx_live else LAT

        if kind == 0:
            w_cat, wgf, wgb = _gla_weights(gla_w_in[j], gla_wg_f[j], gla_wg_b[j])
            bgf, bgb = gla_bg_f[j].reshape(1, QK), gla_bg_b[j].reshape(1, QK)
            onorm = gla_onorm_g[j].reshape(1, DV)
            proj = _mm(h, w_cat, rows_in, 512, 1280, F32, "gla_proj")
            o_c, sf, sb = _gla(proj, wgf, bgf, wgb, bgb, onorm, s_zero, s_zero, CTX, LAT, ctx_live)
            o_l, _, _ = _gla(proj, wgf, bgf, wgb, bgb, onorm, sf, sb, SEQ, 0, True)
            mix_l, mix_c, w_out = o_l, o_c, gla_w_out[j].astype(BF16)
        else:
            a, b = _fnet_in(h, fnet_w_in[j].astype(BF16), cs, rows_in)
            y = _fnet_time(ct_l, st_l, a, b, SEQ, 0)
            y_c = _fnet_time(ct_c, st_c, a, b, CTX, LAT) if ctx_live else y
            mix_l, mix_c, w_out = y, y_c, fnet_w_out[j].astype(BF16)

        rw = jnp.zeros((D, 128), F32).at[:, :MG].set(moe_rw_group[i])
        rw = rw.at[:, MG:MG + ME].set(jnp.transpose(moe_rw_expert[i], (1, 0, 2)).reshape(D, ME))
        rb = jnp.zeros((1, 128), F32).at[0, :MG].set(moe_rb_group[i]).at[0, MG:MG + ME].set(
            moe_rb_expert[i].reshape(ME))
        rw_hi = rw.astype(BF16)
        rw_lo = (rw - rw_hi.astype(F32)).astype(BF16)
        xs, h2, eid, wt, cnt = _mix_route(mix_l, mix_c, w_out, xs, norm2_g[i], mod3, rw_hi, rw_lo, rb, rows_out)
        dest, buf_tok, blk_start = _route_plan(eid, cnt, rows_out)
        yb = _moe_experts(i, blk_start, buf_tok, h2, moe_w_gate, moe_w_up, moe_w_down)
        if i + 1 < DEPTH:
            xs, h = _combine(dest, xs, mod3, wt, yb, rows_out, norm1_g[i + 1], mod3s[i + 1], False)
        else:
            out = _combine(dest, xs, mod3, wt, yb, rows_out, final_g, mod3, True)

    return out.reshape(B, SEQ, D)
```

```python
import functools

import jax
import jax.numpy as jnp
import numpy as np
from jax import lax
from jax.experimental import pallas as pl
from jax.experimental.pallas import tpu as pltpu

F32 = jnp.float32
BF16 = jnp.bfloat16

D = 2048
B = 4
SEQ = 2048
CTX = 256
DEPTH = 4
GRID_W = 64
EPS = 1e-6
POS_BASE = 10000.0
N_MOD = 6

H = 4
DK = 256
DV = 512
QK = H * DK
VV = H * DV
RANK = 16
GATE_NORM = 16.0
CHUNK = 64
STATE_COLS = QK + VV + 2 * RANK
ZPAD = 256
PROJ_N = QK + VV + QK + VV
COL_K, COL_V, COL_Q, COL_R = 0, QK, QK + VV, 2 * QK + VV

FG = 4
FD = D // FG

MG = 4
MPG = 8
ME = MG * MPG
MDE = D // 4
TME = 256

LAT = B * SEQ
NCTX = B * CTX
ROWS = LAT + NCTX
SEGS = 8
CTX_SEG = B

VMEM_LIMIT = 56 * 1024 * 1024


def _cp(*sem):
    return pltpu.CompilerParams(dimension_semantics=sem, vmem_limit_bytes=VMEM_LIMIT)


def _seg_of(i, tm):
    return jnp.where(i < LAT // tm, i // (SEQ // tm), CTX_SEG)


def _split2(a):
    hi = a.astype(BF16)
    lo = (a - hi.astype(F32)).astype(BF16)
    return hi, lo


def _dot(a, b):
    return jnp.dot(a, b, preferred_element_type=F32)


def _dot3(a, b_hi, b_lo):
    a_hi, a_lo = _split2(a)
    return _dot(a_hi, b_hi) + _dot(a_lo, b_hi) + _dot(a_hi, b_lo)


def _silu(a):
    return a * jax.nn.sigmoid(a)


HALF = D // 2


def _pack_rows(a):
    lo = lax.bitcast_convert_type(a[:, :HALF].astype(BF16).astype(F32), jnp.uint32)
    hi = lax.bitcast_convert_type(a[:, HALF:].astype(BF16).astype(F32), jnp.uint32)
    return (hi & jnp.uint32(0xFFFF0000)) | (lo >> 16)


def _unpack_rows(w):
    lo = lax.bitcast_convert_type(w << 16, F32)
    hi = lax.bitcast_convert_type(w & jnp.uint32(0xFFFF0000), F32)
    return lo, hi


def _mod_body(c_ref, w_ref, b_ref, o_ref):
    s = _silu(c_ref[...])
    w_hi, w_lo = _split2(w_ref[0])
    o_ref[0] = _dot3(s, w_hi, w_lo) + b_ref[0]


def _mod_all(cvec, ada_w, ada_b):
    tn = 512
    return pl.pallas_call(
        _mod_body,
        out_shape=jax.ShapeDtypeStruct((DEPTH, SEGS, N_MOD * D), F32),
        grid=(DEPTH, N_MOD * D // tn),
        in_specs=[
            pl.BlockSpec((SEGS, D), lambda l, j: (0, 0)),
            pl.BlockSpec((1, D, tn), lambda l, j: (l, 0, j)),
            pl.BlockSpec((1, 1, tn), lambda l, j: (l, 0, j)),
        ],
        out_specs=pl.BlockSpec((1, SEGS, tn), lambda l, j: (l, 0, j)),
        compiler_params=_cp("arbitrary", "arbitrary"),
        name="adaln_mod",
    )(cvec, ada_w, ada_b.reshape(DEPTH, 1, N_MOD * D))


def _assemble_body(x_ref, pos_ref, ctx_ref, g_ref, sh_ref, sc_ref, o_ref, h_ref, *, nlat):
    i = pl.program_id(0)

    @pl.when(i < nlat)
    def _():
        o_ref[...] = x_ref[...] + pos_ref[...]

    @pl.when(i >= nlat)
    def _():
        o_ref[...] = ctx_ref[...]

    h_ref[...] = _norm_mod_val(o_ref[...], g_ref[...], sh_ref[...], sc_ref[...]).astype(h_ref.dtype)


def _assemble(x2, pos, ctx2, g, mod3):
    tm = 256
    nlat = LAT // tm
    return pl.pallas_call(
        functools.partial(_assemble_body, nlat=nlat),
        out_shape=(jax.ShapeDtypeStruct((ROWS, D), F32), jax.ShapeDtypeStruct((ROWS, D), BF16)),
        grid=(ROWS // tm,),
        in_specs=[
            pl.BlockSpec((tm, D), lambda i: (jnp.minimum(i, nlat - 1), 0)),
            pl.BlockSpec((tm, D), lambda i: (i % (SEQ // tm), 0)),
            pl.BlockSpec((tm, D), lambda i: (jnp.maximum(i - nlat, 0), 0)),
            pl.BlockSpec((1, D), lambda i: (0, 0)),
            _mod_spec(0, tm),
            _mod_spec(1, tm),
        ],
        out_specs=(pl.BlockSpec((tm, D), lambda i: (i, 0)), pl.BlockSpec((tm, D), lambda i: (i, 0))),
        compiler_params=_cp("arbitrary"),
        name="assemble_stream",
    )(x2, pos, ctx2, g.reshape(1, D), mod3, mod3)


def _norm_mod_val(x, g, sh, sc):
    y = x * lax.rsqrt(jnp.mean(x * x, axis=-1, keepdims=True) + EPS) * g
    return y * (1.0 + sc) + sh


def _mod_spec(which, tm):
    return pl.BlockSpec((None, 1, D), lambda i: (_seg_of(i, tm) * N_MOD + which, 0, 0))


def _route_tile(h2, rwh_ref, rwl_ref, rb_ref, run_s):
    lg = _dot3(h2, rwh_ref[...], rwl_ref[...]) + rb_ref[...]
    lane_i = lax.broadcasted_iota(jnp.int32, lg.shape, 1)
    lane = lane_i.astype(F32)
    neg = jnp.float32(-jnp.inf)
    big = jnp.float32(1024.0)

    glog = jnp.where(lane < MG, lg, neg)
    gmax = jnp.max(glog, axis=-1, keepdims=True)
    gidx = jnp.min(jnp.where(glog == gmax, lane, big), axis=-1, keepdims=True)
    g_w = 1.0 / jnp.sum(jnp.exp(glog - gmax), axis=-1, keepdims=True)

    lo = MG + MPG * gidx
    el = jnp.where((lane >= lo) & (lane < lo + MPG), lg, neg)
    m1 = jnp.max(el, axis=-1, keepdims=True)
    i1 = jnp.min(jnp.where(el == m1, lane, big), axis=-1, keepdims=True)
    el2 = jnp.where(lane == i1, neg, el)
    m2 = jnp.max(el2, axis=-1, keepdims=True)
    i2 = jnp.min(jnp.where(el2 == m2, lane, big), axis=-1, keepdims=True)
    e2 = jnp.exp(m2 - m1)
    den = 1.0 / (1.0 + e2)
    w1 = g_w * den
    w2 = g_w * (e2 * den)
    e_a = i1 - MG
    e_b = i2 - MG

    tm = lg.shape[0]
    r_i = lax.broadcasted_iota(jnp.int32, (tm, tm), 0)
    c_i = lax.broadcasted_iota(jnp.int32, (tm, tm), 1)
    earlier = jnp.where(r_i > c_i, 1.0, 0.0).astype(BF16)
    oh_a = jnp.where(lane == e_a, 1.0, 0.0)
    oh_b = jnp.where(lane == e_b, 1.0, 0.0)
    run = run_s[...]
    rank_a = jnp.sum(oh_a * (_dot(earlier, oh_a.astype(BF16)) + run[0:1, :]), axis=-1, keepdims=True)
    rank_b = jnp.sum(oh_b * (_dot(earlier, oh_b.astype(BF16)) + run[1:2, :]), axis=-1, keepdims=True)
    run_s[0:1, :] = run[0:1, :] + jnp.sum(oh_a, axis=0, keepdims=True)
    run_s[1:2, :] = run[1:2, :] + jnp.sum(oh_b, axis=0, keepdims=True)

    ids = jnp.where(lane_i == 0, e_a, jnp.where(lane_i == 1, e_b, jnp.where(lane_i == 2, rank_a,
                                                                           jnp.where(lane_i == 3, rank_b, 0.0))))
    wts = jnp.where(lane_i == 0, w1, jnp.where(lane_i == 1, w2, 0.0))
    return ids.astype(jnp.int32), wts


def _mix_route_body(al_ref, ac_ref, w_ref, res_ref, gate_ref, g_ref, sh_ref, sc_ref, rwh_ref, rwl_ref, rb_ref,
                    o_ref, h_ref, eid_ref, wt_ref, cnt_ref, acc_s, run_s, *, nlat):
    i = pl.program_id(0)

    @pl.when(i == 0)
    def _():
        run_s[...] = jnp.zeros_like(run_s)

    @pl.when(i < nlat)
    def _():
        acc_s[...] = _dot(al_ref[...], w_ref[...])

    @pl.when(i >= nlat)
    def _():
        acc_s[...] = _dot(ac_ref[...], w_ref[...])

    xn = res_ref[...] + gate_ref[...] * acc_s[...]
    o_ref[...] = xn
    h2 = _norm_mod_val(xn, g_ref[...], sh_ref[...], sc_ref[...])
    h_ref[...] = _pack_rows(h2)
    ids, wts = _route_tile(h2, rwh_ref, rwl_ref, rb_ref, run_s)
    eid_ref[...] = ids
    wt_ref[...] = wts
    cnt_ref[...] = run_s[...]


def _mix_route(a_lat, a_ctx, w, xs, g, mod3, rw_hi, rw_lo, rb, rows):
    tm = 256
    k = a_lat.shape[1]
    nlat = LAT // tm
    return pl.pallas_call(
        functools.partial(_mix_route_body, nlat=nlat),
        out_shape=(
            jax.ShapeDtypeStruct((rows, D), F32),
            jax.ShapeDtypeStruct((rows, HALF), jnp.uint32),
            jax.ShapeDtypeStruct((rows, 128), jnp.int32),
            jax.ShapeDtypeStruct((rows, 128), F32),
            jax.ShapeDtypeStruct((8, 128), F32),
        ),
        grid=(rows // tm,),
        in_specs=[
            pl.BlockSpec((tm, k), lambda i: (jnp.minimum(i, nlat - 1), 0)),
            pl.BlockSpec((tm, k), lambda i: (jnp.maximum(i - nlat, 0), 0)),
            pl.BlockSpec((k, D), lambda i: (0, 0)),
            pl.BlockSpec((tm, D), lambda i: (i, 0)),
            _mod_spec(2, tm),
            pl.BlockSpec((1, D), lambda i: (0, 0)),
            _mod_spec(3, tm),
            _mod_spec(4, tm),
            pl.BlockSpec((D, 128), lambda i: (0, 0)),
            pl.BlockSpec((D, 128), lambda i: (0, 0)),
            pl.BlockSpec((1, 128), lambda i: (0, 0)),
        ],
        out_specs=(
            pl.BlockSpec((tm, D), lambda i: (i, 0)),
            pl.BlockSpec((tm, HALF), lambda i: (i, 0)),
            pl.BlockSpec((tm, 128), lambda i: (i, 0)),
            pl.BlockSpec((tm, 128), lambda i: (i, 0)),
            pl.BlockSpec((8, 128), lambda i: (0, 0)),
        ),
        scratch_shapes=[pltpu.VMEM((tm, D), F32), pltpu.VMEM((8, 128), F32)],
        compiler_params=_cp("arbitrary"),
        name="mix_route",
    )(a_lat, a_ctx, w, xs, mod3, g.reshape(1, D), mod3, mod3, rw_hi, rw_lo, rb)


def _mm_body(x_ref, w_ref, o_ref):
    o_ref[...] = _dot(x_ref[...], w_ref[...]).astype(o_ref.dtype)


def _mm(x, w, rows, tm, tn, out_dtype, name):
    k = x.shape[1]
    n = w.shape[1]
    return pl.pallas_call(
        _mm_body,
        out_shape=jax.ShapeDtypeStruct((rows, n), out_dtype),
        grid=(n // tn, rows // tm),
        in_specs=[
            pl.BlockSpec((tm, k), lambda j, i: (i, 0)),
            pl.BlockSpec((k, tn), lambda j, i: (0, j)),
        ],
        out_specs=pl.BlockSpec((tm, tn), lambda j, i: (i, j)),
        compiler_params=_cp("arbitrary", "arbitrary"),
        name=name,
    )(x, w)


def _log_sigmoid(a):
    return jnp.minimum(a, 0.0) - jnp.log1p(jnp.exp(-jnp.abs(a)))


PREP = 256


def _gla_body(k_ref, v_ref, q_ref, r_ref, z_ref, wgf_ref, bgf_ref, wgb_ref, bgb_ref, on_ref, s0f_ref, s0b_ref,
              o_ref, sf_ref, sb_ref,
              qf_s, kf_s, df_s, qb_s, kb_s, db_s, vb_s, dec_s, o_s, stf_s, stb_s, *, T, emit_o):
    nc = T // CHUNK
    nblk = T // PREP
    cpb = PREP // CHUNK
    scale = DK ** -0.5

    row = lax.broadcasted_iota(jnp.int32, (PREP, PREP), 0)
    col = lax.broadcasted_iota(jnp.int32, (PREP, PREP), 1)
    shift = CHUNK.bit_length() - 1
    same = lax.shift_right_logical(row, shift) == lax.shift_right_logical(col, shift)
    tri_f = jnp.where(same & (row >= col), 1.0, 0.0).astype(BF16)
    tri_b = jnp.where(same & (row <= col), 1.0, 0.0).astype(BF16)
    wf_hi, wf_lo = _split2(wgf_ref[...])
    wb_hi, wb_lo = _split2(wgb_ref[...])

    def chunk_sums(tri, g):
        g1 = g.astype(BF16)
        rem = g - g1.astype(F32)
        g2 = rem.astype(BF16)
        g3 = (rem - g2.astype(F32)).astype(BF16)
        return _dot(tri, g1) + _dot(tri, g2) + _dot(tri, g3)

    def edge_rows(G, e):
        return jnp.concatenate(
            [jnp.broadcast_to(G[c * CHUNK + e:c * CHUNK + e + 1, :], (CHUNK, DK)) for c in range(cpb)], axis=0)

    def prep(blk, carry):
        rows = pl.ds(pl.multiple_of(blk * PREP, PREP), PREP)
        z = z_ref[rows, :128]
        gf = _log_sigmoid(_dot3(z, wf_hi, wf_lo) + bgf_ref[...]) / GATE_NORM
        gb = _log_sigmoid(_dot3(z, wb_hi, wb_lo) + bgb_ref[...]) / GATE_NORM
        Gf = chunk_sums(tri_f, gf)
        Gb = chunk_sums(tri_b, gb)
        k = k_ref[rows, :].astype(F32)
        q = q_ref[rows, :].astype(F32) * scale
        qf_s[rows, :] = (q * jnp.exp(Gf)).astype(BF16)
        kf_s[rows, :] = (k * jnp.exp(-Gf)).astype(BF16)
        df_s[rows, :] = (k * jnp.exp(edge_rows(Gf, CHUNK - 1) - Gf)).astype(BF16)
        qb_s[rows, :] = (q * jnp.exp(Gb)).astype(BF16)
        kb_s[rows, :] = (k * jnp.exp(-Gb)).astype(BF16)
        db_s[rows, :] = (k * jnp.exp(edge_rows(Gb, 0) - Gb)).astype(BF16)
        vb_s[rows, :] = v_ref[rows, :]
        for c in range(cpb):
            ef = Gf[c * CHUNK + CHUNK - 1:c * CHUNK + CHUNK, :]
            eb = Gb[c * CHUNK:c * CHUNK + 1, :]
            dec_s[0, blk * cpb + c] = jnp.broadcast_to(jnp.exp(ef), (8, DK))
            dec_s[1, blk * cpb + c] = jnp.broadcast_to(jnp.exp(eb), (8, DK))
        return carry

    lax.fori_loop(0, nblk, prep, 0)

    stf_s[...] = s0f_ref[0, 0]
    stb_s[...] = s0b_ref[0, 0]
    r64 = lax.broadcasted_iota(jnp.int32, (CHUNK, CHUNK), 0)
    c64 = lax.broadcasted_iota(jnp.int32, (CHUNK, CHUNK), 1)
    nt_dims = (((1,), (1,)), ((), ()))
    tn_dims = (((0,), (0,)), ((), ()))

    def chunk(c, q_s, k_s, d_s, st_s, di, keep):
        rows = pl.ds(pl.multiple_of(c * CHUNK, CHUNK), CHUNK)
        qe = q_s[rows, :]
        vb = vb_s[rows, :]
        st = st_s[...]
        o = None
        if emit_o:
            a = lax.dot_general(qe, k_s[rows, :], nt_dims, preferred_element_type=F32)
            a = jnp.where(keep, a, 0.0).astype(BF16)
            o = _dot(a, vb) + lax.dot_general(qe, st.astype(BF16), nt_dims, preferred_element_type=F32)
        upd = lax.dot_general(vb, d_s[rows, :], tn_dims, preferred_element_type=F32)
        st_s[...] = st * dec_s[di, c][0:1, :] + upd
        return o, rows

    def step(ci, carry, accumulate):
        of, rows_f = chunk(ci, qf_s, kf_s, df_s, stf_s, 0, r64 >= c64)
        ob, rows_b = chunk(nc - 1 - ci, qb_s, kb_s, db_s, stb_s, 1, r64 <= c64)
        if emit_o:
            if accumulate:
                o_s[rows_f, :] = o_s[rows_f, :] + of
                o_s[rows_b, :] = o_s[rows_b, :] + ob
            else:
                o_s[rows_f, :] = of
                o_s[rows_b, :] = ob
        return carry

    lax.fori_loop(0, nc // 2, functools.partial(step, accumulate=False), 0, unroll=min(4, nc // 2))
    lax.fori_loop(nc // 2, nc, functools.partial(step, accumulate=True), 0, unroll=min(4, nc // 2))
    sf_ref[0, 0] = stf_s[...]
    sb_ref[0, 0] = stb_s[...]

    if emit_o:
        def fin(blk, carry):
            rows = pl.ds(pl.multiple_of(blk * PREP, PREP), PREP)
            tot = o_s[rows, :]
            y = tot * lax.rsqrt(jnp.mean(tot * tot, axis=-1, keepdims=True) + EPS) * on_ref[...]
            o_ref[rows, :] = (y * _silu(r_ref[rows, :].astype(F32))).astype(o_ref.dtype)
            return carry

        lax.fori_loop(0, nblk, fin, 0)
    else:
        o_ref[...] = jnp.zeros_like(o_ref)


def _gla(proj, zproj, wgf, bgf, wgb, bgb, onorm, s0f, s0b, T, row_off, emit_o):
    rb = row_off // T
    st_spec = pl.BlockSpec((1, 1, DV, DK), lambda b, h: (b, h, 0, 0))
    return pl.pallas_call(
        functools.partial(_gla_body, T=T, emit_o=emit_o),
        out_shape=(
            jax.ShapeDtypeStruct((B * T, VV), BF16),
            jax.ShapeDtypeStruct((B, H, DV, DK), F32),
            jax.ShapeDtypeStruct((B, H, DV, DK), F32),
        ),
        grid=(B, H),
        in_specs=[
            pl.BlockSpec((T, DK), lambda b, h: (rb + b, COL_K // DK + h)),
            pl.BlockSpec((T, DV), lambda b, h: (rb + b, COL_V // DV + h)),
            pl.BlockSpec((T, DK), lambda b, h: (rb + b, COL_Q // DK + h)),
            pl.BlockSpec((T, DV), lambda b, h: (rb + b, COL_R // DV + h)),
            pl.BlockSpec((T, ZPAD), lambda b, h: (rb + b, 0)),
            pl.BlockSpec((128, DK), lambda b, h: (0, h)),
            pl.BlockSpec((1, DK), lambda b, h: (0, h)),
            pl.BlockSpec((128, DK), lambda b, h: (0, h)),
            pl.BlockSpec((1, DK), lambda b, h: (0, h)),
            pl.BlockSpec((1, DV), lambda b, h: (0, 0)),
            st_spec,
            st_spec,
        ],
        out_specs=(
            pl.BlockSpec((T, DV), lambda b, h: (b, h)),
            st_spec,
            st_spec,
        ),
        scratch_shapes=[pltpu.VMEM((T, DK), BF16)] * 6 + [
            pltpu.VMEM((T, DV), BF16),
            pltpu.VMEM((2, T // CHUNK, 8, DK), F32),
            pltpu.VMEM((T, DV), F32),
            pltpu.VMEM((DV, DK), F32),
            pltpu.VMEM((DV, DK), F32),
        ],
        compiler_params=_cp("arbitrary", "arbitrary"),
        name="gla_scan",
    )(proj, proj, proj, proj, zproj, wgf, bgf, wgb, bgb, onorm, s0f, s0b)


def _fnet_in_body(h_ref, w_ref, cs_ref, a_ref, b_ref):
    u = _dot(h_ref[...], w_ref[...]).astype(BF16)
    for g in range(FG):
        ab = _dot(u[:, g * FD:(g + 1) * FD], cs_ref[...])
        a_ref[:, g * FD:(g + 1) * FD] = ab[:, :FD].astype(BF16)
        b_ref[:, g * FD:(g + 1) * FD] = ab[:, FD:].astype(BF16)


def _fnet_in(h, w, cs, rows):
    tm = 512
    return pl.pallas_call(
        _fnet_in_body,
        out_shape=(jax.ShapeDtypeStruct((rows, D), BF16), jax.ShapeDtypeStruct((rows, D), BF16)),
        grid=(rows // tm,),
        in_specs=[
            pl.BlockSpec((tm, D), lambda i: (i, 0)),
            pl.BlockSpec((D, D), lambda i: (0, 0)),
            pl.BlockSpec((FD, 2 * FD), lambda i: (0, 0)),
        ],
        out_specs=(pl.BlockSpec((tm, D), lambda i: (i, 0)), pl.BlockSpec((tm, D), lambda i: (i, 0))),
        compiler_params=_cp("arbitrary"),
        name="fnet_in",
    )(h, w, cs)


def _fnet_time_body(ct_ref, st_ref, a_ref, b_ref, o_ref):
    o_ref[...] = (_dot(ct_ref[...], a_ref[...]) + _dot(st_ref[...], b_ref[...])).astype(o_ref.dtype)


def _fnet_time(ct, st, a, b, T, row_off):
    tm = min(T, 512)
    tn = 1024
    rb = row_off // T
    return pl.pallas_call(
        _fnet_time_body,
        out_shape=jax.ShapeDtypeStruct((B * T, D), BF16),
        grid=(B, D // tn, T // tm),
        in_specs=[
            pl.BlockSpec((tm, T), lambda s, j, i: (i, 0)),
            pl.BlockSpec((tm, T), lambda s, j, i: (i, 0)),
            pl.BlockSpec((T, tn), lambda s, j, i: (rb + s, j)),
            pl.BlockSpec((T, tn), lambda s, j, i: (rb + s, j)),
        ],
        out_specs=pl.BlockSpec((tm, tn), lambda s, j, i: (s * (T // tm) + i, j)),
        compiler_params=_cp("arbitrary", "arbitrary", "arbitrary"),
        name="fnet_time",
    )(ct, st, a, b)


def _dft_mats(n, scale):
    idx = np.arange(n, dtype=np.int64)
    ang = 2.0 * np.pi * ((idx[:, None] * idx[None, :]) % n).astype(np.float64) / n
    return np.cos(ang) * scale, np.sin(ang) * scale


def _row_copy(src_hbm, row, dst_vmem, r, sem):
    return pltpu.make_async_copy(src_hbm.at[pl.ds(row, 1)], dst_vmem.at[pl.ds(r, 1)], sem)


def _moe_body(bs_ref, tok_ref, h_hbm, wg_ref, wu_ref, wd_ref, y_hbm, xbuf, obuf, gsem, osem, wgb, wub, wdb, *, nb):
    e = pl.program_id(0)
    b0 = bs_ref[e]
    b1 = bs_ref[e + 1]
    total = bs_ref[ME]

    def gather(g):
        slot = g % 2
        base = g * TME
        for r in range(TME):
            _row_copy(h_hbm, tok_ref[base + r], xbuf.at[slot], r, gsem.at[slot]).start(priority=r % 2)

    def out_copy(g):
        slot = g % 2
        dst = y_hbm.at[pl.ds(pl.multiple_of(g * TME, TME), TME)]
        return pltpu.make_async_copy(obuf.at[slot], dst, osem.at[slot])

    @pl.when((e == 0) & (total > 0))
    def _():
        gather(0)

    @pl.when(b1 > b0)
    def _():
        wgb[...] = wg_ref[...].astype(BF16)
        wub[...] = wu_ref[...].astype(BF16)
        wdb[...] = wd_ref[...].astype(BF16)

    def block(g, carry):
        slot = g % 2

        @pl.when(g + 1 < total)
        def _():
            gather(g + 1)

        pltpu.make_async_copy(h_hbm.at[pl.ds(0, TME)], xbuf.at[slot], gsem.at[slot]).wait()
        x_lo, x_hi = _unpack_rows(xbuf[slot])
        x = jnp.concatenate([x_lo.astype(BF16), x_hi.astype(BF16)], axis=1)
        hmid = _silu(_dot(x, wgb[...])) * _dot(x, wub[...])
        y = _pack_rows(_dot(hmid.astype(BF16), wdb[...]))

        @pl.when(g >= 2)
        def _():
            out_copy(g - 2).wait()

        obuf[slot] = y
        out_copy(g).start()
        return carry

    lax.fori_loop(b0, b1, block, 0)

    @pl.when(e == ME - 1)
    def _():
        @pl.when(total >= 2)
        def _():
            out_copy(total - 2).wait()

        @pl.when(total >= 1)
        def _():
            out_copy(total - 1).wait()

        def zero_copy(g):
            dst = y_hbm.at[pl.ds(pl.multiple_of(g * TME, TME), TME)]
            return pltpu.make_async_copy(obuf.at[0], dst, osem.at[0])

        obuf[0] = jnp.zeros((TME, HALF), jnp.uint32)
        lax.fori_loop(total, nb, lambda g, c: (zero_copy(g).start(), c)[1], 0)
        lax.fori_loop(total, nb, lambda g, c: (zero_copy(g).wait(), c)[1], 0)


def _moe_experts(layer, blk_start, buf_tok, h2, w_gate, w_up, w_down):
    nb = buf_tok.shape[0] // TME
    w_in_spec = pl.BlockSpec((None, None, D, MDE), lambda e, bs, tk: (layer, e, 0, 0))
    return pl.pallas_call(
        functools.partial(_moe_body, nb=nb),
        out_shape=jax.ShapeDtypeStruct((nb * TME, HALF), jnp.uint32),
        grid_spec=pltpu.PrefetchScalarGridSpec(
            num_scalar_prefetch=2,
            grid=(ME,),
            in_specs=[
                pl.BlockSpec(memory_space=pl.ANY),
                w_in_spec,
                w_in_spec,
                pl.BlockSpec((None, None, MDE, D), lambda e, bs, tk: (layer, e, 0, 0)),
            ],
            out_specs=pl.BlockSpec(memory_space=pl.ANY),
            scratch_shapes=[
                pltpu.VMEM((2, TME, HALF), jnp.uint32),
                pltpu.VMEM((2, TME, HALF), jnp.uint32),
                pltpu.SemaphoreType.DMA((2,)),
                pltpu.SemaphoreType.DMA((2,)),
                pltpu.VMEM((D, MDE), BF16),
                pltpu.VMEM((D, MDE), BF16),
                pltpu.VMEM((MDE, D), BF16),
            ],
        ),
        compiler_params=_cp("arbitrary"),
        name="moe_experts",
    )(blk_start, buf_tok, h2, w_gate, w_up, w_down)


TMC = 128


def _combine_body(dest_ref, x_ref, gate_ref, wt_ref, g_ref, sh_ref, sc_ref, y_hbm, *rest, nt, last):
    if last:
        h_ref, ybuf, sem = rest
    else:
        o_ref, h_ref, ybuf, sem = rest
    i = pl.program_id(0)

    def gather(tile, slot):
        base = tile * (2 * TMC)

        for r in range(TMC):
            _row_copy(y_hbm, dest_ref[base + 2 * r], ybuf.at[slot, 0], r, sem.at[slot]).start(priority=0)
            _row_copy(y_hbm, dest_ref[base + 2 * r + 1], ybuf.at[slot, 1], r, sem.at[slot]).start(priority=1)

    @pl.when(i == 0)
    def _():
        gather(0, 0)

    @pl.when(i + 1 < nt)
    def _():
        gather(i + 1, (i + 1) % 2)

    slot = i % 2
    for k in range(2):
        pltpu.make_async_copy(y_hbm.at[pl.ds(0, TMC)], ybuf.at[slot, k], sem.at[slot]).wait()
    wt = wt_ref[...]
    w0, w1 = wt[:, 0:1], wt[:, 1:2]
    lo0, hi0 = _unpack_rows(ybuf[slot, 0])
    lo1, hi1 = _unpack_rows(ybuf[slot, 1])
    y = jnp.concatenate([w0 * lo0 + w1 * lo1, w0 * hi0 + w1 * hi1], axis=1)
    xn = x_ref[...] + gate_ref[...] * y
    if last:
        h_ref[...] = xn * lax.rsqrt(jnp.mean(xn * xn, axis=-1, keepdims=True) + EPS) * g_ref[...]
    else:
        o_ref[...] = xn
        h_ref[...] = _norm_mod_val(xn, g_ref[...], sh_ref[...], sc_ref[...]).astype(h_ref.dtype)


def _combine(dest, xs, mod3, wt, yb, rows, g_next, mod3_next, last):
    nt = rows // TMC
    row_spec = pl.BlockSpec((TMC, D), lambda i, d: (i, 0))

    def mod_spec(which):
        return pl.BlockSpec((None, 1, D), lambda i, d: (_seg_of(i, TMC) * N_MOD + which, 0, 0))

    if last:
        out_shape = jax.ShapeDtypeStruct((rows, D), F32)
        out_specs = row_spec
    else:
        out_shape = (jax.ShapeDtypeStruct((rows, D), F32), jax.ShapeDtypeStruct((rows, D), BF16))
        out_specs = (row_spec, row_spec)
    return pl.pallas_call(
        functools.partial(_combine_body, nt=nt, last=last),
        out_shape=out_shape,
        grid_spec=pltpu.PrefetchScalarGridSpec(
            num_scalar_prefetch=1,
            grid=(nt,),
            in_specs=[
                row_spec,
                mod_spec(5),
                pl.BlockSpec((TMC, 128), lambda i, d: (i, 0)),
                pl.BlockSpec((1, D), lambda i, d: (0, 0)),
                mod_spec(0),
                mod_spec(1),
                pl.BlockSpec(memory_space=pl.ANY),
            ],
            out_specs=out_specs,
            scratch_shapes=[
                pltpu.VMEM((2, 2, TMC, HALF), jnp.uint32),
                pltpu.SemaphoreType.DMA((2,)),
            ],
        ),
        compiler_params=_cp("arbitrary"),
        name="moe_combine",
    )(dest, xs, mod3, wt, g_next.reshape(1, D), mod3_next, mod3_next, yb)


PLAN_CHUNK = 1024


def _dest_body(ids_ref, base_ref, o_ref):
    ids = ids_ref[...].astype(F32)
    lane_i = lax.broadcasted_iota(jnp.int32, ids.shape, 1)
    lane = lane_i.astype(F32)

    def pick(k):
        return jnp.sum(jnp.where(lane_i == k, ids, 0.0), axis=-1, keepdims=True)

    base = base_ref[...]
    d0 = jnp.sum(jnp.where(lane == pick(0), base[0:1, :], 0.0), axis=-1, keepdims=True) + pick(2)
    d1 = jnp.sum(jnp.where(lane == pick(1), base[1:2, :], 0.0), axis=-1, keepdims=True) + pick(3)
    o_ref[...] = jnp.where(lane_i == 0, d0, jnp.where(lane_i == 1, d1, 0.0)).astype(jnp.int32)


def _slot_body(dest_ref, tok_ref, *, nslots):
    i = pl.program_id(0)

    @pl.when(i == 0)
    def _():
        def clear(s, carry):
            tok_ref[s] = 0
            return carry

        lax.fori_loop(0, nslots, clear, 0, unroll=32)

    def place(t, carry):
        tok = i * PLAN_CHUNK + t
        tok_ref[dest_ref[0, 2 * t]] = tok
        tok_ref[dest_ref[0, 2 * t + 1]] = tok
        return carry

    lax.fori_loop(0, PLAN_CHUNK, place, 0, unroll=16)


def _route_plan(eid, cnt, rows):
    a = rows * 2
    nb = -(-a // TME) + ME
    steps = rows // PLAN_CHUNK
    c0 = cnt[0, :ME].astype(jnp.int32)
    c1 = cnt[1, :ME].astype(jnp.int32)
    padded = ((c0 + c1 + TME - 1) // TME) * TME
    pad_end = jnp.cumsum(padded)
    pad_start = pad_end - padded
    blk_start = jnp.concatenate([jnp.zeros((1,), jnp.int32), (pad_end // TME).astype(jnp.int32)])
    base = jnp.zeros((8, 128), F32).at[0, :ME].set(pad_start.astype(F32)).at[1, :ME].set((pad_start + c0).astype(F32))
    dest_t = pl.pallas_call(
        _dest_body,
        out_shape=jax.ShapeDtypeStruct((rows, 128), jnp.int32),
        grid=(steps,),
        in_specs=[pl.BlockSpec((PLAN_CHUNK, 128), lambda i: (i, 0)), pl.BlockSpec((8, 128), lambda i: (0, 0))],
        out_specs=pl.BlockSpec((PLAN_CHUNK, 128), lambda i: (i, 0)),
        compiler_params=_cp("arbitrary"),
        name="route_dest",
    )(eid, base)
    dest = dest_t[:, :2].reshape(a)
    buf_tok = pl.pallas_call(
        functools.partial(_slot_body, nslots=nb * TME),
        out_shape=jax.ShapeDtypeStruct((nb * TME,), jnp.int32),
        grid=(steps,),
        in_specs=[pl.BlockSpec((None, 1, 2 * PLAN_CHUNK), lambda i: (i, 0, 0), memory_space=pltpu.SMEM)],
        out_specs=pl.BlockSpec(memory_space=pltpu.SMEM),
        compiler_params=_cp("arbitrary"),
        name="route_slots",
    )(dest.reshape(steps, 1, 2 * PLAN_CHUNK))
    return dest, buf_tok, blk_start


def _sincos_2d(rows, cols, d):
    quarter = d // 4
    omega = 1.0 / (POS_BASE ** (jnp.arange(quarter, dtype=F32) / quarter))

    def axis_emb(n):
        p = jnp.arange(n, dtype=F32)[:, None] * omega[None, :]
        return jnp.concatenate([jnp.sin(p), jnp.cos(p)], axis=-1)

    er, ec = axis_emb(rows), axis_emb(cols)
    half = 2 * quarter
    pos = jnp.concatenate([jnp.broadcast_to(er[:, None, :], (rows, cols, half)),
                           jnp.broadcast_to(ec[None, :, :], (rows, cols, half))], axis=-1)
    return pos.reshape(rows * cols, 2 * half)


def _gla_weights(w_in, wg_f, wg_b):
    w_cat = jnp.concatenate([w_in[:, :QK + VV], w_in[:, STATE_COLS:]], axis=1).astype(BF16)
    w_z = jnp.concatenate([w_in[:, QK + VV:STATE_COLS], jnp.zeros((D, ZPAD - 2 * RANK), F32)], axis=1).astype(BF16)
    wgf = jnp.zeros((128, QK), F32).at[:RANK].set(wg_f)
    wgb = jnp.zeros((128, QK), F32).at[RANK:2 * RANK].set(wg_b)
    return w_cat, w_z, wgf, wgb


def kernel(x, c, ctx, c_ctx, ada_w, ada_b, norm1_g, norm2_g, gla_w_in, gla_wg_f, gla_bg_f, gla_wg_b, gla_bg_b,
           gla_onorm_g, gla_w_out, fnet_w_in, fnet_w_out, moe_rw_group, moe_rb_group, moe_rw_expert,
           moe_rb_expert, moe_w_gate, moe_w_up, moe_w_down, final_g):
    cvec = jnp.zeros((SEGS, D), F32).at[:B].set(c).at[CTX_SEG].set(c_ctx)
    mods = _mod_all(cvec, ada_w, ada_b)
    pos = _sincos_2d(SEQ // GRID_W, GRID_W, D)
    mod3s = [mods[i].reshape(SEGS * N_MOD, 1, D) for i in range(DEPTH)]
    xs, h = _assemble(x.reshape(LAT, D), pos, ctx.reshape(NCTX, D), norm1_g[0], mod3s[0])

    cc, sc = _dft_mats(FD, 1.0)
    cs = jnp.asarray(np.concatenate([cc, sc], axis=1), BF16)
    ct_l, st_l = _dft_mats(SEQ, (SEQ * FD) ** -0.5)
    ct_c, st_c = _dft_mats(CTX, (CTX * FD) ** -0.5)
    ct_l, st_l = jnp.asarray(ct_l, BF16), jnp.asarray(-st_l, BF16)
    ct_c, st_c = jnp.asarray(ct_c, BF16), jnp.asarray(-st_c, BF16)
    s_zero = jnp.zeros((B, H, DV, DK), F32)
    last_reader = ((DEPTH - 1) // 2) * 2

    for i in range(DEPTH):
        kind, j = i % 2, i // 2
        ctx_live = i < last_reader
        ctx_needed = i <= last_reader
        mod3 = mod3s[i]
        rows_in = ROWS if ctx_needed else LAT
        rows_out = ROWS if ctx_live else LAT

        if kind == 0:
            w_cat, w_z, wgf, wgb = _gla_weights(gla_w_in[j], gla_wg_f[j], gla_wg_b[j])
            bgf, bgb = gla_bg_f[j].reshape(1, QK), gla_bg_b[j].reshape(1, QK)
            onorm = gla_onorm_g[j].reshape(1, DV)
            proj = _mm(h, w_cat, rows_in, 512, 1536, BF16, "gla_proj")
            zproj = _mm(h, w_z, rows_in, 512, ZPAD, F32, "gla_gate_proj")
            o_c, sf, sb = _gla(proj, zproj, wgf, bgf, wgb, bgb, onorm, s_zero, s_zero, CTX, LAT, ctx_live)
            o_l, _, _ = _gla(proj, zproj, wgf, bgf, wgb, bgb, onorm, sf, sb, SEQ, 0, True)
            mix_l, mix_c, w_out = o_l, o_c, gla_w_out[j].astype(BF16)
        else:
            a, b = _fnet_in(h, fnet_w_in[j].astype(BF16), cs, rows_in)
            y = _fnet_time(ct_l, st_l, a, b, SEQ, 0)
            y_c = _fnet_time(ct_c, st_c, a, b, CTX, LAT) if ctx_live else y
            mix_l, mix_c, w_out = y, y_c, fnet_w_out[j].astype(BF16)

        rw = jnp.zeros((D, 128), F32).at[:, :MG].set(moe_rw_group[i])
        rw = rw.at[:, MG:MG + ME].set(jnp.transpose(moe_rw_expert[i], (1, 0, 2)).reshape(D, ME))
        rb = jnp.zeros((1, 128), F32).at[0, :MG].set(moe_rb_group[i]).at[0, MG:MG + ME].set(
            moe_rb_expert[i].reshape(ME))
        rw_hi = rw.astype(BF16)
        rw_lo = (rw - rw_hi.astype(F32)).astype(BF16)
        xs, h2, eid, wt, cnt = _mix_route(mix_l, mix_c, w_out, xs, norm2_g[i], mod3, rw_hi, rw_lo, rb, rows_out)
        dest, buf_tok, blk_start = _route_plan(eid, cnt, rows_out)
        yb = _moe_experts(i, blk_start, buf_tok, h2, moe_w_gate, moe_w_up, moe_w_down)
        if i + 1 < DEPTH:
            xs, h = _combine(dest, xs, mod3, wt, yb, rows_out, norm1_g[i + 1], mod3s[i + 1], False)
        else:
            out = _combine(dest, xs, mod3, wt, yb, rows_out, final_g, mod3, True)

    return out.reshape(B, SEQ, D)
```

```python
import functools

import jax
import jax.numpy as jnp
import numpy as np
from jax import lax
from jax.experimental import pallas as pl
from jax.experimental.pallas import tpu as pltpu

F32 = jnp.float32
BF16 = jnp.bfloat16

D = 2048
B = 4
SEQ = 2048
CTX = 256
DEPTH = 4
GRID_W = 64
EPS = 1e-6
POS_BASE = 10000.0
N_MOD = 6

H = 4
DK = 256
DV = 512
QK = H * DK
VV = H * DV
RANK = 16
GATE_NORM = 16.0
CHUNK = 64
STATE_COLS = QK + VV + 2 * RANK
ZPAD = 256
PROJ_N = QK + VV + QK + VV
COL_K, COL_V, COL_Q, COL_R = 0, QK, QK + VV, 2 * QK + VV

FG = 4
FD = D // FG

MG = 4
MPG = 8
ME = MG * MPG
MDE = D // 4
TME = 256

LAT = B * SEQ
NCTX = B * CTX
ROWS = LAT + NCTX
SEGS = 8
CTX_SEG = B

VMEM_LIMIT = 56 * 1024 * 1024


def _cp(*sem):
    return pltpu.CompilerParams(dimension_semantics=sem, vmem_limit_bytes=VMEM_LIMIT)


def _seg_of(i, tm):
    return jnp.where(i < LAT // tm, i // (SEQ // tm), CTX_SEG)


def _split2(a):
    hi = a.astype(BF16)
    lo = (a - hi.astype(F32)).astype(BF16)
    return hi, lo


def _dot(a, b):
    return jnp.dot(a, b, preferred_element_type=F32)


def _dot3(a, b_hi, b_lo):
    a_hi, a_lo = _split2(a)
    return _dot(a_hi, b_hi) + _dot(a_lo, b_hi) + _dot(a_hi, b_lo)


def _silu(a):
    return a * jax.nn.sigmoid(a)


HALF = D // 2


def _pack_rows(a):
    lo = lax.bitcast_convert_type(a[:, :HALF].astype(BF16).astype(F32), jnp.uint32)
    hi = lax.bitcast_convert_type(a[:, HALF:].astype(BF16).astype(F32), jnp.uint32)
    return (hi & jnp.uint32(0xFFFF0000)) | (lo >> 16)


def _unpack_rows(w):
    lo = lax.bitcast_convert_type(w << 16, F32)
    hi = lax.bitcast_convert_type(w & jnp.uint32(0xFFFF0000), F32)
    return lo, hi


def _mod_body(c_ref, w_ref, b_ref, o_ref):
    s = _silu(c_ref[...])
    w_hi, w_lo = _split2(w_ref[0])
    o_ref[0] = _dot3(s, w_hi, w_lo) + b_ref[0]


def _mod_all(cvec, ada_w, ada_b):
    tn = 1024
    return pl.pallas_call(
        _mod_body,
        out_shape=jax.ShapeDtypeStruct((DEPTH, SEGS, N_MOD * D), F32),
        grid=(DEPTH, N_MOD * D // tn),
        in_specs=[
            pl.BlockSpec((SEGS, D), lambda l, j: (0, 0)),
            pl.BlockSpec((1, D, tn), lambda l, j: (l, 0, j)),
            pl.BlockSpec((1, 1, tn), lambda l, j: (l, 0, j)),
        ],
        out_specs=pl.BlockSpec((1, SEGS, tn), lambda l, j: (l, 0, j)),
        compiler_params=_cp("arbitrary", "arbitrary"),
        name="adaln_mod",
    )(cvec, ada_w, ada_b.reshape(DEPTH, 1, N_MOD * D))


def _assemble_body(x_ref, pos_ref, ctx_ref, g_ref, sh_ref, sc_ref, o_ref, h_ref, *, nlat):
    i = pl.program_id(0)

    @pl.when(i < nlat)
    def _():
        o_ref[...] = x_ref[...] + pos_ref[...]

    @pl.when(i >= nlat)
    def _():
        o_ref[...] = ctx_ref[...]

    h_ref[...] = _norm_mod_val(o_ref[...], g_ref[...], sh_ref[...], sc_ref[...]).astype(h_ref.dtype)


def _assemble(x2, pos, ctx2, g, mod3):
    tm = 256
    nlat = LAT // tm
    return pl.pallas_call(
        functools.partial(_assemble_body, nlat=nlat),
        out_shape=(jax.ShapeDtypeStruct((ROWS, D), F32), jax.ShapeDtypeStruct((ROWS, D), BF16)),
        grid=(ROWS // tm,),
        in_specs=[
            pl.BlockSpec((tm, D), lambda i: (jnp.minimum(i, nlat - 1), 0)),
            pl.BlockSpec((tm, D), lambda i: (i % (SEQ // tm), 0)),
            pl.BlockSpec((tm, D), lambda i: (jnp.maximum(i - nlat, 0), 0)),
            pl.BlockSpec((1, D), lambda i: (0, 0)),
            _mod_spec(0, tm),
            _mod_spec(1, tm),
        ],
        out_specs=(pl.BlockSpec((tm, D), lambda i: (i, 0)), pl.BlockSpec((tm, D), lambda i: (i, 0))),
        compiler_params=_cp("arbitrary"),
        name="assemble_stream",
    )(x2, pos, ctx2, g.reshape(1, D), mod3, mod3)


def _norm_mod_val(x, g, sh, sc):
    y = x * lax.rsqrt(jnp.mean(x * x, axis=-1, keepdims=True) + EPS) * g
    return y * (1.0 + sc) + sh


def _mod_spec(which, tm):
    return pl.BlockSpec((None, 1, D), lambda i: (_seg_of(i, tm) * N_MOD + which, 0, 0))


def _route_tile(h2, rwh_ref, rwl_ref, rb_ref, run_s):
    lg = _dot3(h2, rwh_ref[...], rwl_ref[...]) + rb_ref[...]
    lane_i = lax.broadcasted_iota(jnp.int32, lg.shape, 1)
    lane = lane_i.astype(F32)
    neg = jnp.float32(-jnp.inf)
    big = jnp.float32(1024.0)

    glog = jnp.where(lane < MG, lg, neg)
    gmax = jnp.max(glog, axis=-1, keepdims=True)
    gidx = jnp.min(jnp.where(glog == gmax, lane, big), axis=-1, keepdims=True)
    g_w = 1.0 / jnp.sum(jnp.exp(glog - gmax), axis=-1, keepdims=True)

    lo = MG + MPG * gidx
    el = jnp.where((lane >= lo) & (lane < lo + MPG), lg, neg)
    m1 = jnp.max(el, axis=-1, keepdims=True)
    i1 = jnp.min(jnp.where(el == m1, lane, big), axis=-1, keepdims=True)
    el2 = jnp.where(lane == i1, neg, el)
    m2 = jnp.max(el2, axis=-1, keepdims=True)
    i2 = jnp.min(jnp.where(el2 == m2, lane, big), axis=-1, keepdims=True)
    e2 = jnp.exp(m2 - m1)
    den = 1.0 / (1.0 + e2)
    w1 = g_w * den
    w2 = g_w * (e2 * den)
    e_a = i1 - MG
    e_b = i2 - MG

    tm = lg.shape[0]
    r_i = lax.broadcasted_iota(jnp.int32, (tm, tm), 0)
    c_i = lax.broadcasted_iota(jnp.int32, (tm, tm), 1)
    earlier = jnp.where(r_i > c_i, 1.0, 0.0).astype(BF16)
    oh_a = jnp.where(lane == e_a, 1.0, 0.0)
    oh_b = jnp.where(lane == e_b, 1.0, 0.0)
    run = run_s[...]
    rank_a = jnp.sum(oh_a * (_dot(earlier, oh_a.astype(BF16)) + run[0:1, :]), axis=-1, keepdims=True)
    rank_b = jnp.sum(oh_b * (_dot(earlier, oh_b.astype(BF16)) + run[1:2, :]), axis=-1, keepdims=True)
    run_s[0:1, :] = run[0:1, :] + jnp.sum(oh_a, axis=0, keepdims=True)
    run_s[1:2, :] = run[1:2, :] + jnp.sum(oh_b, axis=0, keepdims=True)

    ids = jnp.where(lane_i == 0, e_a, jnp.where(lane_i == 1, e_b, jnp.where(lane_i == 2, rank_a,
                                                                           jnp.where(lane_i == 3, rank_b, 0.0))))
    wts = jnp.where(lane_i == 0, w1, jnp.where(lane_i == 1, w2, 0.0))
    return ids.astype(jnp.int32), wts


def _mix_route_body(al_ref, ac_ref, w_ref, res_ref, gate_ref, g_ref, sh_ref, sc_ref, rwh_ref, rwl_ref, rb_ref,
                    o_ref, h_ref, eid_ref, wt_ref, cnt_ref, acc_s, run_s, *, nlat):
    i = pl.program_id(0)

    @pl.when(i == 0)
    def _():
        run_s[...] = jnp.zeros_like(run_s)

    @pl.when(i < nlat)
    def _():
        acc_s[...] = _dot(al_ref[...], w_ref[...])

    @pl.when(i >= nlat)
    def _():
        acc_s[...] = _dot(ac_ref[...], w_ref[...])

    xn = res_ref[...] + gate_ref[...] * acc_s[...]
    o_ref[...] = xn
    h2 = _norm_mod_val(xn, g_ref[...], sh_ref[...], sc_ref[...])
    h_ref[...] = _pack_rows(h2)
    ids, wts = _route_tile(h2, rwh_ref, rwl_ref, rb_ref, run_s)
    eid_ref[...] = ids
    wt_ref[...] = wts
    cnt_ref[...] = run_s[...]


def _mix_route(a_lat, a_ctx, w, xs, g, mod3, rw_hi, rw_lo, rb, rows):
    tm = 256
    k = a_lat.shape[1]
    nlat = LAT // tm
    return pl.pallas_call(
        functools.partial(_mix_route_body, nlat=nlat),
        out_shape=(
            jax.ShapeDtypeStruct((rows, D), F32),
            jax.ShapeDtypeStruct((rows, HALF), jnp.uint32),
            jax.ShapeDtypeStruct((rows, 128), jnp.int32),
            jax.ShapeDtypeStruct((rows, 128), F32),
            jax.ShapeDtypeStruct((8, 128), F32),
        ),
        grid=(rows // tm,),
        in_specs=[
            pl.BlockSpec((tm, k), lambda i: (jnp.minimum(i, nlat - 1), 0)),
            pl.BlockSpec((tm, k), lambda i: (jnp.maximum(i - nlat, 0), 0)),
            pl.BlockSpec((k, D), lambda i: (0, 0)),
            pl.BlockSpec((tm, D), lambda i: (i, 0)),
            _mod_spec(2, tm),
            pl.BlockSpec((1, D), lambda i: (0, 0)),
            _mod_spec(3, tm),
            _mod_spec(4, tm),
            pl.BlockSpec((D, 128), lambda i: (0, 0)),
            pl.BlockSpec((D, 128), lambda i: (0, 0)),
            pl.BlockSpec((1, 128), lambda i: (0, 0)),
        ],
        out_specs=(
            pl.BlockSpec((tm, D), lambda i: (i, 0)),
            pl.BlockSpec((tm, HALF), lambda i: (i, 0)),
            pl.BlockSpec((tm, 128), lambda i: (i, 0)),
            pl.BlockSpec((tm, 128), lambda i: (i, 0)),
            pl.BlockSpec((8, 128), lambda i: (0, 0)),
        ),
        scratch_shapes=[pltpu.VMEM((tm, D), F32), pltpu.VMEM((8, 128), F32)],
        compiler_params=_cp("arbitrary"),
        name="mix_route",
    )(a_lat, a_ctx, w, xs, mod3, g.reshape(1, D), mod3, mod3, rw_hi, rw_lo, rb)


def _mm_body(x_ref, w_ref, o_ref):
    o_ref[...] = _dot(x_ref[...], w_ref[...]).astype(o_ref.dtype)


def _mm(x, w, rows, tm, tn, out_dtype, name):
    k = x.shape[1]
    n = w.shape[1]
    return pl.pallas_call(
        _mm_body,
        out_shape=jax.ShapeDtypeStruct((rows, n), out_dtype),
        grid=(n // tn, rows // tm),
        in_specs=[
            pl.BlockSpec((tm, k), lambda j, i: (i, 0)),
            pl.BlockSpec((k, tn), lambda j, i: (0, j)),
        ],
        out_specs=pl.BlockSpec((tm, tn), lambda j, i: (i, j)),
        compiler_params=_cp("arbitrary", "arbitrary"),
        name=name,
    )(x, w)


def _log_sigmoid(a):
    return jnp.minimum(a, 0.0) - jnp.log(1.0 + jnp.exp(-jnp.abs(a)))


PREP = 256


def _gla_body(k_ref, v_ref, q_ref, r_ref, z_ref, wgf_ref, bgf_ref, wgb_ref, bgb_ref, on_ref, s0f_ref, s0b_ref,
              o_ref, sf_ref, sb_ref,
              qf_s, kf_s, df_s, qb_s, kb_s, db_s, vb_s, dec_s, o_s, stf_s, stb_s, *, T, emit_o):
    nc = T // CHUNK
    nblk = T // PREP
    cpb = PREP // CHUNK
    scale = DK ** -0.5

    row = lax.broadcasted_iota(jnp.int32, (PREP, PREP), 0)
    col = lax.broadcasted_iota(jnp.int32, (PREP, PREP), 1)
    shift = CHUNK.bit_length() - 1
    same = lax.shift_right_logical(row, shift) == lax.shift_right_logical(col, shift)
    tri_f = jnp.where(same & (row >= col), 1.0, 0.0).astype(BF16)
    tri_b = jnp.where(same & (row <= col), 1.0, 0.0).astype(BF16)
    wf_hi, wf_lo = _split2(wgf_ref[...])
    wb_hi, wb_lo = _split2(wgb_ref[...])

    def chunk_sums(tri, g):
        g1 = g.astype(BF16)
        rem = g - g1.astype(F32)
        g2 = rem.astype(BF16)
        g3 = (rem - g2.astype(F32)).astype(BF16)
        return _dot(tri, g1) + _dot(tri, g2) + _dot(tri, g3)

    def edge_rows(G, e):
        return jnp.concatenate(
            [jnp.broadcast_to(G[c * CHUNK + e:c * CHUNK + e + 1, :], (CHUNK, DK)) for c in range(cpb)], axis=0)

    def prep(blk, carry):
        rows = pl.ds(pl.multiple_of(blk * PREP, PREP), PREP)
        z = z_ref[rows, :128]
        gf = _log_sigmoid(_dot3(z, wf_hi, wf_lo) + bgf_ref[...]) / GATE_NORM
        gb = _log_sigmoid(_dot3(z, wb_hi, wb_lo) + bgb_ref[...]) / GATE_NORM
        Gf = chunk_sums(tri_f, gf)
        Gb = chunk_sums(tri_b, gb)
        k = k_ref[rows, :].astype(F32)
        q = q_ref[rows, :].astype(F32) * scale
        qf_s[rows, :] = (q * jnp.exp(Gf)).astype(BF16)
        kf_s[rows, :] = (k * jnp.exp(-Gf)).astype(BF16)
        df_s[rows, :] = (k * jnp.exp(edge_rows(Gf, CHUNK - 1) - Gf)).astype(BF16)
        qb_s[rows, :] = (q * jnp.exp(Gb)).astype(BF16)
        kb_s[rows, :] = (k * jnp.exp(-Gb)).astype(BF16)
        db_s[rows, :] = (k * jnp.exp(edge_rows(Gb, 0) - Gb)).astype(BF16)
        vb_s[rows, :] = v_ref[rows, :]
        for c in range(cpb):
            ef = Gf[c * CHUNK + CHUNK - 1:c * CHUNK + CHUNK, :]
            eb = Gb[c * CHUNK:c * CHUNK + 1, :]
            dec_s[0, blk * cpb + c] = jnp.broadcast_to(jnp.exp(ef), (8, DK))
            dec_s[1, blk * cpb + c] = jnp.broadcast_to(jnp.exp(eb), (8, DK))
        return carry

    lax.fori_loop(0, nblk, prep, 0)

    stf_s[...] = s0f_ref[0, 0]
    stb_s[...] = s0b_ref[0, 0]
    r64 = lax.broadcasted_iota(jnp.int32, (CHUNK, CHUNK), 0)
    c64 = lax.broadcasted_iota(jnp.int32, (CHUNK, CHUNK), 1)
    nt_dims = (((1,), (1,)), ((), ()))
    tn_dims = (((0,), (0,)), ((), ()))

    def chunk(c, q_s, k_s, d_s, st_s, di, keep):
        rows = pl.ds(pl.multiple_of(c * CHUNK, CHUNK), CHUNK)
        qe = q_s[rows, :]
        vb = vb_s[rows, :]
        st = st_s[...]
        o = None
        if emit_o:
            a = lax.dot_general(qe, k_s[rows, :], nt_dims, preferred_element_type=F32)
            a = jnp.where(keep, a, 0.0).astype(BF16)
            o = _dot(a, vb) + lax.dot_general(qe, st.astype(BF16), nt_dims, preferred_element_type=F32)
        upd = lax.dot_general(vb, d_s[rows, :], tn_dims, preferred_element_type=F32)
        st_s[...] = st * dec_s[di, c][0:1, :] + upd
        return o, rows

    def step(ci, carry, accumulate):
        of, rows_f = chunk(ci, qf_s, kf_s, df_s, stf_s, 0, r64 >= c64)
        ob, rows_b = chunk(nc - 1 - ci, qb_s, kb_s, db_s, stb_s, 1, r64 <= c64)
        if emit_o:
            if accumulate:
                o_s[rows_f, :] = o_s[rows_f, :] + of
                o_s[rows_b, :] = o_s[rows_b, :] + ob
            else:
                o_s[rows_f, :] = of
                o_s[rows_b, :] = ob
        return carry

    lax.fori_loop(0, nc // 2, functools.partial(step, accumulate=False), 0, unroll=min(8, nc // 2))
    lax.fori_loop(nc // 2, nc, functools.partial(step, accumulate=True), 0, unroll=min(8, nc // 2))
    sf_ref[0, 0] = stf_s[...]
    sb_ref[0, 0] = stb_s[...]

    if emit_o:
        def fin(blk, carry):
            rows = pl.ds(pl.multiple_of(blk * PREP, PREP), PREP)
            tot = o_s[rows, :]
            y = tot * lax.rsqrt(jnp.mean(tot * tot, axis=-1, keepdims=True) + EPS) * on_ref[...]
            o_ref[rows, :] = (y * _silu(r_ref[rows, :].astype(F32))).astype(o_ref.dtype)
            return carry

        lax.fori_loop(0, nblk, fin, 0)
    else:
        o_ref[...] = jnp.zeros_like(o_ref)


def _gla(proj, zproj, wgf, bgf, wgb, bgb, onorm, s0f, s0b, T, row_off, emit_o):
    rb = row_off // T
    st_spec = pl.BlockSpec((1, 1, DV, DK), lambda b, h: (b, h, 0, 0))
    return pl.pallas_call(
        functools.partial(_gla_body, T=T, emit_o=emit_o),
        out_shape=(
            jax.ShapeDtypeStruct((B * T, VV), BF16),
            jax.ShapeDtypeStruct((B, H, DV, DK), F32),
            jax.ShapeDtypeStruct((B, H, DV, DK), F32),
        ),
        grid=(B, H),
        in_specs=[
            pl.BlockSpec((T, DK), lambda b, h: (rb + b, COL_K // DK + h)),
            pl.BlockSpec((T, DV), lambda b, h: (rb + b, COL_V // DV + h)),
            pl.BlockSpec((T, DK), lambda b, h: (rb + b, COL_Q // DK + h)),
            pl.BlockSpec((T, DV), lambda b, h: (rb + b, COL_R // DV + h)),
            pl.BlockSpec((T, ZPAD), lambda b, h: (rb + b, 0)),
            pl.BlockSpec((128, DK), lambda b, h: (0, h)),
            pl.BlockSpec((1, DK), lambda b, h: (0, h)),
            pl.BlockSpec((128, DK), lambda b, h: (0, h)),
            pl.BlockSpec((1, DK), lambda b, h: (0, h)),
            pl.BlockSpec((1, DV), lambda b, h: (0, 0)),
            st_spec,
            st_spec,
        ],
        out_specs=(
            pl.BlockSpec((T, DV), lambda b, h: (b, h)),
            st_spec,
            st_spec,
        ),
        scratch_shapes=[pltpu.VMEM((T, DK), BF16)] * 6 + [
            pltpu.VMEM((T, DV), BF16),
            pltpu.VMEM((2, T // CHUNK, 8, DK), F32),
            pltpu.VMEM((T, DV), F32),
            pltpu.VMEM((DV, DK), F32),
            pltpu.VMEM((DV, DK), F32),
        ],
        compiler_params=_cp("arbitrary", "arbitrary"),
        name="gla_scan",
    )(proj, proj, proj, proj, zproj, wgf, bgf, wgb, bgb, onorm, s0f, s0b)


def _fnet_in_body(h_ref, w_ref, cs_ref, a_ref, b_ref):
    u = _dot(h_ref[...], w_ref[...]).astype(BF16)
    for g in range(FG):
        ab = _dot(u[:, g * FD:(g + 1) * FD], cs_ref[...])
        a_ref[:, g * FD:(g + 1) * FD] = ab[:, :FD].astype(BF16)
        b_ref[:, g * FD:(g + 1) * FD] = ab[:, FD:].astype(BF16)


def _fnet_in(h, w, cs, rows):
    tm = 512
    return pl.pallas_call(
        _fnet_in_body,
        out_shape=(jax.ShapeDtypeStruct((rows, D), BF16), jax.ShapeDtypeStruct((rows, D), BF16)),
        grid=(rows // tm,),
        in_specs=[
            pl.BlockSpec((tm, D), lambda i: (i, 0)),
            pl.BlockSpec((D, D), lambda i: (0, 0)),
            pl.BlockSpec((FD, 2 * FD), lambda i: (0, 0)),
        ],
        out_specs=(pl.BlockSpec((tm, D), lambda i: (i, 0)), pl.BlockSpec((tm, D), lambda i: (i, 0))),
        compiler_params=_cp("arbitrary"),
        name="fnet_in",
    )(h, w, cs)


def _fnet_time_body(ct_ref, st_ref, a_ref, b_ref, o_ref):
    o_ref[...] = (_dot(ct_ref[...], a_ref[...]) + _dot(st_ref[...], b_ref[...])).astype(o_ref.dtype)


def _fnet_time(ct, st, a, b, T, row_off):
    tm = min(T, 512)
    tn = 1024
    rb = row_off // T
    return pl.pallas_call(
        _fnet_time_body,
        out_shape=jax.ShapeDtypeStruct((B * T, D), BF16),
        grid=(B, D // tn, T // tm),
        in_specs=[
            pl.BlockSpec((tm, T), lambda s, j, i: (i, 0)),
            pl.BlockSpec((tm, T), lambda s, j, i: (i, 0)),
            pl.BlockSpec((T, tn), lambda s, j, i: (rb + s, j)),
            pl.BlockSpec((T, tn), lambda s, j, i: (rb + s, j)),
        ],
        out_specs=pl.BlockSpec((tm, tn), lambda s, j, i: (s * (T // tm) + i, j)),
        compiler_params=_cp("arbitrary", "arbitrary", "arbitrary"),
        name="fnet_time",
    )(ct, st, a, b)


def _dft_mats(n, scale):
    idx = np.arange(n, dtype=np.int64)
    ang = 2.0 * np.pi * ((idx[:, None] * idx[None, :]) % n).astype(np.float64) / n
    return np.cos(ang) * scale, np.sin(ang) * scale


def _row_copy(src_hbm, row, dst_vmem, r, sem):
    return pltpu.make_async_copy(src_hbm.at[pl.ds(row, 1)], dst_vmem.at[pl.ds(r, 1)], sem)


def _moe_body(bs_ref, tok_ref, h_hbm, wg_ref, wu_ref, wd_ref, y_hbm, xbuf, obuf, gsem, osem, wgb, wub, wdb, *, nb):
    e = pl.program_id(0)
    b0 = bs_ref[e]
    b1 = bs_ref[e + 1]
    total = bs_ref[ME]

    def gather(g):
        slot = g % 2
        base = g * TME
        for r in range(TME):
            _row_copy(h_hbm, tok_ref[base + r], xbuf.at[slot], r, gsem.at[slot]).start(priority=r % 2)

    def out_copy(g):
        slot = g % 2
        dst = y_hbm.at[pl.ds(pl.multiple_of(g * TME, TME), TME)]
        return pltpu.make_async_copy(obuf.at[slot], dst, osem.at[slot])

    @pl.when((e == 0) & (total > 0))
    def _():
        gather(0)

    @pl.when(b1 > b0)
    def _():
        wgb[...] = wg_ref[...].astype(BF16)
        wub[...] = wu_ref[...].astype(BF16)
        wdb[...] = wd_ref[...].astype(BF16)

    def block(g, carry):
        slot = g % 2

        @pl.when(g + 1 < total)
        def _():
            gather(g + 1)

        pltpu.make_async_copy(h_hbm.at[pl.ds(0, TME)], xbuf.at[slot], gsem.at[slot]).wait()
        x_lo, x_hi = _unpack_rows(xbuf[slot])
        x = jnp.concatenate([x_lo.astype(BF16), x_hi.astype(BF16)], axis=1)
        hmid = _silu(_dot(x, wgb[...])) * _dot(x, wub[...])
        y = _pack_rows(_dot(hmid.astype(BF16), wdb[...]))

        @pl.when(g >= 2)
        def _():
            out_copy(g - 2).wait()

        obuf[slot] = y
        out_copy(g).start()
        return carry

    lax.fori_loop(b0, b1, block, 0)

    @pl.when(e == ME - 1)
    def _():
        @pl.when(total >= 2)
        def _():
            out_copy(total - 2).wait()

        @pl.when(total >= 1)
        def _():
            out_copy(total - 1).wait()

        def zero_copy(g):
            dst = y_hbm.at[pl.ds(pl.multiple_of(g * TME, TME), TME)]
            return pltpu.make_async_copy(obuf.at[0], dst, osem.at[0])

        obuf[0] = jnp.zeros((TME, HALF), jnp.uint32)
        lax.fori_loop(total, nb, lambda g, c: (zero_copy(g).start(), c)[1], 0)
        lax.fori_loop(total, nb, lambda g, c: (zero_copy(g).wait(), c)[1], 0)


def _moe_experts(layer, blk_start, buf_tok, h2, w_gate, w_up, w_down):
    nb = buf_tok.shape[0] // TME
    w_in_spec = pl.BlockSpec((None, None, D, MDE), lambda e, bs, tk: (layer, e, 0, 0))
    return pl.pallas_call(
        functools.partial(_moe_body, nb=nb),
        out_shape=jax.ShapeDtypeStruct((nb * TME, HALF), jnp.uint32),
        grid_spec=pltpu.PrefetchScalarGridSpec(
            num_scalar_prefetch=2,
            grid=(ME,),
            in_specs=[
                pl.BlockSpec(memory_space=pl.ANY),
                w_in_spec,
                w_in_spec,
                pl.BlockSpec((None, None, MDE, D), lambda e, bs, tk: (layer, e, 0, 0)),
            ],
            out_specs=pl.BlockSpec(memory_space=pl.ANY),
            scratch_shapes=[
                pltpu.VMEM((2, TME, HALF), jnp.uint32),
                pltpu.VMEM((2, TME, HALF), jnp.uint32),
                pltpu.SemaphoreType.DMA((2,)),
                pltpu.SemaphoreType.DMA((2,)),
                pltpu.VMEM((D, MDE), BF16),
                pltpu.VMEM((D, MDE), BF16),
                pltpu.VMEM((MDE, D), BF16),
            ],
        ),
        compiler_params=_cp("arbitrary"),
        name="moe_experts",
    )(blk_start, buf_tok, h2, w_gate, w_up, w_down)


TMC = 128


def _combine_body(dest_ref, x_ref, gate_ref, wt_ref, g_ref, sh_ref, sc_ref, y_hbm, *rest, nt, last):
    if last:
        h_ref, ybuf, sem = rest
    else:
        o_ref, h_ref, ybuf, sem = rest
    i = pl.program_id(0)

    def gather(tile, slot):
        base = tile * (2 * TMC)

        for r in range(TMC):
            _row_copy(y_hbm, dest_ref[base + 2 * r], ybuf.at[slot, 0], r, sem.at[slot]).start(priority=0)
            _row_copy(y_hbm, dest_ref[base + 2 * r + 1], ybuf.at[slot, 1], r, sem.at[slot]).start(priority=1)

    @pl.when(i == 0)
    def _():
        gather(0, 0)

    @pl.when(i + 1 < nt)
    def _():
        gather(i + 1, (i + 1) % 2)

    slot = i % 2
    for k in range(2):
        pltpu.make_async_copy(y_hbm.at[pl.ds(0, TMC)], ybuf.at[slot, k], sem.at[slot]).wait()
    wt = wt_ref[...]
    w0, w1 = wt[:, 0:1], wt[:, 1:2]
    lo0, hi0 = _unpack_rows(ybuf[slot, 0])
    lo1, hi1 = _unpack_rows(ybuf[slot, 1])
    y = jnp.concatenate([w0 * lo0 + w1 * lo1, w0 * hi0 + w1 * hi1], axis=1)
    xn = x_ref[...] + gate_ref[...] * y
    if last:
        h_ref[...] = xn * lax.rsqrt(jnp.mean(xn * xn, axis=-1, keepdims=True) + EPS) * g_ref[...]
    else:
        o_ref[...] = xn
        h_ref[...] = _norm_mod_val(xn, g_ref[...], sh_ref[...], sc_ref[...]).astype(h_ref.dtype)


def _combine(dest, xs, mod3, wt, yb, rows, g_next, mod3_next, last):
    nt = rows // TMC
    row_spec = pl.BlockSpec((TMC, D), lambda i, d: (i, 0))

    def mod_spec(which):
        return pl.BlockSpec((None, 1, D), lambda i, d: (_seg_of(i, TMC) * N_MOD + which, 0, 0))

    if last:
        out_shape = jax.ShapeDtypeStruct((rows, D), F32)
        out_specs = row_spec
    else:
        out_shape = (jax.ShapeDtypeStruct((rows, D), F32), jax.ShapeDtypeStruct((rows, D), BF16))
        out_specs = (row_spec, row_spec)
    return pl.pallas_call(
        functools.partial(_combine_body, nt=nt, last=last),
        out_shape=out_shape,
        grid_spec=pltpu.PrefetchScalarGridSpec(
            num_scalar_prefetch=1,
            grid=(nt,),
            in_specs=[
                row_spec,
                mod_spec(5),
                pl.BlockSpec((TMC, 128), lambda i, d: (i, 0)),
                pl.BlockSpec((1, D), lambda i, d: (0, 0)),
                mod_spec(0),
                mod_spec(1),
                pl.BlockSpec(memory_space=pl.ANY),
            ],
            out_specs=out_specs,
            scratch_shapes=[
                pltpu.VMEM((2, 2, TMC, HALF), jnp.uint32),
                pltpu.SemaphoreType.DMA((2,)),
            ],
        ),
        compiler_params=_cp("arbitrary"),
        name="moe_combine",
    )(dest, xs, mod3, wt, g_next.reshape(1, D), mod3_next, mod3_next, yb)


PLAN_CHUNK = 1024


def _dest_body(ids_ref, base_ref, o_ref):
    ids = ids_ref[...].astype(F32)
    lane_i = lax.broadcasted_iota(jnp.int32, ids.shape, 1)
    lane = lane_i.astype(F32)

    def pick(k):
        return jnp.sum(jnp.where(lane_i == k, ids, 0.0), axis=-1, keepdims=True)

    base = base_ref[...]
    d0 = jnp.sum(jnp.where(lane == pick(0), base[0:1, :], 0.0), axis=-1, keepdims=True) + pick(2)
    d1 = jnp.sum(jnp.where(lane == pick(1), base[1:2, :], 0.0), axis=-1, keepdims=True) + pick(3)
    o_ref[...] = jnp.where(lane_i == 0, d0, jnp.where(lane_i == 1, d1, 0.0)).astype(jnp.int32)


def _slot_body(dest_ref, tok_ref, *, nslots):
    i = pl.program_id(0)

    @pl.when(i == 0)
    def _():
        def clear(s, carry):
            tok_ref[s] = 0
            return carry

        lax.fori_loop(0, nslots, clear, 0, unroll=32)

    def place(t, carry):
        tok = i * PLAN_CHUNK + t
        tok_ref[dest_ref[0, 2 * t]] = tok
        tok_ref[dest_ref[0, 2 * t + 1]] = tok
        return carry

    lax.fori_loop(0, PLAN_CHUNK, place, 0, unroll=16)


def _route_plan(eid, cnt, rows):
    a = rows * 2
    nb = -(-a // TME) + ME
    steps = rows // PLAN_CHUNK
    c0 = cnt[0, :ME].astype(jnp.int32)
    c1 = cnt[1, :ME].astype(jnp.int32)
    padded = ((c0 + c1 + TME - 1) // TME) * TME
    pad_end = jnp.cumsum(padded)
    pad_start = pad_end - padded
    blk_start = jnp.concatenate([jnp.zeros((1,), jnp.int32), (pad_end // TME).astype(jnp.int32)])
    base = jnp.zeros((8, 128), F32).at[0, :ME].set(pad_start.astype(F32)).at[1, :ME].set((pad_start + c0).astype(F32))
    dest_t = pl.pallas_call(
        _dest_body,
        out_shape=jax.ShapeDtypeStruct((rows, 128), jnp.int32),
        grid=(steps,),
        in_specs=[pl.BlockSpec((PLAN_CHUNK, 128), lambda i: (i, 0)), pl.BlockSpec((8, 128), lambda i: (0, 0))],
        out_specs=pl.BlockSpec((PLAN_CHUNK, 128), lambda i: (i, 0)),
        compiler_params=_cp("arbitrary"),
        name="route_dest",
    )(eid, base)
    dest = dest_t[:, :2].reshape(a)
    buf_tok = pl.pallas_call(
        functools.partial(_slot_body, nslots=nb * TME),
        out_shape=jax.ShapeDtypeStruct((nb * TME,), jnp.int32),
        grid=(steps,),
        in_specs=[pl.BlockSpec((None, 1, 2 * PLAN_CHUNK), lambda i: (i, 0, 0), memory_space=pltpu.SMEM)],
        out_specs=pl.BlockSpec(memory_space=pltpu.SMEM),
        compiler_params=_cp("arbitrary"),
        name="route_slots",
    )(dest.reshape(steps, 1, 2 * PLAN_CHUNK))
    return dest, buf_tok, blk_start


def _sincos_2d(rows, cols, d):
    quarter = d // 4
    omega = 1.0 / (POS_BASE ** (jnp.arange(quarter, dtype=F32) / quarter))

    def axis_emb(n):
        p = jnp.arange(n, dtype=F32)[:, None] * omega[None, :]
        return jnp.concatenate([jnp.sin(p), jnp.cos(p)], axis=-1)

    er, ec = axis_emb(rows), axis_emb(cols)
    half = 2 * quarter
    pos = jnp.concatenate([jnp.broadcast_to(er[:, None, :], (rows, cols, half)),
                           jnp.broadcast_to(ec[None, :, :], (rows, cols, half))], axis=-1)
    return pos.reshape(rows * cols, 2 * half)


def _gla_weights(w_in, wg_f, wg_b):
    w_cat = jnp.concatenate([w_in[:, :QK + VV], w_in[:, STATE_COLS:]], axis=1).astype(BF16)
    w_z = jnp.concatenate([w_in[:, QK + VV:STATE_COLS], jnp.zeros((D, ZPAD - 2 * RANK), F32)], axis=1).astype(BF16)
    wgf = jnp.zeros((128, QK), F32).at[:RANK].set(wg_f)
    wgb = jnp.zeros((128, QK), F32).at[RANK:2 * RANK].set(wg_b)
    return w_cat, w_z, wgf, wgb


def kernel(x, c, ctx, c_ctx, ada_w, ada_b, norm1_g, norm2_g, gla_w_in, gla_wg_f, gla_bg_f, gla_wg_b, gla_bg_b,
           gla_onorm_g, gla_w_out, fnet_w_in, fnet_w_out, moe_rw_group, moe_rb_group, moe_rw_expert,
           moe_rb_expert, moe_w_gate, moe_w_up, moe_w_down, final_g):
    cvec = jnp.zeros((SEGS, D), F32).at[:B].set(c).at[CTX_SEG].set(c_ctx)
    mods = _mod_all(cvec, ada_w, ada_b)
    pos = _sincos_2d(SEQ // GRID_W, GRID_W, D)
    mod3s = [mods[i].reshape(SEGS * N_MOD, 1, D) for i in range(DEPTH)]
    xs, h = _assemble(x.reshape(LAT, D), pos, ctx.reshape(NCTX, D), norm1_g[0], mod3s[0])

    cc, sc = _dft_mats(FD, 1.0)
    cs = jnp.asarray(np.concatenate([cc, sc], axis=1), BF16)
    ct_l, st_l = _dft_mats(SEQ, (SEQ * FD) ** -0.5)
    ct_c, st_c = _dft_mats(CTX, (CTX * FD) ** -0.5)
    ct_l, st_l = jnp.asarray(ct_l, BF16), jnp.asarray(-st_l, BF16)
    ct_c, st_c = jnp.asarray(ct_c, BF16), jnp.asarray(-st_c, BF16)
    s_zero = jnp.zeros((B, H, DV, DK), F32)
    last_reader = ((DEPTH - 1) // 2) * 2

    for i in range(DEPTH):
        kind, j = i % 2, i // 2
        ctx_live = i < last_reader
        ctx_needed = i <= last_reader
        mod3 = mod3s[i]
        rows_in = ROWS if ctx_needed else LAT
        rows_out = ROWS if ctx_live else LAT

        if kind == 0:
            w_cat, w_z, wgf, wgb = _gla_weights(gla_w_in[j], gla_wg_f[j], gla_wg_b[j])
            bgf, bgb = gla_bg_f[j].reshape(1, QK), gla_bg_b[j].reshape(1, QK)
            onorm = gla_onorm_g[j].reshape(1, DV)
            proj = _mm(h, w_cat, rows_in, 1024, 1536, BF16, "gla_proj")
            zproj = _mm(h, w_z, rows_in, 512, ZPAD, F32, "gla_gate_proj")
            o_c, sf, sb = _gla(proj, zproj, wgf, bgf, wgb, bgb, onorm, s_zero, s_zero, CTX, LAT, ctx_live)
            o_l, _, _ = _gla(proj, zproj, wgf, bgf, wgb, bgb, onorm, sf, sb, SEQ, 0, True)
            mix_l, mix_c, w_out = o_l, o_c, gla_w_out[j].astype(BF16)
        else:
            a, b = _fnet_in(h, fnet_w_in[j].astype(BF16), cs, rows_in)
            y = _fnet_time(ct_l, st_l, a, b, SEQ, 0)
            y_c = _fnet_time(ct_c, st_c, a, b, CTX, LAT) if ctx_live else y
            mix_l, mix_c, w_out = y, y_c, fnet_w_out[j].astype(BF16)

        rw = jnp.zeros((D, 128), F32).at[:, :MG].set(moe_rw_group[i])
        rw = rw.at[:, MG:MG + ME].set(jnp.transpose(moe_rw_expert[i], (1, 0, 2)).reshape(D, ME))
        rb = jnp.zeros((1, 128), F32).at[0, :MG].set(moe_rb_group[i]).at[0, MG:MG + ME].set(
            moe_rb_expert[i].reshape(ME))
        rw_hi = rw.astype(BF16)
        rw_lo = (rw - rw_hi.astype(F32)).astype(BF16)
        xs, h2, eid, wt, cnt = _mix_route(mix_l, mix_c, w_out, xs, norm2_g[i], mod3, rw_hi, rw_lo, rb, rows_out)
        dest, buf_tok, blk_start = _route_plan(eid, cnt, rows_out)
        yb = _moe_experts(i, blk_start, buf_tok, h2, moe_w_gate, moe_w_up, moe_w_down)
        if i + 1 < DEPTH:
            xs, h = _combine(dest, xs, mod3, wt, yb, rows_out, norm1_g[i + 1], mod3s[i + 1], False)
        else:
            out = _combine(dest, xs, mod3, wt, yb, rows_out, final_g, mod3, True)

    return out.reshape(B, SEQ, D)
```

```python
import functools

import jax
import jax.numpy as jnp
import numpy as np
from jax import lax
from jax.experimental import pallas as pl
from jax.experimental.pallas import tpu as pltpu

F32 = jnp.float32
BF16 = jnp.bfloat16

D = 2048
B = 4
SEQ = 2048
CTX = 256
DEPTH = 4
GRID_W = 64
EPS = 1e-6
POS_BASE = 10000.0
N_MOD = 6

H = 4
DK = 256
DV = 512
QK = H * DK
VV = H * DV
RANK = 16
GATE_NORM = 16.0
CHUNK = 64
STATE_COLS = QK + VV + 2 * RANK
ZPAD = 256
PROJ_N = QK + VV + QK + VV
COL_K, COL_V, COL_Q, COL_R = 0, QK, QK + VV, 2 * QK + VV

FG = 4
FD = D // FG

MG = 4
MPG = 8
ME = MG * MPG
MDE = D // 4
TME = 256

LAT = B * SEQ
NCTX = B * CTX
ROWS = LAT + NCTX
SEGS = 8
CTX_SEG = B

VMEM_LIMIT = 56 * 1024 * 1024


def _cp(*sem):
    return pltpu.CompilerParams(dimension_semantics=sem, vmem_limit_bytes=VMEM_LIMIT)


def _seg_of(i, tm):
    return jnp.where(i < LAT // tm, i // (SEQ // tm), CTX_SEG)


def _split2(a):
    hi = a.astype(BF16)
    lo = (a - hi.astype(F32)).astype(BF16)
    return hi, lo


def _dot(a, b):
    return jnp.dot(a, b, preferred_element_type=F32)


def _dot3(a, b_hi, b_lo):
    a_hi, a_lo = _split2(a)
    return _dot(a_hi, b_hi) + _dot(a_lo, b_hi) + _dot(a_hi, b_lo)


def _silu(a):
    return a * jax.nn.sigmoid(a)


HALF = D // 2


def _pack_rows(a):
    lo = lax.bitcast_convert_type(a[:, :HALF].astype(BF16).astype(F32), jnp.uint32)
    hi = lax.bitcast_convert_type(a[:, HALF:].astype(BF16).astype(F32), jnp.uint32)
    return (hi & jnp.uint32(0xFFFF0000)) | (lo >> 16)


def _unpack_rows(w):
    lo = lax.bitcast_convert_type(w << 16, F32)
    hi = lax.bitcast_convert_type(w & jnp.uint32(0xFFFF0000), F32)
    return lo, hi


def _mod_body(c_ref, w_ref, b_ref, o_ref):
    s = _silu(c_ref[...])
    w_hi, w_lo = _split2(w_ref[0])
    o_ref[0] = _dot3(s, w_hi, w_lo) + b_ref[0]


def _mod_all(cvec, ada_w, ada_b):
    tn = 1024
    return pl.pallas_call(
        _mod_body,
        out_shape=jax.ShapeDtypeStruct((DEPTH, SEGS, N_MOD * D), F32),
        grid=(DEPTH, N_MOD * D // tn),
        in_specs=[
            pl.BlockSpec((SEGS, D), lambda l, j: (0, 0)),
            pl.BlockSpec((1, D, tn), lambda l, j: (l, 0, j)),
            pl.BlockSpec((1, 1, tn), lambda l, j: (l, 0, j)),
        ],
        out_specs=pl.BlockSpec((1, SEGS, tn), lambda l, j: (l, 0, j)),
        compiler_params=_cp("arbitrary", "arbitrary"),
        name="adaln_mod",
    )(cvec, ada_w, ada_b.reshape(DEPTH, 1, N_MOD * D))


def _assemble_body(x_ref, pos_ref, ctx_ref, g_ref, sh_ref, sc_ref, o_ref, h_ref, *, nlat):
    i = pl.program_id(0)

    @pl.when(i < nlat)
    def _():
        o_ref[...] = x_ref[...] + pos_ref[...]

    @pl.when(i >= nlat)
    def _():
        o_ref[...] = ctx_ref[...]

    h_ref[...] = _norm_mod_val(o_ref[...], g_ref[...], sh_ref[...], sc_ref[...]).astype(h_ref.dtype)


def _assemble(x2, pos, ctx2, g, mod3):
    tm = 256
    nlat = LAT // tm
    return pl.pallas_call(
        functools.partial(_assemble_body, nlat=nlat),
        out_shape=(jax.ShapeDtypeStruct((ROWS, D), F32), jax.ShapeDtypeStruct((ROWS, D), BF16)),
        grid=(ROWS // tm,),
        in_specs=[
            pl.BlockSpec((tm, D), lambda i: (jnp.minimum(i, nlat - 1), 0)),
            pl.BlockSpec((tm, D), lambda i: (i % (SEQ // tm), 0)),
            pl.BlockSpec((tm, D), lambda i: (jnp.maximum(i - nlat, 0), 0)),
            pl.BlockSpec((1, D), lambda i: (0, 0)),
            _mod_spec(0, tm),
            _mod_spec(1, tm),
        ],
        out_specs=(pl.BlockSpec((tm, D), lambda i: (i, 0)), pl.BlockSpec((tm, D), lambda i: (i, 0))),
        compiler_params=_cp("arbitrary"),
        name="assemble_stream",
    )(x2, pos, ctx2, g.reshape(1, D), mod3, mod3)


def _norm_mod_val(x, g, sh, sc):
    y = x * lax.rsqrt(jnp.mean(x * x, axis=-1, keepdims=True) + EPS) * g
    return y * (1.0 + sc) + sh


def _mod_spec(which, tm):
    return pl.BlockSpec((None, 1, D), lambda i: (_seg_of(i, tm) * N_MOD + which, 0, 0))


def _route_tile(h2, rwc_ref, rb_ref, run_s):
    h_hi, h_lo = _split2(h2)
    p = _dot(h_hi, rwc_ref[...])
    lg = p[:, :128] + p[:, 128:] + _dot(h_lo, rwc_ref[:, :128]) + rb_ref[...]
    lane_i = lax.broadcasted_iota(jnp.int32, lg.shape, 1)
    lane = lane_i.astype(F32)
    neg = jnp.float32(-jnp.inf)
    big = jnp.float32(1024.0)

    glog = jnp.where(lane < MG, lg, neg)
    gmax = jnp.max(glog, axis=-1, keepdims=True)
    gidx = jnp.min(jnp.where(glog == gmax, lane, big), axis=-1, keepdims=True)
    g_w = 1.0 / jnp.sum(jnp.exp(glog - gmax), axis=-1, keepdims=True)

    lo = MG + MPG * gidx
    el = jnp.where((lane >= lo) & (lane < lo + MPG), lg, neg)
    m1 = jnp.max(el, axis=-1, keepdims=True)
    i1 = jnp.min(jnp.where(el == m1, lane, big), axis=-1, keepdims=True)
    el2 = jnp.where(lane == i1, neg, el)
    m2 = jnp.max(el2, axis=-1, keepdims=True)
    i2 = jnp.min(jnp.where(el2 == m2, lane, big), axis=-1, keepdims=True)
    e2 = jnp.exp(m2 - m1)
    den = 1.0 / (1.0 + e2)
    w1 = g_w * den
    w2 = g_w * (e2 * den)
    e_a = i1 - MG
    e_b = i2 - MG

    tm = lg.shape[0]
    r_i = lax.broadcasted_iota(jnp.int32, (tm, tm), 0)
    c_i = lax.broadcasted_iota(jnp.int32, (tm, tm), 1)
    earlier = jnp.where(r_i > c_i, 1.0, 0.0).astype(BF16)
    oh_a = jnp.where(lane == e_a, 1.0, 0.0)
    oh_b = jnp.where(lane == e_b, 1.0, 0.0)
    run = run_s[...]
    rank_a = jnp.sum(oh_a * (_dot(earlier, oh_a.astype(BF16)) + run[0:1, :]), axis=-1, keepdims=True)
    rank_b = jnp.sum(oh_b * (_dot(earlier, oh_b.astype(BF16)) + run[1:2, :]), axis=-1, keepdims=True)
    run_s[0:1, :] = run[0:1, :] + jnp.sum(oh_a, axis=0, keepdims=True)
    run_s[1:2, :] = run[1:2, :] + jnp.sum(oh_b, axis=0, keepdims=True)

    ids = jnp.where(lane_i == 0, e_a, jnp.where(lane_i == 1, e_b, jnp.where(lane_i == 2, rank_a,
                                                                           jnp.where(lane_i == 3, rank_b, 0.0))))
    wts = jnp.where(lane_i == 0, w1, jnp.where(lane_i == 1, w2, 0.0))
    return ids.astype(jnp.int32), wts


def _mix_route_body(al_ref, ac_ref, w_ref, res_ref, gate_ref, g_ref, sh_ref, sc_ref, rwc_ref, rb_ref,
                    o_ref, h_ref, eid_ref, wt_ref, cnt_ref, acc_s, run_s, *, nlat):
    i = pl.program_id(0)

    @pl.when(i == 0)
    def _():
        run_s[...] = jnp.zeros_like(run_s)

    @pl.when(i < nlat)
    def _():
        acc_s[...] = _dot(al_ref[...], w_ref[...])

    @pl.when(i >= nlat)
    def _():
        acc_s[...] = _dot(ac_ref[...], w_ref[...])

    xn = res_ref[...] + gate_ref[...] * acc_s[...]
    o_ref[...] = xn
    h2 = _norm_mod_val(xn, g_ref[...], sh_ref[...], sc_ref[...])
    h_ref[...] = _pack_rows(h2)
    ids, wts = _route_tile(h2, rwc_ref, rb_ref, run_s)
    eid_ref[...] = ids
    wt_ref[...] = wts
    cnt_ref[...] = run_s[...]


def _mix_route(a_lat, a_ctx, w, xs, g, mod3, rw_cat, rb, rows):
    tm = 256
    k = a_lat.shape[1]
    nlat = LAT // tm
    return pl.pallas_call(
        functools.partial(_mix_route_body, nlat=nlat),
        out_shape=(
            jax.ShapeDtypeStruct((rows, D), F32),
            jax.ShapeDtypeStruct((rows, HALF), jnp.uint32),
            jax.ShapeDtypeStruct((rows, 128), jnp.int32),
            jax.ShapeDtypeStruct((rows, 128), F32),
            jax.ShapeDtypeStruct((8, 128), F32),
        ),
        grid=(rows // tm,),
        in_specs=[
            pl.BlockSpec((tm, k), lambda i: (jnp.minimum(i, nlat - 1), 0)),
            pl.BlockSpec((tm, k), lambda i: (jnp.maximum(i - nlat, 0), 0)),
            pl.BlockSpec((k, D), lambda i: (0, 0)),
            pl.BlockSpec((tm, D), lambda i: (i, 0)),
            _mod_spec(2, tm),
            pl.BlockSpec((1, D), lambda i: (0, 0)),
            _mod_spec(3, tm),
            _mod_spec(4, tm),
            pl.BlockSpec((D, 256), lambda i: (0, 0)),
            pl.BlockSpec((1, 128), lambda i: (0, 0)),
        ],
        out_specs=(
            pl.BlockSpec((tm, D), lambda i: (i, 0)),
            pl.BlockSpec((tm, HALF), lambda i: (i, 0)),
            pl.BlockSpec((tm, 128), lambda i: (i, 0)),
            pl.BlockSpec((tm, 128), lambda i: (i, 0)),
            pl.BlockSpec((8, 128), lambda i: (0, 0)),
        ),
        scratch_shapes=[pltpu.VMEM((tm, D), F32), pltpu.VMEM((8, 128), F32)],
        compiler_params=_cp("arbitrary"),
        name="mix_route",
    )(a_lat, a_ctx, w, xs, mod3, g.reshape(1, D), mod3, mod3, rw_cat, rb)


def _mm_body(x_ref, w_ref, o_ref):
    o_ref[...] = _dot(x_ref[...], w_ref[...]).astype(o_ref.dtype)


def _mm(x, w, rows, tm, tn, out_dtype, name):
    k = x.shape[1]
    n = w.shape[1]
    return pl.pallas_call(
        _mm_body,
        out_shape=jax.ShapeDtypeStruct((rows, n), out_dtype),
        grid=(n // tn, rows // tm),
        in_specs=[
            pl.BlockSpec((tm, k), lambda j, i: (i, 0)),
            pl.BlockSpec((k, tn), lambda j, i: (0, j)),
        ],
        out_specs=pl.BlockSpec((tm, tn), lambda j, i: (i, j)),
        compiler_params=_cp("arbitrary", "arbitrary"),
        name=name,
    )(x, w)


def _log_sigmoid(a):
    return jnp.minimum(a, 0.0) - jnp.log(1.0 + jnp.exp(-jnp.abs(a)))


PREP = 256


def _gla_body(k_ref, v_ref, q_ref, r_ref, z_ref, wgf_ref, bgf_ref, wgb_ref, bgb_ref, on_ref, s0f_ref, s0b_ref,
              o_ref, sf_ref, sb_ref,
              qf_s, kf_s, df_s, qb_s, kb_s, db_s, vb_s, dec_s, o_s, stf_s, stb_s, *, T, emit_o):
    nc = T // CHUNK
    nblk = T // PREP
    cpb = PREP // CHUNK
    scale = DK ** -0.5

    row = lax.broadcasted_iota(jnp.int32, (PREP, PREP), 0)
    col = lax.broadcasted_iota(jnp.int32, (PREP, PREP), 1)
    shift = CHUNK.bit_length() - 1
    same = lax.shift_right_logical(row, shift) == lax.shift_right_logical(col, shift)
    tri_f = jnp.where(same & (row >= col), 1.0, 0.0).astype(BF16)
    tri_b = jnp.where(same & (row <= col), 1.0, 0.0).astype(BF16)
    wf_hi, wf_lo = _split2(wgf_ref[...])
    wb_hi, wb_lo = _split2(wgb_ref[...])

    def chunk_sums(tri, g):
        g1 = g.astype(BF16)
        rem = g - g1.astype(F32)
        g2 = rem.astype(BF16)
        g3 = (rem - g2.astype(F32)).astype(BF16)
        return _dot(tri, g1) + _dot(tri, g2) + _dot(tri, g3)

    def edge_rows(G, e):
        return jnp.concatenate(
            [jnp.broadcast_to(G[c * CHUNK + e:c * CHUNK + e + 1, :], (CHUNK, DK)) for c in range(cpb)], axis=0)

    def prep(blk, carry):
        rows = pl.ds(pl.multiple_of(blk * PREP, PREP), PREP)
        z = z_ref[rows, :128]
        gf = _log_sigmoid(_dot3(z, wf_hi, wf_lo) + bgf_ref[...]) / GATE_NORM
        gb = _log_sigmoid(_dot3(z, wb_hi, wb_lo) + bgb_ref[...]) / GATE_NORM
        Gf = chunk_sums(tri_f, gf)
        Gb = chunk_sums(tri_b, gb)
        k = k_ref[rows, :].astype(F32)
        q = q_ref[rows, :].astype(F32) * scale
        qf_s[rows, :] = (q * jnp.exp(Gf)).astype(BF16)
        kf_s[rows, :] = (k * jnp.exp(-Gf)).astype(BF16)
        df_s[rows, :] = (k * jnp.exp(edge_rows(Gf, CHUNK - 1) - Gf)).astype(BF16)
        qb_s[rows, :] = (q * jnp.exp(Gb)).astype(BF16)
        kb_s[rows, :] = (k * jnp.exp(-Gb)).astype(BF16)
        db_s[rows, :] = (k * jnp.exp(edge_rows(Gb, 0) - Gb)).astype(BF16)
        vb_s[rows, :] = v_ref[rows, :]
        for c in range(cpb):
            ef = Gf[c * CHUNK + CHUNK - 1:c * CHUNK + CHUNK, :]
            eb = Gb[c * CHUNK:c * CHUNK + 1, :]
            dec_s[0, blk * cpb + c] = jnp.broadcast_to(jnp.exp(ef), (8, DK))
            dec_s[1, blk * cpb + c] = jnp.broadcast_to(jnp.exp(eb), (8, DK))
        return carry

    lax.fori_loop(0, nblk, prep, 0)

    stf_s[...] = s0f_ref[0, 0]
    stb_s[...] = s0b_ref[0, 0]
    r64 = lax.broadcasted_iota(jnp.int32, (CHUNK, CHUNK), 0)
    c64 = lax.broadcasted_iota(jnp.int32, (CHUNK, CHUNK), 1)
    nt_dims = (((1,), (1,)), ((), ()))
    tn_dims = (((0,), (0,)), ((), ()))

    def chunk(c, q_s, k_s, d_s, st_s, di, keep):
        rows = pl.ds(pl.multiple_of(c * CHUNK, CHUNK), CHUNK)
        qe = q_s[rows, :]
        vb = vb_s[rows, :]
        st = st_s[...]
        o = None
        if emit_o:
            a = lax.dot_general(qe, k_s[rows, :], nt_dims, preferred_element_type=F32)
            a = jnp.where(keep, a, 0.0).astype(BF16)
            o = _dot(a, vb) + lax.dot_general(qe, st.astype(BF16), nt_dims, preferred_element_type=F32)
        upd = lax.dot_general(vb, d_s[rows, :], tn_dims, preferred_element_type=F32)
        st_s[...] = st * dec_s[di, c][0:1, :] + upd
        return o, rows

    def step(ci, carry, accumulate):
        of, rows_f = chunk(ci, qf_s, kf_s, df_s, stf_s, 0, r64 >= c64)
        ob, rows_b = chunk(nc - 1 - ci, qb_s, kb_s, db_s, stb_s, 1, r64 <= c64)
        if emit_o:
            if accumulate:
                o_s[rows_f, :] = o_s[rows_f, :] + of
                o_s[rows_b, :] = o_s[rows_b, :] + ob
            else:
                o_s[rows_f, :] = of
                o_s[rows_b, :] = ob
        return carry

    lax.fori_loop(0, nc // 2, functools.partial(step, accumulate=False), 0, unroll=min(8, nc // 2))
    lax.fori_loop(nc // 2, nc, functools.partial(step, accumulate=True), 0, unroll=min(8, nc // 2))
    sf_ref[0, 0] = stf_s[...]
    sb_ref[0, 0] = stb_s[...]

    if emit_o:
        def fin(blk, carry):
            rows = pl.ds(pl.multiple_of(blk * PREP, PREP), PREP)
            tot = o_s[rows, :]
            y = tot * lax.rsqrt(jnp.mean(tot * tot, axis=-1, keepdims=True) + EPS) * on_ref[...]
            o_ref[rows, :] = (y * _silu(r_ref[rows, :].astype(F32))).astype(o_ref.dtype)
            return carry

        lax.fori_loop(0, nblk, fin, 0)
    else:
        o_ref[...] = jnp.zeros_like(o_ref)


def _gla(proj, zproj, wgf, bgf, wgb, bgb, onorm, s0f, s0b, T, row_off, emit_o):
    rb = row_off // T
    st_spec = pl.BlockSpec((1, 1, DV, DK), lambda b, h: (b, h, 0, 0))
    return pl.pallas_call(
        functools.partial(_gla_body, T=T, emit_o=emit_o),
        out_shape=(
            jax.ShapeDtypeStruct((B * T, VV), BF16),
            jax.ShapeDtypeStruct((B, H, DV, DK), F32),
            jax.ShapeDtypeStruct((B, H, DV, DK), F32),
        ),
        grid=(B, H),
        in_specs=[
            pl.BlockSpec((T, DK), lambda b, h: (rb + b, COL_K // DK + h)),
            pl.BlockSpec((T, DV), lambda b, h: (rb + b, COL_V // DV + h)),
            pl.BlockSpec((T, DK), lambda b, h: (rb + b, COL_Q // DK + h)),
            pl.BlockSpec((T, DV), lambda b, h: (rb + b, COL_R // DV + h)),
            pl.BlockSpec((T, ZPAD), lambda b, h: (rb + b, 0)),
            pl.BlockSpec((128, DK), lambda b, h: (0, h)),
            pl.BlockSpec((1, DK), lambda b, h: (0, h)),
            pl.BlockSpec((128, DK), lambda b, h: (0, h)),
            pl.BlockSpec((1, DK), lambda b, h: (0, h)),
            pl.BlockSpec((1, DV), lambda b, h: (0, 0)),
            st_spec,
            st_spec,
        ],
        out_specs=(
            pl.BlockSpec((T, DV), lambda b, h: (b, h)),
            st_spec,
            st_spec,
        ),
        scratch_shapes=[pltpu.VMEM((T, DK), BF16)] * 6 + [
            pltpu.VMEM((T, DV), BF16),
            pltpu.VMEM((2, T // CHUNK, 8, DK), F32),
            pltpu.VMEM((T, DV), F32),
            pltpu.VMEM((DV, DK), F32),
            pltpu.VMEM((DV, DK), F32),
        ],
        compiler_params=_cp("arbitrary", "arbitrary"),
        name="gla_scan",
    )(proj, proj, proj, proj, zproj, wgf, bgf, wgb, bgb, onorm, s0f, s0b)


def _fnet_in_body(h_ref, w_ref, cs_ref, a_ref, b_ref):
    u = _dot(h_ref[...], w_ref[...]).astype(BF16)
    for g in range(FG):
        ab = _dot(u[:, g * FD:(g + 1) * FD], cs_ref[...])
        a_ref[:, g * FD:(g + 1) * FD] = ab[:, :FD].astype(BF16)
        b_ref[:, g * FD:(g + 1) * FD] = ab[:, FD:].astype(BF16)


def _fnet_in(h, w, cs, rows):
    tm = 512
    return pl.pallas_call(
        _fnet_in_body,
        out_shape=(jax.ShapeDtypeStruct((rows, D), BF16), jax.ShapeDtypeStruct((rows, D), BF16)),
        grid=(rows // tm,),
        in_specs=[
            pl.BlockSpec((tm, D), lambda i: (i, 0)),
            pl.BlockSpec((D, D), lambda i: (0, 0)),
            pl.BlockSpec((FD, 2 * FD), lambda i: (0, 0)),
        ],
        out_specs=(pl.BlockSpec((tm, D), lambda i: (i, 0)), pl.BlockSpec((tm, D), lambda i: (i, 0))),
        compiler_params=_cp("arbitrary"),
        name="fnet_in",
    )(h, w, cs)


def _fnet_time_body(ct_ref, st_ref, a_ref, b_ref, o_ref):
    o_ref[...] = (_dot(ct_ref[...], a_ref[...]) + _dot(st_ref[...], b_ref[...])).astype(o_ref.dtype)


def _fnet_time(ct, st, a, b, T, row_off):
    tm = min(T, 512)
    tn = 1024
    rb = row_off // T
    return pl.pallas_call(
        _fnet_time_body,
        out_shape=jax.ShapeDtypeStruct((B * T, D), BF16),
        grid=(B, D // tn, T // tm),
        in_specs=[
            pl.BlockSpec((tm, T), lambda s, j, i: (i, 0)),
            pl.BlockSpec((tm, T), lambda s, j, i: (i, 0)),
            pl.BlockSpec((T, tn), lambda s, j, i: (rb + s, j)),
            pl.BlockSpec((T, tn), lambda s, j, i: (rb + s, j)),
        ],
        out_specs=pl.BlockSpec((tm, tn), lambda s, j, i: (s * (T // tm) + i, j)),
        compiler_params=_cp("arbitrary", "arbitrary", "arbitrary"),
        name="fnet_time",
    )(ct, st, a, b)


def _dft_mats(n, scale):
    idx = np.arange(n, dtype=np.int64)
    ang = 2.0 * np.pi * ((idx[:, None] * idx[None, :]) % n).astype(np.float64) / n
    return np.cos(ang) * scale, np.sin(ang) * scale


def _row_copy(src_hbm, row, dst_vmem, r, sem):
    return pltpu.make_async_copy(src_hbm.at[pl.ds(row, 1)], dst_vmem.at[pl.ds(r, 1)], sem)


GCH = 32


def _moe_body(bs_ref, nv_ref, tok_ref, h_hbm, wg_ref, wu_ref, wd_ref, y_hbm, xbuf, obuf, gsem, osem, wgb, wub, wdb,
              *, nb):
    e = pl.program_id(0)
    b0 = bs_ref[e]
    b1 = bs_ref[e + 1]
    total = bs_ref[ME]

    def gather(g):
        slot = g % 2
        base = g * TME
        for c in range(TME // GCH):
            @pl.when(c * GCH < nv_ref[g])
            def _():
                for r in range(c * GCH, (c + 1) * GCH):
                    _row_copy(h_hbm, tok_ref[base + r], xbuf.at[slot], r, gsem.at[slot]).start(priority=r % 2)

    def gather_wait(g):
        slot = g % 2
        for c in range(TME // GCH):
            @pl.when(c * GCH < nv_ref[g])
            def _():
                pltpu.make_async_copy(h_hbm.at[pl.ds(0, GCH)], xbuf.at[slot, pl.ds(c * GCH, GCH)],
                                      gsem.at[slot]).wait()

    def out_copy(g):
        slot = g % 2
        dst = y_hbm.at[pl.ds(pl.multiple_of(g * TME, TME), TME)]
        return pltpu.make_async_copy(obuf.at[slot], dst, osem.at[slot])

    @pl.when(e == 0)
    def _():
        xbuf[...] = jnp.zeros_like(xbuf)

        @pl.when(total > 0)
        def _():
            gather(0)

    @pl.when(b1 > b0)
    def _():
        wgb[...] = wg_ref[...].astype(BF16)
        wub[...] = wu_ref[...].astype(BF16)
        wdb[...] = wd_ref[...].astype(BF16)

    def block(g, carry):
        slot = g % 2

        @pl.when(g + 1 < total)
        def _():
            gather(g + 1)

        gather_wait(g)
        x_lo, x_hi = _unpack_rows(xbuf[slot])
        x = jnp.concatenate([x_lo.astype(BF16), x_hi.astype(BF16)], axis=1)
        hmid = _silu(_dot(x, wgb[...])) * _dot(x, wub[...])
        y = _pack_rows(_dot(hmid.astype(BF16), wdb[...]))

        @pl.when(g >= 2)
        def _():
            out_copy(g - 2).wait()

        obuf[slot] = y
        out_copy(g).start()
        return carry

    lax.fori_loop(b0, b1, block, 0)

    @pl.when(e == ME - 1)
    def _():
        @pl.when(total >= 2)
        def _():
            out_copy(total - 2).wait()

        @pl.when(total >= 1)
        def _():
            out_copy(total - 1).wait()

        def zero_copy(g):
            dst = y_hbm.at[pl.ds(pl.multiple_of(g * TME, TME), TME)]
            return pltpu.make_async_copy(obuf.at[0], dst, osem.at[0])

        obuf[0] = jnp.zeros((TME, HALF), jnp.uint32)
        lax.fori_loop(total, nb, lambda g, c: (zero_copy(g).start(), c)[1], 0)
        lax.fori_loop(total, nb, lambda g, c: (zero_copy(g).wait(), c)[1], 0)


def _moe_experts(layer, blk_start, blk_rows, buf_tok, h2, w_gate, w_up, w_down):
    nb = buf_tok.shape[0] // TME
    w_in_spec = pl.BlockSpec((None, None, D, MDE), lambda e, bs, nv, tk: (layer, e, 0, 0))
    return pl.pallas_call(
        functools.partial(_moe_body, nb=nb),
        out_shape=jax.ShapeDtypeStruct((nb * TME, HALF), jnp.uint32),
        grid_spec=pltpu.PrefetchScalarGridSpec(
            num_scalar_prefetch=3,
            grid=(ME,),
            in_specs=[
                pl.BlockSpec(memory_space=pl.ANY),
                w_in_spec,
                w_in_spec,
                pl.BlockSpec((None, None, MDE, D), lambda e, bs, nv, tk: (layer, e, 0, 0)),
            ],
            out_specs=pl.BlockSpec(memory_space=pl.ANY),
            scratch_shapes=[
                pltpu.VMEM((2, TME, HALF), jnp.uint32),
                pltpu.VMEM((2, TME, HALF), jnp.uint32),
                pltpu.SemaphoreType.DMA((2,)),
                pltpu.SemaphoreType.DMA((2,)),
                pltpu.VMEM((D, MDE), BF16),
                pltpu.VMEM((D, MDE), BF16),
                pltpu.VMEM((MDE, D), BF16),
            ],
        ),
        compiler_params=_cp("arbitrary"),
        name="moe_experts",
    )(blk_start, blk_rows, buf_tok, h2, w_gate, w_up, w_down)


TMC = 128


def _combine_body(dest_ref, x_ref, gate_ref, wt_ref, g_ref, sh_ref, sc_ref, y_hbm, *rest, nt, last):
    if last:
        h_ref, ybuf, sem = rest
    else:
        o_ref, h_ref, ybuf, sem = rest
    i = pl.program_id(0)

    def gather(tile, slot):
        base = tile * (2 * TMC)

        for r in range(TMC):
            _row_copy(y_hbm, dest_ref[base + 2 * r], ybuf.at[slot, 0], r, sem.at[slot]).start(priority=0)
            _row_copy(y_hbm, dest_ref[base + 2 * r + 1], ybuf.at[slot, 1], r, sem.at[slot]).start(priority=1)

    @pl.when(i == 0)
    def _():
        gather(0, 0)

    @pl.when(i + 1 < nt)
    def _():
        gather(i + 1, (i + 1) % 2)

    slot = i % 2
    for k in range(2):
        pltpu.make_async_copy(y_hbm.at[pl.ds(0, TMC)], ybuf.at[slot, k], sem.at[slot]).wait()
    wt = wt_ref[...]
    w0, w1 = wt[:, 0:1], wt[:, 1:2]
    lo0, hi0 = _unpack_rows(ybuf[slot, 0])
    lo1, hi1 = _unpack_rows(ybuf[slot, 1])
    y = jnp.concatenate([w0 * lo0 + w1 * lo1, w0 * hi0 + w1 * hi1], axis=1)
    xn = x_ref[...] + gate_ref[...] * y
    if last:
        h_ref[...] = xn * lax.rsqrt(jnp.mean(xn * xn, axis=-1, keepdims=True) + EPS) * g_ref[...]
    else:
        o_ref[...] = xn
        h_ref[...] = _norm_mod_val(xn, g_ref[...], sh_ref[...], sc_ref[...]).astype(h_ref.dtype)


def _combine(dest, xs, mod3, wt, yb, rows, g_next, mod3_next, last):
    nt = rows // TMC
    row_spec = pl.BlockSpec((TMC, D), lambda i, d: (i, 0))

    def mod_spec(which):
        return pl.BlockSpec((None, 1, D), lambda i, d: (_seg_of(i, TMC) * N_MOD + which, 0, 0))

    if last:
        out_shape = jax.ShapeDtypeStruct((rows, D), F32)
        out_specs = row_spec
    else:
        out_shape = (jax.ShapeDtypeStruct((rows, D), F32), jax.ShapeDtypeStruct((rows, D), BF16))
        out_specs = (row_spec, row_spec)
    return pl.pallas_call(
        functools.partial(_combine_body, nt=nt, last=last),
        out_shape=out_shape,
        grid_spec=pltpu.PrefetchScalarGridSpec(
            num_scalar_prefetch=1,
            grid=(nt,),
            in_specs=[
                row_spec,
                mod_spec(5),
                pl.BlockSpec((TMC, 128), lambda i, d: (i, 0)),
                pl.BlockSpec((1, D), lambda i, d: (0, 0)),
                mod_spec(0),
                mod_spec(1),
                pl.BlockSpec(memory_space=pl.ANY),
            ],
            out_specs=out_specs,
            scratch_shapes=[
                pltpu.VMEM((2, 2, TMC, HALF), jnp.uint32),
                pltpu.SemaphoreType.DMA((2,)),
            ],
        ),
        compiler_params=_cp("arbitrary"),
        name="moe_combine",
    )(dest, xs, mod3, wt, g_next.reshape(1, D), mod3_next, mod3_next, yb)


PLAN_CHUNK = 1024


def _dest_body(ids_ref, base_ref, o_ref):
    ids = ids_ref[...].astype(F32)
    lane_i = lax.broadcasted_iota(jnp.int32, ids.shape, 1)
    lane = lane_i.astype(F32)

    def pick(k):
        return jnp.sum(jnp.where(lane_i == k, ids, 0.0), axis=-1, keepdims=True)

    base = base_ref[...]
    d0 = jnp.sum(jnp.where(lane == pick(0), base[0:1, :], 0.0), axis=-1, keepdims=True) + pick(2)
    d1 = jnp.sum(jnp.where(lane == pick(1), base[1:2, :], 0.0), axis=-1, keepdims=True) + pick(3)
    o_ref[...] = jnp.where(lane_i == 0, d0, jnp.where(lane_i == 1, d1, 0.0)).astype(jnp.int32)


def _slot_body(dest_ref, tok_ref, *, nslots):
    i = pl.program_id(0)

    @pl.when(i == 0)
    def _():
        def clear(s, carry):
            tok_ref[s] = 0
            return carry

        lax.fori_loop(0, nslots, clear, 0, unroll=32)

    def place(t, carry):
        tok = i * PLAN_CHUNK + t
        tok_ref[dest_ref[0, 2 * t]] = tok
        tok_ref[dest_ref[0, 2 * t + 1]] = tok
        return carry

    lax.fori_loop(0, PLAN_CHUNK, place, 0, unroll=16)


def _route_plan(eid, cnt, rows):
    a = rows * 2
    nb = -(-a // TME) + ME
    steps = rows // PLAN_CHUNK
    c0 = cnt[0, :ME].astype(jnp.int32)
    c1 = cnt[1, :ME].astype(jnp.int32)
    padded = ((c0 + c1 + TME - 1) // TME) * TME
    pad_end = jnp.cumsum(padded)
    pad_start = pad_end - padded
    blk_start = jnp.concatenate([jnp.zeros((1,), jnp.int32), (pad_end // TME).astype(jnp.int32)])
    blk = jnp.arange(nb, dtype=jnp.int32)
    blk_e = jnp.minimum(jnp.searchsorted(blk_start[1:], blk, side='right'), ME - 1)
    blk_rows = jnp.clip((c0 + c1)[blk_e] - (blk - blk_start[blk_e]) * TME, 0, TME).astype(jnp.int32)
    base = jnp.zeros((8, 128), F32).at[0, :ME].set(pad_start.astype(F32)).at[1, :ME].set((pad_start + c0).astype(F32))
    dest_t = pl.pallas_call(
        _dest_body,
        out_shape=jax.ShapeDtypeStruct((rows, 128), jnp.int32),
        grid=(steps,),
        in_specs=[pl.BlockSpec((PLAN_CHUNK, 128), lambda i: (i, 0)), pl.BlockSpec((8, 128), lambda i: (0, 0))],
        out_specs=pl.BlockSpec((PLAN_CHUNK, 128), lambda i: (i, 0)),
        compiler_params=_cp("arbitrary"),
        name="route_dest",
    )(eid, base)
    dest = dest_t[:, :2].reshape(a)
    buf_tok = pl.pallas_call(
        functools.partial(_slot_body, nslots=nb * TME),
        out_shape=jax.ShapeDtypeStruct((nb * TME,), jnp.int32),
        grid=(steps,),
        in_specs=[pl.BlockSpec((None, 1, 2 * PLAN_CHUNK), lambda i: (i, 0, 0), memory_space=pltpu.SMEM)],
        out_specs=pl.BlockSpec(memory_space=pltpu.SMEM),
        compiler_params=_cp("arbitrary"),
        name="route_slots",
    )(dest.reshape(steps, 1, 2 * PLAN_CHUNK))
    return dest, buf_tok, blk_start, blk_rows


def _sincos_2d(rows, cols, d):
    quarter = d // 4
    omega = 1.0 / (POS_BASE ** (jnp.arange(quarter, dtype=F32) / quarter))

    def axis_emb(n):
        p = jnp.arange(n, dtype=F32)[:, None] * omega[None, :]
        return jnp.concatenate([jnp.sin(p), jnp.cos(p)], axis=-1)

    er, ec = axis_emb(rows), axis_emb(cols)
    half = 2 * quarter
    pos = jnp.concatenate([jnp.broadcast_to(er[:, None, :], (rows, cols, half)),
                           jnp.broadcast_to(ec[None, :, :], (rows, cols, half))], axis=-1)
    return pos.reshape(rows * cols, 2 * half)


def _gla_weights(w_in, wg_f, wg_b):
    w_cat = jnp.concatenate([w_in[:, :QK + VV], w_in[:, STATE_COLS:]], axis=1).astype(BF16)
    w_z = jnp.concatenate([w_in[:, QK + VV:STATE_COLS], jnp.zeros((D, ZPAD - 2 * RANK), F32)], axis=1).astype(BF16)
    wgf = jnp.zeros((128, QK), F32).at[:RANK].set(wg_f)
    wgb = jnp.zeros((128, QK), F32).at[RANK:2 * RANK].set(wg_b)
    return w_cat, w_z, wgf, wgb


def kernel(x, c, ctx, c_ctx, ada_w, ada_b, norm1_g, norm2_g, gla_w_in, gla_wg_f, gla_bg_f, gla_wg_b, gla_bg_b,
           gla_onorm_g, gla_w_out, fnet_w_in, fnet_w_out, moe_rw_group, moe_rb_group, moe_rw_expert,
           moe_rb_expert, moe_w_gate, moe_w_up, moe_w_down, final_g):
    cvec = jnp.zeros((SEGS, D), F32).at[:B].set(c).at[CTX_SEG].set(c_ctx)
    mods = _mod_all(cvec, ada_w, ada_b)
    pos = _sincos_2d(SEQ // GRID_W, GRID_W, D)
    mod3s = [mods[i].reshape(SEGS * N_MOD, 1, D) for i in range(DEPTH)]
    xs, h = _assemble(x.reshape(LAT, D), pos, ctx.reshape(NCTX, D), norm1_g[0], mod3s[0])

    cc, sc = _dft_mats(FD, 1.0)
    cs = jnp.asarray(np.concatenate([cc, sc], axis=1), BF16)
    ct_l, st_l = _dft_mats(SEQ, (SEQ * FD) ** -0.5)
    ct_c, st_c = _dft_mats(CTX, (CTX * FD) ** -0.5)
    ct_l, st_l = jnp.asarray(ct_l, BF16), jnp.asarray(-st_l, BF16)
    ct_c, st_c = jnp.asarray(ct_c, BF16), jnp.asarray(-st_c, BF16)
    s_zero = jnp.zeros((B, H, DV, DK), F32)
    last_reader = ((DEPTH - 1) // 2) * 2

    for i in range(DEPTH):
        kind, j = i % 2, i // 2
        ctx_live = i < last_reader
        ctx_needed = i <= last_reader
        mod3 = mod3s[i]
        rows_in = ROWS if ctx_needed else LAT
        rows_out = ROWS if ctx_live else LAT

        if kind == 0:
            w_cat, w_z, wgf, wgb = _gla_weights(gla_w_in[j], gla_wg_f[j], gla_wg_b[j])
            bgf, bgb = gla_bg_f[j].reshape(1, QK), gla_bg_b[j].reshape(1, QK)
            onorm = gla_onorm_g[j].reshape(1, DV)
            proj = _mm(h, w_cat, rows_in, 1024, 1536, BF16, "gla_proj")
            zproj = _mm(h, w_z, rows_in, 512, ZPAD, F32, "gla_gate_proj")
            o_c, sf, sb = _gla(proj, zproj, wgf, bgf, wgb, bgb, onorm, s_zero, s_zero, CTX, LAT, ctx_live)
            o_l, _, _ = _gla(proj, zproj, wgf, bgf, wgb, bgb, onorm, sf, sb, SEQ, 0, True)
            mix_l, mix_c, w_out = o_l, o_c, gla_w_out[j].astype(BF16)
        else:
            a, b = _fnet_in(h, fnet_w_in[j].astype(BF16), cs, rows_in)
            y = _fnet_time(ct_l, st_l, a, b, SEQ, 0)
            y_c = _fnet_time(ct_c, st_c, a, b, CTX, LAT) if ctx_live else y
            mix_l, mix_c, w_out = y, y_c, fnet_w_out[j].astype(BF16)

        rw = jnp.zeros((D, 128), F32).at[:, :MG].set(moe_rw_group[i])
        rw = rw.at[:, MG:MG + ME].set(jnp.transpose(moe_rw_expert[i], (1, 0, 2)).reshape(D, ME))
        rb = jnp.zeros((1, 128), F32).at[0, :MG].set(moe_rb_group[i]).at[0, MG:MG + ME].set(
            moe_rb_expert[i].reshape(ME))
        rw_hi = rw.astype(BF16)
        rw_cat = jnp.concatenate([rw_hi, (rw - rw_hi.astype(F32)).astype(BF16)], axis=1)
        xs, h2, eid, wt, cnt = _mix_route(mix_l, mix_c, w_out, xs, norm2_g[i], mod3, rw_cat, rb, rows_out)
        dest, buf_tok, blk_start, blk_rows = _route_plan(eid, cnt, rows_out)
        yb = _moe_experts(i, blk_start, blk_rows, buf_tok, h2, moe_w_gate, moe_w_up, moe_w_down)
        if i + 1 < DEPTH:
            xs, h = _combine(dest, xs, mod3, wt, yb, rows_out, norm1_g[i + 1], mod3s[i + 1], False)
        else:
            out = _combine(dest, xs, mod3, wt, yb, rows_out, final_g, mod3, True)

    return out.reshape(B, SEQ, D)
```

```python
import functools

import jax
import jax.numpy as jnp
import numpy as np
from jax import lax
from jax.experimental import pallas as pl
from jax.experimental.pallas import tpu as pltpu

F32 = jnp.float32
BF16 = jnp.bfloat16

D = 2048
B = 4
SEQ = 2048
CTX = 256
DEPTH = 4
GRID_W = 64
EPS = 1e-6
POS_BASE = 10000.0
N_MOD = 6

H = 4
DK = 256
DV = 512
QK = H * DK
VV = H * DV
RANK = 16
GATE_NORM = 16.0
CHUNK = 64
STATE_COLS = QK + VV + 2 * RANK
ZPAD = 256
PROJ_N = QK + VV + QK + VV
COL_K, COL_V, COL_Q, COL_R = 0, QK, QK + VV, 2 * QK + VV

FG = 4
FD = D // FG

MG = 4
MPG = 8
ME = MG * MPG
MDE = D // 4
TME = 256

LAT = B * SEQ
NCTX = B * CTX
ROWS = LAT + NCTX
SEGS = 8
CTX_SEG = B

VMEM_LIMIT = 56 * 1024 * 1024


def _cp(*sem):
    return pltpu.CompilerParams(dimension_semantics=sem, vmem_limit_bytes=VMEM_LIMIT)


def _seg_of(i, tm):
    return jnp.where(i < LAT // tm, i // (SEQ // tm), CTX_SEG)


def _split2(a):
    hi = a.astype(BF16)
    lo = (a - hi.astype(F32)).astype(BF16)
    return hi, lo


def _dot(a, b):
    return jnp.dot(a, b, preferred_element_type=F32)


def _dot3(a, b_hi, b_lo):
    a_hi, a_lo = _split2(a)
    return _dot(a_hi, b_hi) + _dot(a_lo, b_hi) + _dot(a_hi, b_lo)


def _silu(a):
    return a * jax.nn.sigmoid(a)


HALF = D // 2


def _pack_rows(a):
    lo = lax.bitcast_convert_type(a[:, :HALF].astype(BF16).astype(F32), jnp.uint32)
    hi = lax.bitcast_convert_type(a[:, HALF:].astype(BF16).astype(F32), jnp.uint32)
    return (hi & jnp.uint32(0xFFFF0000)) | (lo >> 16)


def _unpack_rows(w):
    lo = lax.bitcast_convert_type(w << 16, F32)
    hi = lax.bitcast_convert_type(w & jnp.uint32(0xFFFF0000), F32)
    return lo, hi


def _mod_body(c_ref, w_ref, b_ref, o_ref):
    s = _silu(c_ref[...])
    w_hi, w_lo = _split2(w_ref[0])
    o_ref[0] = _dot3(s, w_hi, w_lo) + b_ref[0]


def _mod_all(cvec, ada_w, ada_b):
    tn = 1024
    return pl.pallas_call(
        _mod_body,
        out_shape=jax.ShapeDtypeStruct((DEPTH, SEGS, N_MOD * D), F32),
        grid=(DEPTH, N_MOD * D // tn),
        in_specs=[
            pl.BlockSpec((SEGS, D), lambda l, j: (0, 0)),
            pl.BlockSpec((1, D, tn), lambda l, j: (l, 0, j)),
            pl.BlockSpec((1, 1, tn), lambda l, j: (l, 0, j)),
        ],
        out_specs=pl.BlockSpec((1, SEGS, tn), lambda l, j: (l, 0, j)),
        compiler_params=_cp("arbitrary", "arbitrary"),
        name="adaln_mod",
    )(cvec, ada_w, ada_b.reshape(DEPTH, 1, N_MOD * D))


def _assemble_body(x_ref, pos_ref, ctx_ref, g_ref, sh_ref, sc_ref, o_ref, h_ref, *, nlat):
    i = pl.program_id(0)

    @pl.when(i < nlat)
    def _():
        o_ref[...] = x_ref[...] + pos_ref[...]

    @pl.when(i >= nlat)
    def _():
        o_ref[...] = ctx_ref[...]

    h_ref[...] = _norm_mod_val(o_ref[...], g_ref[...], sh_ref[...], sc_ref[...]).astype(h_ref.dtype)


def _assemble(x2, pos, ctx2, g, mod3):
    tm = 256
    nlat = LAT // tm
    return pl.pallas_call(
        functools.partial(_assemble_body, nlat=nlat),
        out_shape=(jax.ShapeDtypeStruct((ROWS, D), F32), jax.ShapeDtypeStruct((ROWS, D), BF16)),
        grid=(ROWS // tm,),
        in_specs=[
            pl.BlockSpec((tm, D), lambda i: (jnp.minimum(i, nlat - 1), 0)),
            pl.BlockSpec((tm, D), lambda i: (i % (SEQ // tm), 0)),
            pl.BlockSpec((tm, D), lambda i: (jnp.maximum(i - nlat, 0), 0)),
            pl.BlockSpec((1, D), lambda i: (0, 0)),
            _mod_spec(0, tm),
            _mod_spec(1, tm),
        ],
        out_specs=(pl.BlockSpec((tm, D), lambda i: (i, 0)), pl.BlockSpec((tm, D), lambda i: (i, 0))),
        compiler_params=_cp("arbitrary"),
        name="assemble_stream",
    )(x2, pos, ctx2, g.reshape(1, D), mod3, mod3)


def _norm_mod_val(x, g, sh, sc):
    y = x * lax.rsqrt(jnp.mean(x * x, axis=-1, keepdims=True) + EPS) * g
    return y * (1.0 + sc) + sh


def _mod_spec(which, tm):
    return pl.BlockSpec((None, 1, D), lambda i: (_seg_of(i, tm) * N_MOD + which, 0, 0))


def _route_tile(h2, rwc_ref, rb_ref, run_s):
    h_hi, h_lo = _split2(h2)
    p = _dot(h_hi, rwc_ref[...])
    lg = p[:, :128] + p[:, 128:] + _dot(h_lo, rwc_ref[:, :128]) + rb_ref[...]
    lane_i = lax.broadcasted_iota(jnp.int32, lg.shape, 1)
    lane = lane_i.astype(F32)
    neg = jnp.float32(-jnp.inf)
    big = jnp.float32(1024.0)

    glog = jnp.where(lane < MG, lg, neg)
    gmax = jnp.max(glog, axis=-1, keepdims=True)
    gidx = jnp.min(jnp.where(glog == gmax, lane, big), axis=-1, keepdims=True)
    g_w = 1.0 / jnp.sum(jnp.exp(glog - gmax), axis=-1, keepdims=True)

    lo = MG + MPG * gidx
    el = jnp.where((lane >= lo) & (lane < lo + MPG), lg, neg)
    m1 = jnp.max(el, axis=-1, keepdims=True)
    i1 = jnp.min(jnp.where(el == m1, lane, big), axis=-1, keepdims=True)
    el2 = jnp.where(lane == i1, neg, el)
    m2 = jnp.max(el2, axis=-1, keepdims=True)
    i2 = jnp.min(jnp.where(el2 == m2, lane, big), axis=-1, keepdims=True)
    e2 = jnp.exp(m2 - m1)
    den = 1.0 / (1.0 + e2)
    w1 = g_w * den
    w2 = g_w * (e2 * den)
    e_a = i1 - MG
    e_b = i2 - MG

    tm = lg.shape[0]
    r_i = lax.broadcasted_iota(jnp.int32, (tm, tm), 0)
    c_i = lax.broadcasted_iota(jnp.int32, (tm, tm), 1)
    earlier = jnp.where(r_i > c_i, 1.0, 0.0).astype(BF16)
    oh_a = jnp.where(lane == e_a, 1.0, 0.0)
    oh_b = jnp.where(lane == e_b, 1.0, 0.0)
    run = run_s[...]
    rank_a = jnp.sum(oh_a * (_dot(earlier, oh_a.astype(BF16)) + run[0:1, :]), axis=-1, keepdims=True)
    rank_b = jnp.sum(oh_b * (_dot(earlier, oh_b.astype(BF16)) + run[1:2, :]), axis=-1, keepdims=True)
    run_s[0:1, :] = run[0:1, :] + jnp.sum(oh_a, axis=0, keepdims=True)
    run_s[1:2, :] = run[1:2, :] + jnp.sum(oh_b, axis=0, keepdims=True)

    ids = jnp.where(lane_i == 0, e_a, jnp.where(lane_i == 1, e_b, jnp.where(lane_i == 2, rank_a,
                                                                           jnp.where(lane_i == 3, rank_b, 0.0))))
    wts = jnp.where(lane_i == 0, w1, jnp.where(lane_i == 1, w2, 0.0))
    return ids.astype(jnp.int32), wts


def _mix_route_body(al_ref, ac_ref, w_ref, res_ref, gate_ref, g_ref, sh_ref, sc_ref, rwc_ref, rb_ref,
                    o_ref, h_ref, eid_ref, wt_ref, cnt_ref, acc_s, run_s, *, nlat):
    i = pl.program_id(0)

    @pl.when(i == 0)
    def _():
        run_s[...] = jnp.zeros_like(run_s)

    @pl.when(i < nlat)
    def _():
        acc_s[...] = _dot(al_ref[...], w_ref[...])

    @pl.when(i >= nlat)
    def _():
        acc_s[...] = _dot(ac_ref[...], w_ref[...])

    xn = res_ref[...] + gate_ref[...] * acc_s[...]
    o_ref[...] = xn
    h2 = _norm_mod_val(xn, g_ref[...], sh_ref[...], sc_ref[...])
    h_ref[...] = _pack_rows(h2)
    ids, wts = _route_tile(h2, rwc_ref, rb_ref, run_s)
    eid_ref[...] = ids
    wt_ref[...] = wts
    cnt_ref[...] = run_s[...]


def _mix_route(a_lat, a_ctx, w, xs, g, mod3, rw_cat, rb, rows):
    tm = 256
    k = a_lat.shape[1]
    nlat = LAT // tm
    return pl.pallas_call(
        functools.partial(_mix_route_body, nlat=nlat),
        out_shape=(
            jax.ShapeDtypeStruct((rows, D), F32),
            jax.ShapeDtypeStruct((rows, HALF), jnp.uint32),
            jax.ShapeDtypeStruct((rows, 128), jnp.int32),
            jax.ShapeDtypeStruct((rows, 128), F32),
            jax.ShapeDtypeStruct((8, 128), F32),
        ),
        grid=(rows // tm,),
        in_specs=[
            pl.BlockSpec((tm, k), lambda i: (jnp.minimum(i, nlat - 1), 0)),
            pl.BlockSpec((tm, k), lambda i: (jnp.maximum(i - nlat, 0), 0)),
            pl.BlockSpec((k, D), lambda i: (0, 0)),
            pl.BlockSpec((tm, D), lambda i: (i, 0)),
            _mod_spec(2, tm),
            pl.BlockSpec((1, D), lambda i: (0, 0)),
            _mod_spec(3, tm),
            _mod_spec(4, tm),
            pl.BlockSpec((D, 256), lambda i: (0, 0)),
            pl.BlockSpec((1, 128), lambda i: (0, 0)),
        ],
        out_specs=(
            pl.BlockSpec((tm, D), lambda i: (i, 0)),
            pl.BlockSpec((tm, HALF), lambda i: (i, 0)),
            pl.BlockSpec((tm, 128), lambda i: (i, 0)),
            pl.BlockSpec((tm, 128), lambda i: (i, 0)),
            pl.BlockSpec((8, 128), lambda i: (0, 0)),
        ),
        scratch_shapes=[pltpu.VMEM((tm, D), F32), pltpu.VMEM((8, 128), F32)],
        compiler_params=_cp("arbitrary"),
        name="mix_route",
    )(a_lat, a_ctx, w, xs, mod3, g.reshape(1, D), mod3, mod3, rw_cat, rb)


def _mm_body(x_ref, w_ref, o_ref):
    o_ref[...] = _dot(x_ref[...], w_ref[...]).astype(o_ref.dtype)


def _mm(x, w, rows, tm, tn, out_dtype, name):
    k = x.shape[1]
    n = w.shape[1]
    return pl.pallas_call(
        _mm_body,
        out_shape=jax.ShapeDtypeStruct((rows, n), out_dtype),
        grid=(n // tn, rows // tm),
        in_specs=[
            pl.BlockSpec((tm, k), lambda j, i: (i, 0)),
            pl.BlockSpec((k, tn), lambda j, i: (0, j)),
        ],
        out_specs=pl.BlockSpec((tm, tn), lambda j, i: (i, j)),
        compiler_params=_cp("arbitrary", "arbitrary"),
        name=name,
    )(x, w)


def _log_sigmoid(a):
    return jnp.minimum(a, 0.0) - jnp.log(1.0 + jnp.exp(-jnp.abs(a)))


PREP = 256


def _gla_body(k_ref, v_ref, q_ref, r_ref, z_ref, wgf_ref, bgf_ref, wgb_ref, bgb_ref, on_ref, s0f_ref, s0b_ref,
              o_ref, sf_ref, sb_ref,
              qf_s, kf_s, df_s, qb_s, kb_s, db_s, vb_s, dec_s, o_s, stf_s, stb_s, *, T, emit_o):
    nc = T // CHUNK
    nblk = T // PREP
    cpb = PREP // CHUNK
    scale = DK ** -0.5

    row = lax.broadcasted_iota(jnp.int32, (PREP, PREP), 0)
    col = lax.broadcasted_iota(jnp.int32, (PREP, PREP), 1)
    shift = CHUNK.bit_length() - 1
    same = lax.shift_right_logical(row, shift) == lax.shift_right_logical(col, shift)
    tri_f = jnp.where(same & (row >= col), 1.0, 0.0).astype(BF16)
    tri_b = jnp.where(same & (row <= col), 1.0, 0.0).astype(BF16)
    wf_hi, wf_lo = _split2(wgf_ref[...])
    wb_hi, wb_lo = _split2(wgb_ref[...])

    def chunk_sums(tri, g):
        g1, g2 = _split2(g)
        return _dot(tri, g1) + _dot(tri, g2)

    def edge_rows(G, e):
        return jnp.concatenate(
            [jnp.broadcast_to(G[c * CHUNK + e:c * CHUNK + e + 1, :], (CHUNK, DK)) for c in range(cpb)], axis=0)

    def prep(blk, carry):
        rows = pl.ds(pl.multiple_of(blk * PREP, PREP), PREP)
        z = z_ref[rows, :128]
        gf = _log_sigmoid(_dot3(z, wf_hi, wf_lo) + bgf_ref[...]) / GATE_NORM
        gb = _log_sigmoid(_dot3(z, wb_hi, wb_lo) + bgb_ref[...]) / GATE_NORM
        Gf = chunk_sums(tri_f, gf)
        Gb = chunk_sums(tri_b, gb)
        k = k_ref[rows, :].astype(F32)
        q = q_ref[rows, :].astype(F32) * scale
        qf_s[rows, :] = (q * jnp.exp(Gf)).astype(BF16)
        kf_s[rows, :] = (k * jnp.exp(-Gf)).astype(BF16)
        df_s[rows, :] = (k * jnp.exp(edge_rows(Gf, CHUNK - 1) - Gf)).astype(BF16)
        qb_s[rows, :] = (q * jnp.exp(Gb)).astype(BF16)
        kb_s[rows, :] = (k * jnp.exp(-Gb)).astype(BF16)
        db_s[rows, :] = (k * jnp.exp(edge_rows(Gb, 0) - Gb)).astype(BF16)
        vb_s[rows, :] = v_ref[rows, :]
        for c in range(cpb):
            ef = Gf[c * CHUNK + CHUNK - 1:c * CHUNK + CHUNK, :]
            eb = Gb[c * CHUNK:c * CHUNK + 1, :]
            dec_s[0, blk * cpb + c] = jnp.broadcast_to(jnp.exp(ef), (8, DK))
            dec_s[1, blk * cpb + c] = jnp.broadcast_to(jnp.exp(eb), (8, DK))
        return carry

    lax.fori_loop(0, nblk, prep, 0)

    stf_s[...] = s0f_ref[0, 0]
    stb_s[...] = s0b_ref[0, 0]
    r64 = lax.broadcasted_iota(jnp.int32, (CHUNK, CHUNK), 0)
    c64 = lax.broadcasted_iota(jnp.int32, (CHUNK, CHUNK), 1)
    nt_dims = (((1,), (1,)), ((), ()))
    tn_dims = (((0,), (0,)), ((), ()))

    def chunk(c, q_s, k_s, d_s, st_s, di, keep):
        rows = pl.ds(pl.multiple_of(c * CHUNK, CHUNK), CHUNK)
        qe = q_s[rows, :]
        vb = vb_s[rows, :]
        st = st_s[...]
        o = None
        if emit_o:
            a = lax.dot_general(qe, k_s[rows, :], nt_dims, preferred_element_type=F32)
            a = jnp.where(keep, a, 0.0).astype(BF16)
            o = _dot(a, vb) + lax.dot_general(qe, st.astype(BF16), nt_dims, preferred_element_type=F32)
        upd = lax.dot_general(vb, d_s[rows, :], tn_dims, preferred_element_type=F32)
        st_s[...] = st * dec_s[di, c][0:1, :] + upd
        return o, rows

    def step(ci, carry, accumulate):
        of, rows_f = chunk(ci, qf_s, kf_s, df_s, stf_s, 0, r64 >= c64)
        ob, rows_b = chunk(nc - 1 - ci, qb_s, kb_s, db_s, stb_s, 1, r64 <= c64)
        if emit_o:
            if accumulate:
                o_s[rows_f, :] = o_s[rows_f, :] + of
                o_s[rows_b, :] = o_s[rows_b, :] + ob
            else:
                o_s[rows_f, :] = of
                o_s[rows_b, :] = ob
        return carry

    lax.fori_loop(0, nc // 2, functools.partial(step, accumulate=False), 0, unroll=min(8, nc // 2))
    lax.fori_loop(nc // 2, nc, functools.partial(step, accumulate=True), 0, unroll=min(8, nc // 2))
    sf_ref[0, 0] = stf_s[...]
    sb_ref[0, 0] = stb_s[...]

    if emit_o:
        def fin(blk, carry):
            rows = pl.ds(pl.multiple_of(blk * PREP, PREP), PREP)
            tot = o_s[rows, :]
            y = tot * lax.rsqrt(jnp.mean(tot * tot, axis=-1, keepdims=True) + EPS) * on_ref[...]
            o_ref[rows, :] = (y * _silu(r_ref[rows, :].astype(F32))).astype(o_ref.dtype)
            return carry

        lax.fori_loop(0, nblk, fin, 0)
    else:
        o_ref[...] = jnp.zeros_like(o_ref)


def _gla(proj, zproj, wgf, bgf, wgb, bgb, onorm, s0f, s0b, T, row_off, emit_o):
    rb = row_off // T
    st_spec = pl.BlockSpec((1, 1, DV, DK), lambda b, h: (b, h, 0, 0))
    return pl.pallas_call(
        functools.partial(_gla_body, T=T, emit_o=emit_o),
        out_shape=(
            jax.ShapeDtypeStruct((B * T, VV), BF16),
            jax.ShapeDtypeStruct((B, H, DV, DK), F32),
            jax.ShapeDtypeStruct((B, H, DV, DK), F32),
        ),
        grid=(B, H),
        in_specs=[
            pl.BlockSpec((T, DK), lambda b, h: (rb + b, COL_K // DK + h)),
            pl.BlockSpec((T, DV), lambda b, h: (rb + b, COL_V // DV + h)),
            pl.BlockSpec((T, DK), lambda b, h: (rb + b, COL_Q // DK + h)),
            pl.BlockSpec((T, DV), lambda b, h: (rb + b, COL_R // DV + h)),
            pl.BlockSpec((T, ZPAD), lambda b, h: (rb + b, 0)),
            pl.BlockSpec((128, DK), lambda b, h: (0, h)),
            pl.BlockSpec((1, DK), lambda b, h: (0, h)),
            pl.BlockSpec((128, DK), lambda b, h: (0, h)),
            pl.BlockSpec((1, DK), lambda b, h: (0, h)),
            pl.BlockSpec((1, DV), lambda b, h: (0, 0)),
            st_spec,
            st_spec,
        ],
        out_specs=(
            pl.BlockSpec((T, DV), lambda b, h: (b, h)),
            st_spec,
            st_spec,
        ),
        scratch_shapes=[pltpu.VMEM((T, DK), BF16)] * 6 + [
            pltpu.VMEM((T, DV), BF16),
            pltpu.VMEM((2, T // CHUNK, 8, DK), F32),
            pltpu.VMEM((T, DV), F32),
            pltpu.VMEM((DV, DK), F32),
            pltpu.VMEM((DV, DK), F32),
        ],
        compiler_params=_cp("arbitrary", "arbitrary"),
        name="gla_scan",
    )(proj, proj, proj, proj, zproj, wgf, bgf, wgb, bgb, onorm, s0f, s0b)


def _fnet_in_body(h_ref, w_ref, cs_ref, a_ref, b_ref):
    u = _dot(h_ref[...], w_ref[...]).astype(BF16)
    for g in range(FG):
        ab = _dot(u[:, g * FD:(g + 1) * FD], cs_ref[...])
        a_ref[:, g * FD:(g + 1) * FD] = ab[:, :FD].astype(BF16)
        b_ref[:, g * FD:(g + 1) * FD] = ab[:, FD:].astype(BF16)


def _fnet_in(h, w, cs, rows):
    tm = 512
    return pl.pallas_call(
        _fnet_in_body,
        out_shape=(jax.ShapeDtypeStruct((rows, D), BF16), jax.ShapeDtypeStruct((rows, D), BF16)),
        grid=(rows // tm,),
        in_specs=[
            pl.BlockSpec((tm, D), lambda i: (i, 0)),
            pl.BlockSpec((D, D), lambda i: (0, 0)),
            pl.BlockSpec((FD, 2 * FD), lambda i: (0, 0)),
        ],
        out_specs=(pl.BlockSpec((tm, D), lambda i: (i, 0)), pl.BlockSpec((tm, D), lambda i: (i, 0))),
        compiler_params=_cp("arbitrary"),
        name="fnet_in",
    )(h, w, cs)


def _fnet_time_body(ct_ref, st_ref, a_ref, b_ref, o_ref):
    o_ref[...] = (_dot(ct_ref[...], a_ref[...]) + _dot(st_ref[...], b_ref[...])).astype(o_ref.dtype)


def _fnet_time(ct, st, a, b, T, row_off):
    tm = min(T, 1024)
    tn = 1024
    rb = row_off // T
    return pl.pallas_call(
        _fnet_time_body,
        out_shape=jax.ShapeDtypeStruct((B * T, D), BF16),
        grid=(B, D // tn, T // tm),
        in_specs=[
            pl.BlockSpec((tm, T), lambda s, j, i: (i, 0)),
            pl.BlockSpec((tm, T), lambda s, j, i: (i, 0)),
            pl.BlockSpec((T, tn), lambda s, j, i: (rb + s, j)),
            pl.BlockSpec((T, tn), lambda s, j, i: (rb + s, j)),
        ],
        out_specs=pl.BlockSpec((tm, tn), lambda s, j, i: (s * (T // tm) + i, j)),
        compiler_params=_cp("arbitrary", "arbitrary", "arbitrary"),
        name="fnet_time",
    )(ct, st, a, b)


def _dft_mats(n, scale):
    idx = np.arange(n, dtype=np.int64)
    ang = 2.0 * np.pi * ((idx[:, None] * idx[None, :]) % n).astype(np.float64) / n
    return np.cos(ang) * scale, np.sin(ang) * scale


def _row_copy(src_hbm, row, dst_vmem, r, sem):
    return pltpu.make_async_copy(src_hbm.at[pl.ds(row, 1)], dst_vmem.at[pl.ds(r, 1)], sem)


GCH = 32


def _moe_body(bs_ref, nv_ref, tok_ref, h_hbm, wg_ref, wu_ref, wd_ref, y_hbm, xbuf, obuf, gsem, osem, wgb, wub, wdb,
              *, nb):
    e = pl.program_id(0)
    b0 = bs_ref[e]
    b1 = bs_ref[e + 1]
    total = bs_ref[ME]

    def gather(g):
        slot = g % 2
        base = g * TME
        for c in range(TME // GCH):
            @pl.when(c * GCH < nv_ref[g])
            def _():
                for r in range(c * GCH, (c + 1) * GCH):
                    _row_copy(h_hbm, tok_ref[base + r], xbuf.at[slot], r, gsem.at[slot]).start(priority=r % 2)

    def gather_wait(g):
        slot = g % 2
        for c in range(TME // GCH):
            @pl.when(c * GCH < nv_ref[g])
            def _():
                pltpu.make_async_copy(h_hbm.at[pl.ds(0, GCH)], xbuf.at[slot, pl.ds(c * GCH, GCH)],
                                      gsem.at[slot]).wait()

    def out_copy(g):
        slot = g % 2
        dst = y_hbm.at[pl.ds(pl.multiple_of(g * TME, TME), TME)]
        return pltpu.make_async_copy(obuf.at[slot], dst, osem.at[slot])

    @pl.when(e == 0)
    def _():
        xbuf[...] = jnp.zeros_like(xbuf)

        @pl.when(total > 0)
        def _():
            gather(0)

    @pl.when(b1 > b0)
    def _():
        wgb[...] = wg_ref[...].astype(BF16)
        wub[...] = wu_ref[...].astype(BF16)
        wdb[...] = wd_ref[...].astype(BF16)

    def block(g, carry):
        slot = g % 2

        @pl.when(g + 1 < total)
        def _():
            gather(g + 1)

        gather_wait(g)
        x_lo, x_hi = _unpack_rows(xbuf[slot])
        x = jnp.concatenate([x_lo.astype(BF16), x_hi.astype(BF16)], axis=1)
        hmid = _silu(_dot(x, wgb[...])) * _dot(x, wub[...])
        y = _pack_rows(_dot(hmid.astype(BF16), wdb[...]))

        @pl.when(g >= 2)
        def _():
            out_copy(g - 2).wait()

        obuf[slot] = y
        out_copy(g).start()
        return carry

    lax.fori_loop(b0, b1, block, 0)

    @pl.when(e == ME - 1)
    def _():
        @pl.when(total >= 2)
        def _():
            out_copy(total - 2).wait()

        @pl.when(total >= 1)
        def _():
            out_copy(total - 1).wait()

        def zero_copy(g):
            dst = y_hbm.at[pl.ds(pl.multiple_of(g * TME, TME), TME)]
            return pltpu.make_async_copy(obuf.at[0], dst, osem.at[0])

        obuf[0] = jnp.zeros((TME, HALF), jnp.uint32)
        lax.fori_loop(total, nb, lambda g, c: (zero_copy(g).start(), c)[1], 0)
        lax.fori_loop(total, nb, lambda g, c: (zero_copy(g).wait(), c)[1], 0)


def _moe_experts(layer, blk_start, blk_rows, buf_tok, h2, w_gate, w_up, w_down):
    nb = buf_tok.shape[0] // TME
    w_in_spec = pl.BlockSpec((None, None, D, MDE), lambda e, bs, nv, tk: (layer, e, 0, 0))
    return pl.pallas_call(
        functools.partial(_moe_body, nb=nb),
        out_shape=jax.ShapeDtypeStruct((nb * TME, HALF), jnp.uint32),
        grid_spec=pltpu.PrefetchScalarGridSpec(
            num_scalar_prefetch=3,
            grid=(ME,),
            in_specs=[
                pl.BlockSpec(memory_space=pl.ANY),
                w_in_spec,
                w_in_spec,
                pl.BlockSpec((None, None, MDE, D), lambda e, bs, nv, tk: (layer, e, 0, 0)),
            ],
            out_specs=pl.BlockSpec(memory_space=pl.ANY),
            scratch_shapes=[
                pltpu.VMEM((2, TME, HALF), jnp.uint32),
                pltpu.VMEM((2, TME, HALF), jnp.uint32),
                pltpu.SemaphoreType.DMA((2,)),
                pltpu.SemaphoreType.DMA((2,)),
                pltpu.VMEM((D, MDE), BF16),
                pltpu.VMEM((D, MDE), BF16),
                pltpu.VMEM((MDE, D), BF16),
            ],
        ),
        compiler_params=_cp("arbitrary"),
        name="moe_experts",
    )(blk_start, blk_rows, buf_tok, h2, w_gate, w_up, w_down)


TMC = 128


def _combine_body(dest_ref, x_ref, gate_ref, wt_ref, g_ref, sh_ref, sc_ref, y_hbm, *rest, nt, last):
    if last:
        h_ref, ybuf, sem = rest
    else:
        o_ref, h_ref, ybuf, sem = rest
    i = pl.program_id(0)

    def gather(tile, slot):
        base = tile * (2 * TMC)

        for r in range(TMC):
            _row_copy(y_hbm, dest_ref[base + 2 * r], ybuf.at[slot, 0], r, sem.at[slot]).start(priority=0)
            _row_copy(y_hbm, dest_ref[base + 2 * r + 1], ybuf.at[slot, 1], r, sem.at[slot]).start(priority=1)

    @pl.when(i == 0)
    def _():
        gather(0, 0)

    @pl.when(i + 1 < nt)
    def _():
        gather(i + 1, (i + 1) % 2)

    slot = i % 2
    for k in range(2):
        pltpu.make_async_copy(y_hbm.at[pl.ds(0, TMC)], ybuf.at[slot, k], sem.at[slot]).wait()
    wt = wt_ref[...]
    w0, w1 = wt[:, 0:1], wt[:, 1:2]
    lo0, hi0 = _unpack_rows(ybuf[slot, 0])
    lo1, hi1 = _unpack_rows(ybuf[slot, 1])
    y = jnp.concatenate([w0 * lo0 + w1 * lo1, w0 * hi0 + w1 * hi1], axis=1)
    xn = x_ref[...] + gate_ref[...] * y
    if last:
        h_ref[...] = xn * lax.rsqrt(jnp.mean(xn * xn, axis=-1, keepdims=True) + EPS) * g_ref[...]
    else:
        o_ref[...] = xn
        h_ref[...] = _norm_mod_val(xn, g_ref[...], sh_ref[...], sc_ref[...]).astype(h_ref.dtype)


def _combine(dest, xs, mod3, wt, yb, rows, g_next, mod3_next, last):
    nt = rows // TMC
    row_spec = pl.BlockSpec((TMC, D), lambda i, d: (i, 0))

    def mod_spec(which):
        return pl.BlockSpec((None, 1, D), lambda i, d: (_seg_of(i, TMC) * N_MOD + which, 0, 0))

    if last:
        out_shape = jax.ShapeDtypeStruct((rows, D), F32)
        out_specs = row_spec
    else:
        out_shape = (jax.ShapeDtypeStruct((rows, D), F32), jax.ShapeDtypeStruct((rows, D), BF16))
        out_specs = (row_spec, row_spec)
    return pl.pallas_call(
        functools.partial(_combine_body, nt=nt, last=last),
        out_shape=out_shape,
        grid_spec=pltpu.PrefetchScalarGridSpec(
            num_scalar_prefetch=1,
            grid=(nt,),
            in_specs=[
                row_spec,
                mod_spec(5),
                pl.BlockSpec((TMC, 128), lambda i, d: (i, 0)),
                pl.BlockSpec((1, D), lambda i, d: (0, 0)),
                mod_spec(0),
                mod_spec(1),
                pl.BlockSpec(memory_space=pl.ANY),
            ],
            out_specs=out_specs,
            scratch_shapes=[
                pltpu.VMEM((2, 2, TMC, HALF), jnp.uint32),
                pltpu.SemaphoreType.DMA((2,)),
            ],
        ),
        compiler_params=_cp("arbitrary"),
        name="moe_combine",
    )(dest, xs, mod3, wt, g_next.reshape(1, D), mod3_next, mod3_next, yb)


PLAN_CHUNK = 1024


def _dest_body(ids_ref, base_ref, o_ref):
    ids = ids_ref[...].astype(F32)
    lane_i = lax.broadcasted_iota(jnp.int32, ids.shape, 1)
    lane = lane_i.astype(F32)

    def pick(k):
        return jnp.sum(jnp.where(lane_i == k, ids, 0.0), axis=-1, keepdims=True)

    base = base_ref[...]
    d0 = jnp.sum(jnp.where(lane == pick(0), base[0:1, :], 0.0), axis=-1, keepdims=True) + pick(2)
    d1 = jnp.sum(jnp.where(lane == pick(1), base[1:2, :], 0.0), axis=-1, keepdims=True) + pick(3)
    o_ref[...] = jnp.where(lane_i == 0, d0, jnp.where(lane_i == 1, d1, 0.0)).astype(jnp.int32)


def _slot_body(dest_ref, tok_ref, *, nslots):
    i = pl.program_id(0)

    @pl.when(i == 0)
    def _():
        def clear(s, carry):
            tok_ref[s] = s & (LAT - 1)
            return carry

        lax.fori_loop(0, nslots, clear, 0, unroll=32)

    def place(t, carry):
        tok = i * PLAN_CHUNK + t
        tok_ref[dest_ref[0, 2 * t]] = tok
        tok_ref[dest_ref[0, 2 * t + 1]] = tok
        return carry

    lax.fori_loop(0, PLAN_CHUNK, place, 0, unroll=16)


def _route_plan(eid, cnt, rows):
    a = rows * 2
    nb = -(-a // TME) + ME
    steps = rows // PLAN_CHUNK
    c0 = cnt[0, :ME].astype(jnp.int32)
    c1 = cnt[1, :ME].astype(jnp.int32)
    padded = ((c0 + c1 + TME - 1) // TME) * TME
    pad_end = jnp.cumsum(padded)
    pad_start = pad_end - padded
    blk_start = jnp.concatenate([jnp.zeros((1,), jnp.int32), (pad_end // TME).astype(jnp.int32)])
    blk = jnp.arange(nb, dtype=jnp.int32)
    blk_e = jnp.minimum(jnp.searchsorted(blk_start[1:], blk, side='right'), ME - 1)
    blk_rows = jnp.clip((c0 + c1)[blk_e] - (blk - blk_start[blk_e]) * TME, 0, TME).astype(jnp.int32)
    base = jnp.zeros((8, 128), F32).at[0, :ME].set(pad_start.astype(F32)).at[1, :ME].set((pad_start + c0).astype(F32))
    dest_t = pl.pallas_call(
        _dest_body,
        out_shape=jax.ShapeDtypeStruct((rows, 128), jnp.int32),
        grid=(steps,),
        in_specs=[pl.BlockSpec((PLAN_CHUNK, 128), lambda i: (i, 0)), pl.BlockSpec((8, 128), lambda i: (0, 0))],
        out_specs=pl.BlockSpec((PLAN_CHUNK, 128), lambda i: (i, 0)),
        compiler_params=_cp("arbitrary"),
        name="route_dest",
    )(eid, base)
    dest = dest_t[:, :2].reshape(a)
    buf_tok = pl.pallas_call(
        functools.partial(_slot_body, nslots=nb * TME),
        out_shape=jax.ShapeDtypeStruct((nb * TME,), jnp.int32),
        grid=(steps,),
        in_specs=[pl.BlockSpec((None, 1, 2 * PLAN_CHUNK), lambda i: (i, 0, 0), memory_space=pltpu.SMEM)],
        out_specs=pl.BlockSpec(memory_space=pltpu.SMEM),
        compiler_params=_cp("arbitrary"),
        name="route_slots",
    )(dest.reshape(steps, 1, 2 * PLAN_CHUNK))
    return dest, buf_tok, blk_start, blk_rows


def _sincos_2d(rows, cols, d):
    quarter = d // 4
    omega = 1.0 / (POS_BASE ** (jnp.arange(quarter, dtype=F32) / quarter))

    def axis_emb(n):
        p = jnp.arange(n, dtype=F32)[:, None] * omega[None, :]
        return jnp.concatenate([jnp.sin(p), jnp.cos(p)], axis=-1)

    er, ec = axis_emb(rows), axis_emb(cols)
    half = 2 * quarter
    pos = jnp.concatenate([jnp.broadcast_to(er[:, None, :], (rows, cols, half)),
                           jnp.broadcast_to(ec[None, :, :], (rows, cols, half))], axis=-1)
    return pos.reshape(rows * cols, 2 * half)


def _gla_weights(w_in, wg_f, wg_b):
    w_cat = jnp.concatenate([w_in[:, :QK + VV], w_in[:, STATE_COLS:]], axis=1).astype(BF16)
    w_z = jnp.concatenate([w_in[:, QK + VV:STATE_COLS], jnp.zeros((D, ZPAD - 2 * RANK), F32)], axis=1).astype(BF16)
    wgf = jnp.zeros((128, QK), F32).at[:RANK].set(wg_f)
    wgb = jnp.zeros((128, QK), F32).at[RANK:2 * RANK].set(wg_b)
    return w_cat, w_z, wgf, wgb


def kernel(x, c, ctx, c_ctx, ada_w, ada_b, norm1_g, norm2_g, gla_w_in, gla_wg_f, gla_bg_f, gla_wg_b, gla_bg_b,
           gla_onorm_g, gla_w_out, fnet_w_in, fnet_w_out, moe_rw_group, moe_rb_group, moe_rw_expert,
           moe_rb_expert, moe_w_gate, moe_w_up, moe_w_down, final_g):
    cvec = jnp.zeros((SEGS, D), F32).at[:B].set(c).at[CTX_SEG].set(c_ctx)
    mods = _mod_all(cvec, ada_w, ada_b)
    pos = _sincos_2d(SEQ // GRID_W, GRID_W, D)
    mod3s = [mods[i].reshape(SEGS * N_MOD, 1, D) for i in range(DEPTH)]
    xs, h = _assemble(x.reshape(LAT, D), pos, ctx.reshape(NCTX, D), norm1_g[0], mod3s[0])

    cc, sc = _dft_mats(FD, 1.0)
    cs = jnp.asarray(np.concatenate([cc, sc], axis=1), BF16)
    ct_l, st_l = _dft_mats(SEQ, (SEQ * FD) ** -0.5)
    ct_c, st_c = _dft_mats(CTX, (CTX * FD) ** -0.5)
    ct_l, st_l = jnp.asarray(ct_l, BF16), jnp.asarray(-st_l, BF16)
    ct_c, st_c = jnp.asarray(ct_c, BF16), jnp.asarray(-st_c, BF16)
    s_zero = jnp.zeros((B, H, DV, DK), F32)
    last_reader = ((DEPTH - 1) // 2) * 2

    for i in range(DEPTH):
        kind, j = i % 2, i // 2
        ctx_live = i < last_reader
        ctx_needed = i <= last_reader
        mod3 = mod3s[i]
        rows_in = ROWS if ctx_needed else LAT
        rows_out = ROWS if ctx_live else LAT

        if kind == 0:
            w_cat, w_z, wgf, wgb = _gla_weights(gla_w_in[j], gla_wg_f[j], gla_wg_b[j])
            bgf, bgb = gla_bg_f[j].reshape(1, QK), gla_bg_b[j].reshape(1, QK)
            onorm = gla_onorm_g[j].reshape(1, DV)
            proj = _mm(h, w_cat, rows_in, 1024, 1536, BF16, "gla_proj")
            zproj = _mm(h, w_z, rows_in, 512, ZPAD, F32, "gla_gate_proj")
            o_c, sf, sb = _gla(proj, zproj, wgf, bgf, wgb, bgb, onorm, s_zero, s_zero, CTX, LAT, ctx_live)
            o_l, _, _ = _gla(proj, zproj, wgf, bgf, wgb, bgb, onorm, sf, sb, SEQ, 0, True)
            mix_l, mix_c, w_out = o_l, o_c, gla_w_out[j].astype(BF16)
        else:
            a, b = _fnet_in(h, fnet_w_in[j].astype(BF16), cs, rows_in)
            y = _fnet_time(ct_l, st_l, a, b, SEQ, 0)
            y_c = _fnet_time(ct_c, st_c, a, b, CTX, LAT) if ctx_live else y
            mix_l, mix_c, w_out = y, y_c, fnet_w_out[j].astype(BF16)

        rw = jnp.zeros((D, 128), F32).at[:, :MG].set(moe_rw_group[i])
        rw = rw.at[:, MG:MG + ME].set(jnp.transpose(moe_rw_expert[i], (1, 0, 2)).reshape(D, ME))
        rb = jnp.zeros((1, 128), F32).at[0, :MG].set(moe_rb_group[i]).at[0, MG:MG + ME].set(
            moe_rb_expert[i].reshape(ME))
        rw_hi = rw.astype(BF16)
        rw_cat = jnp.concatenate([rw_hi, (rw - rw_hi.astype(F32)).astype(BF16)], axis=1)
        xs, h2, eid, wt, cnt = _mix_route(mix_l, mix_c, w_out, xs, norm2_g[i], mod3, rw_cat, rb, rows_out)
        dest, buf_tok, blk_start, blk_rows = _route_plan(eid, cnt, rows_out)
        yb = _moe_experts(i, blk_start, blk_rows, buf_tok, h2, moe_w_gate, moe_w_up, moe_w_down)
        if i + 1 < DEPTH:
            xs, h = _combine(dest, xs, mod3, wt, yb, rows_out, norm1_g[i + 1], mod3s[i + 1], False)
        else:
            out = _combine(dest, xs, mod3, wt, yb, rows_out, final_g, mod3, True)

    return out.reshape(B, SEQ, D)
```

```python
import functools

import jax
import jax.numpy as jnp
import numpy as np
from jax import lax
from jax.experimental import pallas as pl
from jax.experimental.pallas import tpu as pltpu

F32 = jnp.float32
BF16 = jnp.bfloat16

D = 2048
B = 4
SEQ = 2048
CTX = 256
DEPTH = 4
GRID_W = 64
EPS = 1e-6
POS_BASE = 10000.0
N_MOD = 6

H = 4
DK = 256
DV = 512
QK = H * DK
VV = H * DV
RANK = 16
GATE_NORM = 16.0
CHUNK = 64
STATE_COLS = QK + VV + 2 * RANK
ZPAD = 256
PROJ_N = QK + VV + QK + VV
COL_K, COL_V, COL_Q, COL_R = 0, QK, QK + VV, 2 * QK + VV

FG = 4
FD = D // FG

MG = 4
MPG = 8
ME = MG * MPG
MDE = D // 4
TME = 256

LAT = B * SEQ
NCTX = B * CTX
ROWS = LAT + NCTX
SEGS = 8
CTX_SEG = B

VMEM_LIMIT = 56 * 1024 * 1024


def _cp(*sem):
    return pltpu.CompilerParams(dimension_semantics=sem, vmem_limit_bytes=VMEM_LIMIT)


def _seg_of(i, tm):
    return jnp.where(i < LAT // tm, i // (SEQ // tm), CTX_SEG)


def _split2(a):
    hi = a.astype(BF16)
    lo = (a - hi.astype(F32)).astype(BF16)
    return hi, lo


def _dot(a, b):
    return jnp.dot(a, b, preferred_element_type=F32)


def _dot3(a, b_hi, b_lo):
    a_hi, a_lo = _split2(a)
    return _dot(a_hi, b_hi) + _dot(a_lo, b_hi) + _dot(a_hi, b_lo)


def _silu(a):
    return a * jax.nn.sigmoid(a)


HALF = D // 2


def _pack_rows(a):
    lo = lax.bitcast_convert_type(a[:, :HALF].astype(BF16).astype(F32), jnp.uint32)
    hi = lax.bitcast_convert_type(a[:, HALF:].astype(BF16).astype(F32), jnp.uint32)
    return (hi & jnp.uint32(0xFFFF0000)) | (lo >> 16)


def _unpack_rows(w):
    lo = lax.bitcast_convert_type(w << 16, F32)
    hi = lax.bitcast_convert_type(w & jnp.uint32(0xFFFF0000), F32)
    return lo, hi


def _mod_body(c_ref, w_ref, b_ref, o_ref):
    s = _silu(c_ref[...])
    w_hi, w_lo = _split2(w_ref[0])
    o_ref[0] = _dot3(s, w_hi, w_lo) + b_ref[0]


def _mod_all(cvec, ada_w, ada_b):
    tn = 1024
    return pl.pallas_call(
        _mod_body,
        out_shape=jax.ShapeDtypeStruct((DEPTH, SEGS, N_MOD * D), F32),
        grid=(DEPTH, N_MOD * D // tn),
        in_specs=[
            pl.BlockSpec((SEGS, D), lambda l, j: (0, 0)),
            pl.BlockSpec((1, D, tn), lambda l, j: (l, 0, j)),
            pl.BlockSpec((1, 1, tn), lambda l, j: (l, 0, j)),
        ],
        out_specs=pl.BlockSpec((1, SEGS, tn), lambda l, j: (l, 0, j)),
        compiler_params=_cp("arbitrary", "arbitrary"),
        name="adaln_mod",
    )(cvec, ada_w, ada_b.reshape(DEPTH, 1, N_MOD * D))


def _assemble_body(x_ref, pos_ref, ctx_ref, g_ref, sh_ref, sc_ref, o_ref, h_ref, *, nlat):
    i = pl.program_id(0)

    @pl.when(i < nlat)
    def _():
        o_ref[...] = x_ref[...] + pos_ref[...]

    @pl.when(i >= nlat)
    def _():
        o_ref[...] = ctx_ref[...]

    h_ref[...] = _norm_mod_val(o_ref[...], g_ref[...], sh_ref[...], sc_ref[...]).astype(h_ref.dtype)


def _assemble(x2, pos, ctx2, g, mod3):
    tm = 256
    nlat = LAT // tm
    return pl.pallas_call(
        functools.partial(_assemble_body, nlat=nlat),
        out_shape=(jax.ShapeDtypeStruct((ROWS, D), F32), jax.ShapeDtypeStruct((ROWS, D), BF16)),
        grid=(ROWS // tm,),
        in_specs=[
            pl.BlockSpec((tm, D), lambda i: (jnp.minimum(i, nlat - 1), 0)),
            pl.BlockSpec((tm, D), lambda i: (i % (SEQ // tm), 0)),
            pl.BlockSpec((tm, D), lambda i: (jnp.maximum(i - nlat, 0), 0)),
            pl.BlockSpec((1, D), lambda i: (0, 0)),
            _mod_spec(0, tm),
            _mod_spec(1, tm),
        ],
        out_specs=(pl.BlockSpec((tm, D), lambda i: (i, 0)), pl.BlockSpec((tm, D), lambda i: (i, 0))),
        compiler_params=_cp("arbitrary"),
        name="assemble_stream",
    )(x2, pos, ctx2, g.reshape(1, D), mod3, mod3)


def _norm_mod_val(x, g, sh, sc):
    y = x * lax.rsqrt(jnp.mean(x * x, axis=-1, keepdims=True) + EPS) * g
    return y * (1.0 + sc) + sh


def _mod_spec(which, tm):
    return pl.BlockSpec((None, 1, D), lambda i: (_seg_of(i, tm) * N_MOD + which, 0, 0))


def _route_tile(h2, rwc_ref, rb_ref, run_s):
    h_hi, h_lo = _split2(h2)
    p = _dot(h_hi, rwc_ref[...])
    lg = p[:, :128] + p[:, 128:] + _dot(h_lo, rwc_ref[:, :128]) + rb_ref[...]
    lane_i = lax.broadcasted_iota(jnp.int32, lg.shape, 1)
    lane = lane_i.astype(F32)
    neg = jnp.float32(-jnp.inf)
    big = jnp.float32(1024.0)

    glog = jnp.where(lane < MG, lg, neg)
    gmax = jnp.max(glog, axis=-1, keepdims=True)
    gidx = jnp.min(jnp.where(glog == gmax, lane, big), axis=-1, keepdims=True)
    g_w = 1.0 / jnp.sum(jnp.exp(glog - gmax), axis=-1, keepdims=True)

    lo = MG + MPG * gidx
    el = jnp.where((lane >= lo) & (lane < lo + MPG), lg, neg)
    m1 = jnp.max(el, axis=-1, keepdims=True)
    i1 = jnp.min(jnp.where(el == m1, lane, big), axis=-1, keepdims=True)
    el2 = jnp.where(lane == i1, neg, el)
    m2 = jnp.max(el2, axis=-1, keepdims=True)
    i2 = jnp.min(jnp.where(el2 == m2, lane, big), axis=-1, keepdims=True)
    e2 = jnp.exp(m2 - m1)
    den = 1.0 / (1.0 + e2)
    w1 = g_w * den
    w2 = g_w * (e2 * den)
    e_a = i1 - MG
    e_b = i2 - MG

    tm = lg.shape[0]
    r_i = lax.broadcasted_iota(jnp.int32, (tm, tm), 0)
    c_i = lax.broadcasted_iota(jnp.int32, (tm, tm), 1)
    earlier = jnp.where(r_i > c_i, 1.0, 0.0).astype(BF16)
    oh_a = jnp.where(lane == e_a, 1.0, 0.0)
    oh_b = jnp.where(lane == e_b, 1.0, 0.0)
    run = run_s[...]
    rank_a = jnp.sum(oh_a * (_dot(earlier, oh_a.astype(BF16)) + run[0:1, :]), axis=-1, keepdims=True)
    rank_b = jnp.sum(oh_b * (_dot(earlier, oh_b.astype(BF16)) + run[1:2, :]), axis=-1, keepdims=True)
    run_s[0:1, :] = run[0:1, :] + jnp.sum(oh_a, axis=0, keepdims=True)
    run_s[1:2, :] = run[1:2, :] + jnp.sum(oh_b, axis=0, keepdims=True)

    ids = jnp.where(lane_i == 0, e_a, jnp.where(lane_i == 1, e_b, jnp.where(lane_i == 2, rank_a,
                                                                           jnp.where(lane_i == 3, rank_b, 0.0))))
    wts = jnp.where(lane_i == 0, w1, jnp.where(lane_i == 1, w2, 0.0))
    return ids.astype(jnp.int32), wts


def _mix_route_body(al_ref, ac_ref, w_ref, res_ref, gate_ref, g_ref, sh_ref, sc_ref, rwc_ref, rb_ref,
                    o_ref, h_ref, eid_ref, wt_ref, cnt_ref, acc_s, run_s, *, nlat):
    i = pl.program_id(0)

    @pl.when(i == 0)
    def _():
        run_s[...] = jnp.zeros_like(run_s)

    @pl.when(i < nlat)
    def _():
        acc_s[...] = _dot(al_ref[...], w_ref[...])

    @pl.when(i >= nlat)
    def _():
        acc_s[...] = _dot(ac_ref[...], w_ref[...])

    xn = res_ref[...] + gate_ref[...] * acc_s[...]
    o_ref[...] = xn
    h2 = _norm_mod_val(xn, g_ref[...], sh_ref[...], sc_ref[...])
    h_ref[...] = _pack_rows(h2)
    ids, wts = _route_tile(h2, rwc_ref, rb_ref, run_s)
    eid_ref[...] = ids
    wt_ref[...] = wts
    cnt_ref[...] = run_s[...]


def _mix_route(a_lat, a_ctx, w, xs, g, mod3, rw_cat, rb, rows):
    tm = 256
    k = a_lat.shape[1]
    nlat = LAT // tm
    return pl.pallas_call(
        functools.partial(_mix_route_body, nlat=nlat),
        out_shape=(
            jax.ShapeDtypeStruct((rows, D), F32),
            jax.ShapeDtypeStruct((rows, HALF), jnp.uint32),
            jax.ShapeDtypeStruct((rows, 128), jnp.int32),
            jax.ShapeDtypeStruct((rows, 128), F32),
            jax.ShapeDtypeStruct((8, 128), F32),
        ),
        grid=(rows // tm,),
        in_specs=[
            pl.BlockSpec((tm, k), lambda i: (jnp.minimum(i, nlat - 1), 0)),
            pl.BlockSpec((tm, k), lambda i: (jnp.maximum(i - nlat, 0), 0)),
            pl.BlockSpec((k, D), lambda i: (0, 0)),
            pl.BlockSpec((tm, D), lambda i: (i, 0)),
            _mod_spec(2, tm),
            pl.BlockSpec((1, D), lambda i: (0, 0)),
            _mod_spec(3, tm),
            _mod_spec(4, tm),
            pl.BlockSpec((D, 256), lambda i: (0, 0)),
            pl.BlockSpec((1, 128), lambda i: (0, 0)),
        ],
        out_specs=(
            pl.BlockSpec((tm, D), lambda i: (i, 0)),
            pl.BlockSpec((tm, HALF), lambda i: (i, 0)),
            pl.BlockSpec((tm, 128), lambda i: (i, 0)),
            pl.BlockSpec((tm, 128), lambda i: (i, 0)),
            pl.BlockSpec((8, 128), lambda i: (0, 0)),
        ),
        scratch_shapes=[pltpu.VMEM((tm, D), F32), pltpu.VMEM((8, 128), F32)],
        compiler_params=_cp("arbitrary"),
        name="mix_route",
    )(a_lat, a_ctx, w, xs, mod3, g.reshape(1, D), mod3, mod3, rw_cat, rb)


def _mm_body(x_ref, w_ref, o_ref):
    o_ref[...] = _dot(x_ref[...], w_ref[...]).astype(o_ref.dtype)


def _mm(x, w, rows, tm, tn, out_dtype, name):
    k = x.shape[1]
    n = w.shape[1]
    return pl.pallas_call(
        _mm_body,
        out_shape=jax.ShapeDtypeStruct((rows, n), out_dtype),
        grid=(n // tn, rows // tm),
        in_specs=[
            pl.BlockSpec((tm, k), lambda j, i: (i, 0)),
            pl.BlockSpec((k, tn), lambda j, i: (0, j)),
        ],
        out_specs=pl.BlockSpec((tm, tn), lambda j, i: (i, j)),
        compiler_params=_cp("arbitrary", "arbitrary"),
        name=name,
    )(x, w)


def _log_sigmoid(a):
    return jnp.minimum(a, 0.0) - jnp.log(1.0 + jnp.exp(-jnp.abs(a)))


PREP = 256


def _gla_body(k_ref, v_ref, q_ref, r_ref, z_ref, wgf_ref, bgf_ref, wgb_ref, bgb_ref, on_ref, s0f_ref, s0b_ref,
              o_ref, sf_ref, sb_ref,
              qf_s, kf_s, df_s, qb_s, kb_s, db_s, vb_s, dec_s, o_s, stf_s, stb_s, *, T, emit_o):
    nc = T // CHUNK
    nblk = T // PREP
    cpb = PREP // CHUNK
    scale = DK ** -0.5

    row = lax.broadcasted_iota(jnp.int32, (PREP, PREP), 0)
    col = lax.broadcasted_iota(jnp.int32, (PREP, PREP), 1)
    shift = CHUNK.bit_length() - 1
    same = lax.shift_right_logical(row, shift) == lax.shift_right_logical(col, shift)
    tri_f = jnp.where(same & (row >= col), 1.0, 0.0).astype(BF16)
    tri_b = jnp.where(same & (row <= col), 1.0, 0.0).astype(BF16)
    wf_hi, wf_lo = _split2(wgf_ref[...])
    wb_hi, wb_lo = _split2(wgb_ref[...])

    def chunk_sums(tri, g):
        g1, g2 = _split2(g)
        return _dot(tri, g1) + _dot(tri, g2)

    def edge_rows(G, e):
        return jnp.concatenate(
            [jnp.broadcast_to(G[c * CHUNK + e:c * CHUNK + e + 1, :], (CHUNK, DK)) for c in range(cpb)], axis=0)

    def prep(blk, carry):
        rows = pl.ds(pl.multiple_of(blk * PREP, PREP), PREP)
        z = z_ref[rows, :128]
        gf = _log_sigmoid(_dot3(z, wf_hi, wf_lo) + bgf_ref[...]) / GATE_NORM
        gb = _log_sigmoid(_dot3(z, wb_hi, wb_lo) + bgb_ref[...]) / GATE_NORM
        Gf = chunk_sums(tri_f, gf)
        Gb = chunk_sums(tri_b, gb)
        k = k_ref[rows, :].astype(F32)
        q = q_ref[rows, :].astype(F32) * scale
        qf_s[rows, :] = (q * jnp.exp(Gf)).astype(BF16)
        kf_s[rows, :] = (k * jnp.exp(-Gf)).astype(BF16)
        df_s[rows, :] = (k * jnp.exp(edge_rows(Gf, CHUNK - 1) - Gf)).astype(BF16)
        qb_s[rows, :] = (q * jnp.exp(Gb)).astype(BF16)
        kb_s[rows, :] = (k * jnp.exp(-Gb)).astype(BF16)
        db_s[rows, :] = (k * jnp.exp(edge_rows(Gb, 0) - Gb)).astype(BF16)
        vb_s[rows, :] = v_ref[rows, :]
        for c in range(cpb):
            ef = Gf[c * CHUNK + CHUNK - 1:c * CHUNK + CHUNK, :]
            eb = Gb[c * CHUNK:c * CHUNK + 1, :]
            dec_s[0, blk * cpb + c] = jnp.broadcast_to(jnp.exp(ef), (8, DK))
            dec_s[1, blk * cpb + c] = jnp.broadcast_to(jnp.exp(eb), (8, DK))
        return carry

    lax.fori_loop(0, nblk, prep, 0)

    stf_s[...] = s0f_ref[0, 0]
    stb_s[...] = s0b_ref[0, 0]
    r64 = lax.broadcasted_iota(jnp.int32, (CHUNK, CHUNK), 0)
    c64 = lax.broadcasted_iota(jnp.int32, (CHUNK, CHUNK), 1)
    nt_dims = (((1,), (1,)), ((), ()))
    tn_dims = (((0,), (0,)), ((), ()))

    def chunk(c, q_s, k_s, d_s, st_s, di, keep):
        rows = pl.ds(pl.multiple_of(c * CHUNK, CHUNK), CHUNK)
        qe = q_s[rows, :]
        vb = vb_s[rows, :]
        st = st_s[...]
        o = None
        if emit_o:
            a = lax.dot_general(qe, k_s[rows, :], nt_dims, preferred_element_type=F32)
            a = jnp.where(keep, a, 0.0).astype(BF16)
            o = _dot(a, vb) + lax.dot_general(qe, st.astype(BF16), nt_dims, preferred_element_type=F32)
        upd = lax.dot_general(vb, d_s[rows, :], tn_dims, preferred_element_type=F32)
        st_s[...] = st * dec_s[di, c][0:1, :] + upd
        return o, rows

    def step(ci, carry, accumulate):
        of, rows_f = chunk(ci, qf_s, kf_s, df_s, stf_s, 0, r64 >= c64)
        ob, rows_b = chunk(nc - 1 - ci, qb_s, kb_s, db_s, stb_s, 1, r64 <= c64)
        if emit_o:
            if accumulate:
                o_s[rows_f, :] = o_s[rows_f, :] + of
                o_s[rows_b, :] = o_s[rows_b, :] + ob
            else:
                o_s[rows_f, :] = of
                o_s[rows_b, :] = ob
        return carry

    lax.fori_loop(0, nc // 2, functools.partial(step, accumulate=False), 0, unroll=min(8, nc // 2))
    lax.fori_loop(nc // 2, nc, functools.partial(step, accumulate=True), 0, unroll=min(8, nc // 2))
    sf_ref[0, 0] = stf_s[...]
    sb_ref[0, 0] = stb_s[...]

    if emit_o:
        def fin(blk, carry):
            rows = pl.ds(pl.multiple_of(blk * PREP, PREP), PREP)
            tot = o_s[rows, :]
            y = tot * lax.rsqrt(jnp.mean(tot * tot, axis=-1, keepdims=True) + EPS) * on_ref[...]
            o_ref[rows, :] = (y * _silu(r_ref[rows, :].astype(F32))).astype(o_ref.dtype)
            return carry

        lax.fori_loop(0, nblk, fin, 0)
    else:
        o_ref[...] = jnp.zeros_like(o_ref)


def _gla(proj, zproj, wgf, bgf, wgb, bgb, onorm, s0f, s0b, T, row_off, emit_o):
    rb = row_off // T
    st_spec = pl.BlockSpec((1, 1, DV, DK), lambda b, h: (b, h, 0, 0))
    return pl.pallas_call(
        functools.partial(_gla_body, T=T, emit_o=emit_o),
        out_shape=(
            jax.ShapeDtypeStruct((B * T, VV), BF16),
            jax.ShapeDtypeStruct((B, H, DV, DK), F32),
            jax.ShapeDtypeStruct((B, H, DV, DK), F32),
        ),
        grid=(B, H),
        in_specs=[
            pl.BlockSpec((T, DK), lambda b, h: (rb + b, COL_K // DK + h)),
            pl.BlockSpec((T, DV), lambda b, h: (rb + b, COL_V // DV + h)),
            pl.BlockSpec((T, DK), lambda b, h: (rb + b, COL_Q // DK + h)),
            pl.BlockSpec((T, DV), lambda b, h: (rb + b, COL_R // DV + h)),
            pl.BlockSpec((T, ZPAD), lambda b, h: (rb + b, 0)),
            pl.BlockSpec((128, DK), lambda b, h: (0, h)),
            pl.BlockSpec((1, DK), lambda b, h: (0, h)),
            pl.BlockSpec((128, DK), lambda b, h: (0, h)),
            pl.BlockSpec((1, DK), lambda b, h: (0, h)),
            pl.BlockSpec((1, DV), lambda b, h: (0, 0)),
            st_spec,
            st_spec,
        ],
        out_specs=(
            pl.BlockSpec((T, DV), lambda b, h: (b, h)),
            st_spec,
            st_spec,
        ),
        scratch_shapes=[pltpu.VMEM((T, DK), BF16)] * 6 + [
            pltpu.VMEM((T, DV), BF16),
            pltpu.VMEM((2, T // CHUNK, 8, DK), F32),
            pltpu.VMEM((T, DV), F32),
            pltpu.VMEM((DV, DK), F32),
            pltpu.VMEM((DV, DK), F32),
        ],
        compiler_params=_cp("arbitrary", "arbitrary"),
        name="gla_scan",
    )(proj, proj, proj, proj, zproj, wgf, bgf, wgb, bgb, onorm, s0f, s0b)


def _fnet_in_body(h_ref, w_ref, cs_ref, a_ref, b_ref):
    u = _dot(h_ref[...], w_ref[...]).astype(BF16)
    for g in range(FG):
        ab = _dot(u[:, g * FD:(g + 1) * FD], cs_ref[...])
        a_ref[:, g * FD:(g + 1) * FD] = ab[:, :FD].astype(BF16)
        b_ref[:, g * FD:(g + 1) * FD] = ab[:, FD:].astype(BF16)


def _fnet_in(h, w, cs, rows):
    tm = 512
    return pl.pallas_call(
        _fnet_in_body,
        out_shape=(jax.ShapeDtypeStruct((rows, D), BF16), jax.ShapeDtypeStruct((rows, D), BF16)),
        grid=(rows // tm,),
        in_specs=[
            pl.BlockSpec((tm, D), lambda i: (i, 0)),
            pl.BlockSpec((D, D), lambda i: (0, 0)),
            pl.BlockSpec((FD, 2 * FD), lambda i: (0, 0)),
        ],
        out_specs=(pl.BlockSpec((tm, D), lambda i: (i, 0)), pl.BlockSpec((tm, D), lambda i: (i, 0))),
        compiler_params=_cp("arbitrary"),
        name="fnet_in",
    )(h, w, cs)


def _fnet_time_body(ct_ref, st_ref, a_ref, b_ref, o_ref):
    o_ref[...] = (_dot(ct_ref[...], a_ref[...]) + _dot(st_ref[...], b_ref[...])).astype(o_ref.dtype)


def _fnet_time(ct, st, a, b, T, row_off):
    tm = min(T, 1024)
    tn = 1024
    rb = row_off // T
    return pl.pallas_call(
        _fnet_time_body,
        out_shape=jax.ShapeDtypeStruct((B * T, D), BF16),
        grid=(B, D // tn, T // tm),
        in_specs=[
            pl.BlockSpec((tm, T), lambda s, j, i: (i, 0)),
            pl.BlockSpec((tm, T), lambda s, j, i: (i, 0)),
            pl.BlockSpec((T, tn), lambda s, j, i: (rb + s, j)),
            pl.BlockSpec((T, tn), lambda s, j, i: (rb + s, j)),
        ],
        out_specs=pl.BlockSpec((tm, tn), lambda s, j, i: (s * (T // tm) + i, j)),
        compiler_params=_cp("arbitrary", "arbitrary", "arbitrary"),
        name="fnet_time",
    )(ct, st, a, b)


def _dft_mats(n, scale):
    idx = np.arange(n, dtype=np.int64)
    ang = 2.0 * np.pi * ((idx[:, None] * idx[None, :]) % n).astype(np.float64) / n
    return np.cos(ang) * scale, np.sin(ang) * scale


def _row_copy(src_hbm, row, dst_vmem, r, sem):
    return pltpu.make_async_copy(src_hbm.at[pl.ds(row, 1)], dst_vmem.at[pl.ds(r, 1)], sem)


GCH = 32


def _moe_body(bs_ref, nv_ref, tok_ref, h_hbm, wg_ref, wu_ref, wd_ref, y_hbm, xbuf, obuf, gsem, osem, wgb, wub, wdb,
              *, nb):
    e = pl.program_id(0)
    b0 = bs_ref[e]
    b1 = bs_ref[e + 1]
    total = bs_ref[ME]

    def gather(g):
        slot = g % 2
        base = g * TME
        for c in range(TME // GCH):
            @pl.when(c * GCH < nv_ref[g])
            def _():
                for r in range(c * GCH, (c + 1) * GCH):
                    _row_copy(h_hbm, tok_ref[base + r], xbuf.at[slot], r, gsem.at[slot]).start(priority=r % 2)

    def gather_wait(g):
        slot = g % 2
        for c in range(TME // GCH):
            @pl.when(c * GCH < nv_ref[g])
            def _():
                pltpu.make_async_copy(h_hbm.at[pl.ds(0, GCH)], xbuf.at[slot, pl.ds(c * GCH, GCH)],
                                      gsem.at[slot]).wait()

    def out_copy(g):
        slot = g % 2
        dst = y_hbm.at[pl.ds(pl.multiple_of(g * TME, TME), TME)]
        return pltpu.make_async_copy(obuf.at[slot], dst, osem.at[slot])

    @pl.when(e == 0)
    def _():
        xbuf[...] = jnp.zeros_like(xbuf)

        @pl.when(total > 0)
        def _():
            gather(0)

    @pl.when(b1 > b0)
    def _():
        wgb[...] = wg_ref[...].astype(BF16)
        wub[...] = wu_ref[...].astype(BF16)
        wdb[...] = wd_ref[...].astype(BF16)

    def block(g, carry):
        slot = g % 2

        @pl.when(g + 1 < total)
        def _():
            gather(g + 1)

        gather_wait(g)
        x_lo, x_hi = _unpack_rows(xbuf[slot])
        x = jnp.concatenate([x_lo.astype(BF16), x_hi.astype(BF16)], axis=1)
        hmid = _silu(_dot(x, wgb[...])) * _dot(x, wub[...])
        y = _pack_rows(_dot(hmid.astype(BF16), wdb[...]))

        @pl.when(g >= 2)
        def _():
            out_copy(g - 2).wait()

        obuf[slot] = y
        out_copy(g).start()
        return carry

    lax.fori_loop(b0, b1, block, 0)

    @pl.when(e == ME - 1)
    def _():
        @pl.when(total >= 2)
        def _():
            out_copy(total - 2).wait()

        @pl.when(total >= 1)
        def _():
            out_copy(total - 1).wait()

        def zero_copy(g):
            dst = y_hbm.at[pl.ds(pl.multiple_of(g * TME, TME), TME)]
            return pltpu.make_async_copy(obuf.at[0], dst, osem.at[0])

        obuf[0] = jnp.zeros((TME, HALF), jnp.uint32)
        lax.fori_loop(total, nb, lambda g, c: (zero_copy(g).start(), c)[1], 0)
        lax.fori_loop(total, nb, lambda g, c: (zero_copy(g).wait(), c)[1], 0)


def _moe_experts(layer, blk_start, blk_rows, buf_tok, h2, w_gate, w_up, w_down):
    nb = buf_tok.shape[0] // TME
    w_in_spec = pl.BlockSpec((None, None, D, MDE), lambda e, bs, nv, tk: (layer, e, 0, 0))
    return pl.pallas_call(
        functools.partial(_moe_body, nb=nb),
        out_shape=jax.ShapeDtypeStruct((nb * TME, HALF), jnp.uint32),
        grid_spec=pltpu.PrefetchScalarGridSpec(
            num_scalar_prefetch=3,
            grid=(ME,),
            in_specs=[
                pl.BlockSpec(memory_space=pl.ANY),
                w_in_spec,
                w_in_spec,
                pl.BlockSpec((None, None, MDE, D), lambda e, bs, nv, tk: (layer, e, 0, 0)),
            ],
            out_specs=pl.BlockSpec(memory_space=pl.ANY),
            scratch_shapes=[
                pltpu.VMEM((2, TME, HALF), jnp.uint32),
                pltpu.VMEM((2, TME, HALF), jnp.uint32),
                pltpu.SemaphoreType.DMA((2,)),
                pltpu.SemaphoreType.DMA((2,)),
                pltpu.VMEM((D, MDE), BF16),
                pltpu.VMEM((D, MDE), BF16),
                pltpu.VMEM((MDE, D), BF16),
            ],
        ),
        compiler_params=_cp("arbitrary"),
        name="moe_experts",
    )(blk_start, blk_rows, buf_tok, h2, w_gate, w_up, w_down)


TMC = 128


def _combine_body(dest_ref, x_ref, gate_ref, wt_ref, g_ref, sh_ref, sc_ref, y_hbm, *rest, nt, last):
    if last:
        h_ref, ybuf, sem = rest
    else:
        o_ref, h_ref, ybuf, sem = rest
    i = pl.program_id(0)

    def gather(tile, slot):
        base = tile * (2 * TMC)

        for r in range(TMC):
            _row_copy(y_hbm, dest_ref[base + 2 * r], ybuf.at[slot, 0], r, sem.at[slot]).start(priority=0)
            _row_copy(y_hbm, dest_ref[base + 2 * r + 1], ybuf.at[slot, 1], r, sem.at[slot]).start(priority=1)

    @pl.when(i == 0)
    def _():
        gather(0, 0)

    @pl.when(i + 1 < nt)
    def _():
        gather(i + 1, (i + 1) % 2)

    slot = i % 2
    for k in range(2):
        pltpu.make_async_copy(y_hbm.at[pl.ds(0, TMC)], ybuf.at[slot, k], sem.at[slot]).wait()
    wt = wt_ref[...]
    w0, w1 = wt[:, 0:1], wt[:, 1:2]
    lo0, hi0 = _unpack_rows(ybuf[slot, 0])
    lo1, hi1 = _unpack_rows(ybuf[slot, 1])
    y = jnp.concatenate([w0 * lo0 + w1 * lo1, w0 * hi0 + w1 * hi1], axis=1)
    xn = x_ref[...] + gate_ref[...] * y
    if last:
        h_ref[...] = xn * lax.rsqrt(jnp.mean(xn * xn, axis=-1, keepdims=True) + EPS) * g_ref[...]
    else:
        o_ref[...] = xn
        h_ref[...] = _norm_mod_val(xn, g_ref[...], sh_ref[...], sc_ref[...]).astype(h_ref.dtype)


def _combine(dest, xs, mod3, wt, yb, rows, g_next, mod3_next, last):
    nt = rows // TMC
    row_spec = pl.BlockSpec((TMC, D), lambda i, d: (i, 0))

    def mod_spec(which):
        return pl.BlockSpec((None, 1, D), lambda i, d: (_seg_of(i, TMC) * N_MOD + which, 0, 0))

    if last:
        out_shape = jax.ShapeDtypeStruct((rows, D), F32)
        out_specs = row_spec
    else:
        out_shape = (jax.ShapeDtypeStruct((rows, D), F32), jax.ShapeDtypeStruct((rows, D), BF16))
        out_specs = (row_spec, row_spec)
    return pl.pallas_call(
        functools.partial(_combine_body, nt=nt, last=last),
        out_shape=out_shape,
        grid_spec=pltpu.PrefetchScalarGridSpec(
            num_scalar_prefetch=1,
            grid=(nt,),
            in_specs=[
                row_spec,
                mod_spec(5),
                pl.BlockSpec((TMC, 128), lambda i, d: (i, 0)),
                pl.BlockSpec((1, D), lambda i, d: (0, 0)),
                mod_spec(0),
                mod_spec(1),
                pl.BlockSpec(memory_space=pl.ANY),
            ],
            out_specs=out_specs,
            scratch_shapes=[
                pltpu.VMEM((2, 2, TMC, HALF), jnp.uint32),
                pltpu.SemaphoreType.DMA((2,)),
            ],
        ),
        compiler_params=_cp("arbitrary"),
        name="moe_combine",
    )(dest, xs, mod3, wt, g_next.reshape(1, D), mod3_next, mod3_next, yb)


PLAN_CHUNK = 1024


def _dest_body(ids_ref, base_ref, o_ref):
    ids = ids_ref[...].astype(F32)
    lane_i = lax.broadcasted_iota(jnp.int32, ids.shape, 1)
    lane = lane_i.astype(F32)

    def pick(k):
        return jnp.sum(jnp.where(lane_i == k, ids, 0.0), axis=-1, keepdims=True)

    base = base_ref[...]
    d0 = jnp.sum(jnp.where(lane == pick(0), base[0:1, :], 0.0), axis=-1, keepdims=True) + pick(2)
    d1 = jnp.sum(jnp.where(lane == pick(1), base[1:2, :], 0.0), axis=-1, keepdims=True) + pick(3)
    o_ref[...] = jnp.where(lane_i == 0, d0, jnp.where(lane_i == 1, d1, 0.0)).astype(jnp.int32)


def _slot_body(dest_ref, tok_ref, *, nslots):
    i = pl.program_id(0)

    @pl.when(i == 0)
    def _():
        def clear(s, carry):
            tok_ref[s] = 0
            return carry

        lax.fori_loop(0, nslots, clear, 0, unroll=32)

    def place(t, carry):
        tok = i * PLAN_CHUNK + t
        tok_ref[dest_ref[0, 2 * t]] = tok
        tok_ref[dest_ref[0, 2 * t + 1]] = tok
        return carry

    lax.fori_loop(0, PLAN_CHUNK, place, 0, unroll=16)


def _route_plan(eid, cnt, rows):
    a = rows * 2
    nb = -(-a // TME) + ME
    steps = rows // PLAN_CHUNK
    c0 = cnt[0, :ME].astype(jnp.int32)
    c1 = cnt[1, :ME].astype(jnp.int32)
    padded = ((c0 + c1 + TME - 1) // TME) * TME
    pad_end = jnp.cumsum(padded)
    pad_start = pad_end - padded
    blk_start = jnp.concatenate([jnp.zeros((1,), jnp.int32), (pad_end // TME).astype(jnp.int32)])
    blk = jnp.arange(nb, dtype=jnp.int32)
    blk_e = jnp.minimum(jnp.searchsorted(blk_start[1:], blk, side='right'), ME - 1)
    blk_rows = jnp.clip((c0 + c1)[blk_e] - (blk - blk_start[blk_e]) * TME, 0, TME).astype(jnp.int32)
    base = jnp.zeros((8, 128), F32).at[0, :ME].set(pad_start.astype(F32)).at[1, :ME].set((pad_start + c0).astype(F32))
    dest_t = pl.pallas_call(
        _dest_body,
        out_shape=jax.ShapeDtypeStruct((rows, 128), jnp.int32),
        grid=(steps,),
        in_specs=[pl.BlockSpec((PLAN_CHUNK, 128), lambda i: (i, 0)), pl.BlockSpec((8, 128), lambda i: (0, 0))],
        out_specs=pl.BlockSpec((PLAN_CHUNK, 128), lambda i: (i, 0)),
        compiler_params=_cp("arbitrary"),
        name="route_dest",
    )(eid, base)
    dest = dest_t[:, :2].reshape(a)
    buf_tok = pl.pallas_call(
        functools.partial(_slot_body, nslots=nb * TME),
        out_shape=jax.ShapeDtypeStruct((nb * TME,), jnp.int32),
        grid=(steps,),
        in_specs=[pl.BlockSpec((None, 1, 2 * PLAN_CHUNK), lambda i: (i, 0, 0), memory_space=pltpu.SMEM)],
        out_specs=pl.BlockSpec(memory_space=pltpu.SMEM),
        compiler_params=_cp("arbitrary"),
        name="route_slots",
    )(dest.reshape(steps, 1, 2 * PLAN_CHUNK))
    return dest, buf_tok, blk_start, blk_rows


def _sincos_2d(rows, cols, d):
    quarter = d // 4
    omega = 1.0 / (POS_BASE ** (jnp.arange(quarter, dtype=F32) / quarter))

    def axis_emb(n):
        p = jnp.arange(n, dtype=F32)[:, None] * omega[None, :]
        return jnp.concatenate([jnp.sin(p), jnp.cos(p)], axis=-1)

    er, ec = axis_emb(rows), axis_emb(cols)
    half = 2 * quarter
    pos = jnp.concatenate([jnp.broadcast_to(er[:, None, :], (rows, cols, half)),
                           jnp.broadcast_to(ec[None, :, :], (rows, cols, half))], axis=-1)
    return pos.reshape(rows * cols, 2 * half)


def _gla_weights(w_in, wg_f, wg_b):
    w_cat = jnp.concatenate([w_in[:, :QK + VV], w_in[:, STATE_COLS:]], axis=1).astype(BF16)
    w_z = jnp.concatenate([w_in[:, QK + VV:STATE_COLS], jnp.zeros((D, ZPAD - 2 * RANK), F32)], axis=1).astype(BF16)
    wgf = jnp.zeros((128, QK), F32).at[:RANK].set(wg_f)
    wgb = jnp.zeros((128, QK), F32).at[RANK:2 * RANK].set(wg_b)
    return w_cat, w_z, wgf, wgb


def kernel(x, c, ctx, c_ctx, ada_w, ada_b, norm1_g, norm2_g, gla_w_in, gla_wg_f, gla_bg_f, gla_wg_b, gla_bg_b,
           gla_onorm_g, gla_w_out, fnet_w_in, fnet_w_out, moe_rw_group, moe_rb_group, moe_rw_expert,
           moe_rb_expert, moe_w_gate, moe_w_up, moe_w_down, final_g):
    cvec = jnp.zeros((SEGS, D), F32).at[:B].set(c).at[CTX_SEG].set(c_ctx)
    mods = _mod_all(cvec, ada_w, ada_b)
    pos = _sincos_2d(SEQ // GRID_W, GRID_W, D)
    mod3s = [mods[i].reshape(SEGS * N_MOD, 1, D) for i in range(DEPTH)]
    xs, h = _assemble(x.reshape(LAT, D), pos, ctx.reshape(NCTX, D), norm1_g[0], mod3s[0])

    cc, sc = _dft_mats(FD, 1.0)
    cs = jnp.asarray(np.concatenate([cc, sc], axis=1), BF16)
    ct_l, st_l = _dft_mats(SEQ, (SEQ * FD) ** -0.5)
    ct_c, st_c = _dft_mats(CTX, (CTX * FD) ** -0.5)
    ct_l, st_l = jnp.asarray(ct_l, BF16), jnp.asarray(-st_l, BF16)
    ct_c, st_c = jnp.asarray(ct_c, BF16), jnp.asarray(-st_c, BF16)
    s_zero = jnp.zeros((B, H, DV, DK), F32)
    last_reader = ((DEPTH - 1) // 2) * 2

    for i in range(DEPTH):
        kind, j = i % 2, i // 2
        ctx_live = i < last_reader
        ctx_needed = i <= last_reader
        mod3 = mod3s[i]
        rows_in = ROWS if ctx_needed else LAT
        rows_out = ROWS if ctx_live else LAT

        if kind == 0:
            w_cat, w_z, wgf, wgb = _gla_weights(gla_w_in[j], gla_wg_f[j], gla_wg_b[j])
            bgf, bgb = gla_bg_f[j].reshape(1, QK), gla_bg_b[j].reshape(1, QK)
            onorm = gla_onorm_g[j].reshape(1, DV)
            proj = _mm(h, w_cat, rows_in, 1024, 1536, BF16, "gla_proj")
            zproj = _mm(h, w_z, rows_in, 512, ZPAD, F32, "gla_gate_proj")
            o_c, sf, sb = _gla(proj, zproj, wgf, bgf, wgb, bgb, onorm, s_zero, s_zero, CTX, LAT, ctx_live)
            o_l, _, _ = _gla(proj, zproj, wgf, bgf, wgb, bgb, onorm, sf, sb, SEQ, 0, True)
            mix_l, mix_c, w_out = o_l, o_c, gla_w_out[j].astype(BF16)
        else:
            a, b = _fnet_in(h, fnet_w_in[j].astype(BF16), cs, rows_in)
            y = _fnet_time(ct_l, st_l, a, b, SEQ, 0)
            y_c = _fnet_time(ct_c, st_c, a, b, CTX, LAT) if ctx_live else y
            mix_l, mix_c, w_out = y, y_c, fnet_w_out[j].astype(BF16)

        rw = jnp.zeros((D, 128), F32).at[:, :MG].set(moe_rw_group[i])
        rw = rw.at[:, MG:MG + ME].set(jnp.transpose(moe_rw_expert[i], (1, 0, 2)).reshape(D, ME))
        rb = jnp.zeros((1, 128), F32).at[0, :MG].set(moe_rb_group[i]).at[0, MG:MG + ME].set(
            moe_rb_expert[i].reshape(ME))
        rw_hi = rw.astype(BF16)
        rw_cat = jnp.concatenate([rw_hi, (rw - rw_hi.astype(F32)).astype(BF16)], axis=1)
        xs, h2, eid, wt, cnt = _mix_route(mix_l, mix_c, w_out, xs, norm2_g[i], mod3, rw_cat, rb, rows_out)
        dest, buf_tok, blk_start, blk_rows = _route_plan(eid, cnt, rows_out)
        yb = _moe_experts(i, blk_start, blk_rows, buf_tok, h2, moe_w_gate, moe_w_up, moe_w_down)
        if i + 1 < DEPTH:
            xs, h = _combine(dest, xs, mod3, wt, yb, rows_out, norm1_g[i + 1], mod3s[i + 1], False)
        else:
            out = _combine(dest, xs, mod3, wt, yb, rows_out, final_g, mod3, True)

    return out.reshape(B, SEQ, D)
```

```python
import functools

import jax
import jax.numpy as jnp
import numpy as np
from jax import lax
from jax.experimental import pallas as pl
from jax.experimental.pallas import tpu as pltpu

F32 = jnp.float32
BF16 = jnp.bfloat16

D = 2048
B = 4
SEQ = 2048
CTX = 256
DEPTH = 4
GRID_W = 64
EPS = 1e-6
POS_BASE = 10000.0
N_MOD = 6

H = 4
DK = 256
DV = 512
QK = H * DK
VV = H * DV
RANK = 16
GATE_NORM = 16.0
CHUNK = 64
STATE_COLS = QK + VV + 2 * RANK
ZPAD = 256
PROJ_N = QK + VV + QK + VV
COL_K, COL_V, COL_Q, COL_R = 0, QK, QK + VV, 2 * QK + VV

FG = 4
FD = D // FG

MG = 4
MPG = 8
ME = MG * MPG
MDE = D // 4
TME = 256

LAT = B * SEQ
NCTX = B * CTX
ROWS = LAT + NCTX
SEGS = 8
CTX_SEG = B

VMEM_LIMIT = 56 * 1024 * 1024


def _cp(*sem):
    return pltpu.CompilerParams(dimension_semantics=sem, vmem_limit_bytes=VMEM_LIMIT)


def _seg_of(i, tm):
    return jnp.where(i < LAT // tm, i // (SEQ // tm), CTX_SEG)


def _split2(a):
    hi = a.astype(BF16)
    lo = (a - hi.astype(F32)).astype(BF16)
    return hi, lo


def _dot(a, b):
    return jnp.dot(a, b, preferred_element_type=F32)


def _dot3(a, b_hi, b_lo):
    a_hi, a_lo = _split2(a)
    return _dot(a_hi, b_hi) + _dot(a_lo, b_hi) + _dot(a_hi, b_lo)


def _silu(a):
    return a * jax.nn.sigmoid(a)


HALF = D // 2


def _pack_rows(a):
    lo = lax.bitcast_convert_type(a[:, :HALF].astype(BF16).astype(F32), jnp.uint32)
    hi = lax.bitcast_convert_type(a[:, HALF:].astype(BF16).astype(F32), jnp.uint32)
    return (hi & jnp.uint32(0xFFFF0000)) | (lo >> 16)


def _unpack_rows(w):
    lo = lax.bitcast_convert_type(w << 16, F32)
    hi = lax.bitcast_convert_type(w & jnp.uint32(0xFFFF0000), F32)
    return lo, hi


def _mod_body(c_ref, w_ref, b_ref, o_ref):
    s = _silu(c_ref[...])
    w_hi, w_lo = _split2(w_ref[0])
    o_ref[0] = _dot3(s, w_hi, w_lo) + b_ref[0]


def _mod_all(cvec, ada_w, ada_b):
    tn = 1024
    return pl.pallas_call(
        _mod_body,
        out_shape=jax.ShapeDtypeStruct((DEPTH, SEGS, N_MOD * D), F32),
        grid=(DEPTH, N_MOD * D // tn),
        in_specs=[
            pl.BlockSpec((SEGS, D), lambda l, j: (0, 0)),
            pl.BlockSpec((1, D, tn), lambda l, j: (l, 0, j)),
            pl.BlockSpec((1, 1, tn), lambda l, j: (l, 0, j)),
        ],
        out_specs=pl.BlockSpec((1, SEGS, tn), lambda l, j: (l, 0, j)),
        compiler_params=_cp("arbitrary", "arbitrary"),
        name="adaln_mod",
    )(cvec, ada_w, ada_b.reshape(DEPTH, 1, N_MOD * D))


def _assemble_body(x_ref, pos_ref, ctx_ref, g_ref, sh_ref, sc_ref, o_ref, h_ref, *, nlat):
    i = pl.program_id(0)

    @pl.when(i < nlat)
    def _():
        o_ref[...] = x_ref[...] + pos_ref[...]

    @pl.when(i >= nlat)
    def _():
        o_ref[...] = ctx_ref[...]

    h_ref[...] = _norm_mod_val(o_ref[...], g_ref[...], sh_ref[...], sc_ref[...]).astype(h_ref.dtype)


def _assemble(x2, pos, ctx2, g, mod3):
    tm = 256
    nlat = LAT // tm
    return pl.pallas_call(
        functools.partial(_assemble_body, nlat=nlat),
        out_shape=(jax.ShapeDtypeStruct((ROWS, D), F32), jax.ShapeDtypeStruct((ROWS, D), BF16)),
        grid=(ROWS // tm,),
        in_specs=[
            pl.BlockSpec((tm, D), lambda i: (jnp.minimum(i, nlat - 1), 0)),
            pl.BlockSpec((tm, D), lambda i: (i % (SEQ // tm), 0)),
            pl.BlockSpec((tm, D), lambda i: (jnp.maximum(i - nlat, 0), 0)),
            pl.BlockSpec((1, D), lambda i: (0, 0)),
            _mod_spec(0, tm),
            _mod_spec(1, tm),
        ],
        out_specs=(pl.BlockSpec((tm, D), lambda i: (i, 0)), pl.BlockSpec((tm, D), lambda i: (i, 0))),
        compiler_params=_cp("arbitrary"),
        name="assemble_stream",
    )(x2, pos, ctx2, g.reshape(1, D), mod3, mod3)


def _norm_mod_val(x, g, sh, sc):
    y = x * lax.rsqrt(jnp.mean(x * x, axis=-1, keepdims=True) + EPS) * g
    return y * (1.0 + sc) + sh


def _mod_spec(which, tm):
    return pl.BlockSpec((None, 1, D), lambda i: (_seg_of(i, tm) * N_MOD + which, 0, 0))


def _route_tile(h2, rwc_ref, rb_ref, run_s):
    h_hi, h_lo = _split2(h2)
    p = _dot(h_hi, rwc_ref[...])
    lg = p[:, :128] + p[:, 128:] + _dot(h_lo, rwc_ref[:, :128]) + rb_ref[...]
    lane_i = lax.broadcasted_iota(jnp.int32, lg.shape, 1)
    lane = lane_i.astype(F32)
    neg = jnp.float32(-jnp.inf)
    big = jnp.float32(1024.0)

    glog = jnp.where(lane < MG, lg, neg)
    gmax = jnp.max(glog, axis=-1, keepdims=True)
    gidx = jnp.min(jnp.where(glog == gmax, lane, big), axis=-1, keepdims=True)
    g_w = 1.0 / jnp.sum(jnp.exp(glog - gmax), axis=-1, keepdims=True)

    lo = MG + MPG * gidx
    el = jnp.where((lane >= lo) & (lane < lo + MPG), lg, neg)
    m1 = jnp.max(el, axis=-1, keepdims=True)
    i1 = jnp.min(jnp.where(el == m1, lane, big), axis=-1, keepdims=True)
    el2 = jnp.where(lane == i1, neg, el)
    m2 = jnp.max(el2, axis=-1, keepdims=True)
    i2 = jnp.min(jnp.where(el2 == m2, lane, big), axis=-1, keepdims=True)
    e2 = jnp.exp(m2 - m1)
    den = 1.0 / (1.0 + e2)
    w1 = g_w * den
    w2 = g_w * (e2 * den)
    e_a = i1 - MG
    e_b = i2 - MG

    tm = lg.shape[0]
    r_i = lax.broadcasted_iota(jnp.int32, (tm, tm), 0)
    c_i = lax.broadcasted_iota(jnp.int32, (tm, tm), 1)
    earlier = jnp.where(r_i > c_i, 1.0, 0.0).astype(BF16)
    oh_a = jnp.where(lane == e_a, 1.0, 0.0)
    oh_b = jnp.where(lane == e_b, 1.0, 0.0)
    run = run_s[...]
    rank_a = jnp.sum(oh_a * (_dot(earlier, oh_a.astype(BF16)) + run[0:1, :]), axis=-1, keepdims=True)
    rank_b = jnp.sum(oh_b * (_dot(earlier, oh_b.astype(BF16)) + run[1:2, :]), axis=-1, keepdims=True)
    run_s[0:1, :] = run[0:1, :] + jnp.sum(oh_a, axis=0, keepdims=True)
    run_s[1:2, :] = run[1:2, :] + jnp.sum(oh_b, axis=0, keepdims=True)

    ids = jnp.where(lane_i == 0, e_a, jnp.where(lane_i == 1, e_b, jnp.where(lane_i == 2, rank_a,
                                                                           jnp.where(lane_i == 3, rank_b, 0.0))))
    wts = jnp.where(lane_i == 0, w1, jnp.where(lane_i == 1, w2, 0.0))
    return ids.astype(jnp.int32), wts


def _mix_route_body(al_ref, ac_ref, w_ref, res_ref, gate_ref, g_ref, sh_ref, sc_ref, rwc_ref, rb_ref,
                    o_ref, h_ref, eid_ref, wt_ref, cnt_ref, acc_s, run_s, *, nlat):
    i = pl.program_id(0)

    @pl.when(i == 0)
    def _():
        run_s[...] = jnp.zeros_like(run_s)

    @pl.when(i < nlat)
    def _():
        acc_s[...] = _dot(al_ref[...], w_ref[...])

    @pl.when(i >= nlat)
    def _():
        acc_s[...] = _dot(ac_ref[...], w_ref[...])

    xn = res_ref[...] + gate_ref[...] * acc_s[...]
    o_ref[...] = xn
    h2 = _norm_mod_val(xn, g_ref[...], sh_ref[...], sc_ref[...])
    h_ref[...] = _pack_rows(h2)
    ids, wts = _route_tile(h2, rwc_ref, rb_ref, run_s)
    eid_ref[...] = ids
    wt_ref[...] = wts
    cnt_ref[...] = run_s[...]


def _mix_route(a_lat, a_ctx, w, xs, g, mod3, rw_cat, rb, rows):
    tm = 256
    k = a_lat.shape[1]
    nlat = LAT // tm
    return pl.pallas_call(
        functools.partial(_mix_route_body, nlat=nlat),
        out_shape=(
            jax.ShapeDtypeStruct((rows, D), F32),
            jax.ShapeDtypeStruct((rows, HALF), jnp.uint32),
            jax.ShapeDtypeStruct((rows, 128), jnp.int32),
            jax.ShapeDtypeStruct((rows, 128), F32),
            jax.ShapeDtypeStruct((8, 128), F32),
        ),
        grid=(rows // tm,),
        in_specs=[
            pl.BlockSpec((tm, k), lambda i: (jnp.minimum(i, nlat - 1), 0)),
            pl.BlockSpec((tm, k), lambda i: (jnp.maximum(i - nlat, 0), 0)),
            pl.BlockSpec((k, D), lambda i: (0, 0)),
            pl.BlockSpec((tm, D), lambda i: (i, 0)),
            _mod_spec(2, tm),
            pl.BlockSpec((1, D), lambda i: (0, 0)),
            _mod_spec(3, tm),
            _mod_spec(4, tm),
            pl.BlockSpec((D, 256), lambda i: (0, 0)),
            pl.BlockSpec((1, 128), lambda i: (0, 0)),
        ],
        out_specs=(
            pl.BlockSpec((tm, D), lambda i: (i, 0)),
            pl.BlockSpec((tm, HALF), lambda i: (i, 0)),
            pl.BlockSpec((tm, 128), lambda i: (i, 0)),
            pl.BlockSpec((tm, 128), lambda i: (i, 0)),
            pl.BlockSpec((8, 128), lambda i: (0, 0)),
        ),
        scratch_shapes=[pltpu.VMEM((tm, D), F32), pltpu.VMEM((8, 128), F32)],
        compiler_params=_cp("arbitrary"),
        name="mix_route",
    )(a_lat, a_ctx, w, xs, mod3, g.reshape(1, D), mod3, mod3, rw_cat, rb)


def _mm_body(x_ref, w_ref, o_ref):
    o_ref[...] = _dot(x_ref[...], w_ref[...]).astype(o_ref.dtype)


def _mm(x, w, rows, tm, tn, out_dtype, name):
    k = x.shape[1]
    n = w.shape[1]
    return pl.pallas_call(
        _mm_body,
        out_shape=jax.ShapeDtypeStruct((rows, n), out_dtype),
        grid=(n // tn, rows // tm),
        in_specs=[
            pl.BlockSpec((tm, k), lambda j, i: (i, 0)),
            pl.BlockSpec((k, tn), lambda j, i: (0, j)),
        ],
        out_specs=pl.BlockSpec((tm, tn), lambda j, i: (i, j)),
        compiler_params=_cp("arbitrary", "arbitrary"),
        name=name,
    )(x, w)


def _log_sigmoid(a):
    return jnp.minimum(a, 0.0) - jnp.log(1.0 + jnp.exp(-jnp.abs(a)))


PREP = 256


def _gla_body(k_ref, v_ref, q_ref, r_ref, z_ref, wgf_ref, bgf_ref, wgb_ref, bgb_ref, on_ref, s0f_ref, s0b_ref,
              o_ref, sf_ref, sb_ref,
              qf_s, kf_s, df_s, qb_s, kb_s, db_s, vb_s, dec_s, o_s, stf_s, stb_s, *, T, emit_o):
    nc = T // CHUNK
    nblk = T // PREP
    cpb = PREP // CHUNK
    scale = DK ** -0.5

    row = lax.broadcasted_iota(jnp.int32, (PREP, PREP), 0)
    col = lax.broadcasted_iota(jnp.int32, (PREP, PREP), 1)
    shift = CHUNK.bit_length() - 1
    same = lax.shift_right_logical(row, shift) == lax.shift_right_logical(col, shift)
    tri_f = jnp.where(same & (row >= col), 1.0, 0.0).astype(BF16)
    tri_b = jnp.where(same & (row <= col), 1.0, 0.0).astype(BF16)
    wf_hi, wf_lo = _split2(wgf_ref[...])
    wb_hi, wb_lo = _split2(wgb_ref[...])

    def chunk_sums(tri, g):
        g1, g2 = _split2(g)
        return _dot(tri, g1) + _dot(tri, g2)

    def edge_rows(G, e):
        return jnp.concatenate(
            [jnp.broadcast_to(G[c * CHUNK + e:c * CHUNK + e + 1, :], (CHUNK, DK)) for c in range(cpb)], axis=0)

    def prep(blk, carry):
        rows = pl.ds(pl.multiple_of(blk * PREP, PREP), PREP)
        z = z_ref[rows, :128]
        gf = _log_sigmoid(_dot3(z, wf_hi, wf_lo) + bgf_ref[...]) / GATE_NORM
        gb = _log_sigmoid(_dot3(z, wb_hi, wb_lo) + bgb_ref[...]) / GATE_NORM
        Gf = chunk_sums(tri_f, gf)
        Gb = chunk_sums(tri_b, gb)
        k = k_ref[rows, :].astype(F32)
        q = q_ref[rows, :].astype(F32) * scale
        qf_s[rows, :] = (q * jnp.exp(Gf)).astype(BF16)
        kf_s[rows, :] = (k * jnp.exp(-Gf)).astype(BF16)
        df_s[rows, :] = (k * jnp.exp(edge_rows(Gf, CHUNK - 1) - Gf)).astype(BF16)
        qb_s[rows, :] = (q * jnp.exp(Gb)).astype(BF16)
        kb_s[rows, :] = (k * jnp.exp(-Gb)).astype(BF16)
        db_s[rows, :] = (k * jnp.exp(edge_rows(Gb, 0) - Gb)).astype(BF16)
        vb_s[rows, :] = v_ref[rows, :]
        for c in range(cpb):
            ef = Gf[c * CHUNK + CHUNK - 1:c * CHUNK + CHUNK, :]
            eb = Gb[c * CHUNK:c * CHUNK + 1, :]
            dec_s[0, blk * cpb + c] = jnp.broadcast_to(jnp.exp(ef), (8, DK))
            dec_s[1, blk * cpb + c] = jnp.broadcast_to(jnp.exp(eb), (8, DK))
        return carry

    lax.fori_loop(0, nblk, prep, 0)

    stf_s[...] = s0f_ref[0, 0]
    stb_s[...] = s0b_ref[0, 0]
    r64 = lax.broadcasted_iota(jnp.int32, (CHUNK, CHUNK), 0)
    c64 = lax.broadcasted_iota(jnp.int32, (CHUNK, CHUNK), 1)
    nt_dims = (((1,), (1,)), ((), ()))
    tn_dims = (((0,), (0,)), ((), ()))

    def chunk(c, q_s, k_s, d_s, st_s, di, keep):
        rows = pl.ds(pl.multiple_of(c * CHUNK, CHUNK), CHUNK)
        qe = q_s[rows, :]
        vb = vb_s[rows, :]
        st = st_s[...]
        o = None
        if emit_o:
            a = lax.dot_general(qe, k_s[rows, :], nt_dims, preferred_element_type=F32)
            a = jnp.where(keep, a, 0.0).astype(BF16)
            o = _dot(a, vb) + lax.dot_general(qe, st.astype(BF16), nt_dims, preferred_element_type=F32)
        upd = lax.dot_general(vb, d_s[rows, :], tn_dims, preferred_element_type=F32)
        st_s[...] = st * dec_s[di, c][0:1, :] + upd
        return o, rows

    def step(ci, carry, accumulate):
        of, rows_f = chunk(ci, qf_s, kf_s, df_s, stf_s, 0, r64 >= c64)
        ob, rows_b = chunk(nc - 1 - ci, qb_s, kb_s, db_s, stb_s, 1, r64 <= c64)
        if emit_o:
            if accumulate:
                o_s[rows_f, :] = o_s[rows_f, :] + of
                o_s[rows_b, :] = o_s[rows_b, :] + ob
            else:
                o_s[rows_f, :] = of
                o_s[rows_b, :] = ob
        return carry

    lax.fori_loop(0, nc // 2, functools.partial(step, accumulate=False), 0, unroll=min(8, nc // 2))
    lax.fori_loop(nc // 2, nc, functools.partial(step, accumulate=True), 0, unroll=min(8, nc // 2))
    sf_ref[0, 0] = stf_s[...]
    sb_ref[0, 0] = stb_s[...]

    if emit_o:
        def fin(blk, carry):
            rows = pl.ds(pl.multiple_of(blk * PREP, PREP), PREP)
            tot = o_s[rows, :]
            y = tot * lax.rsqrt(jnp.mean(tot * tot, axis=-1, keepdims=True) + EPS) * on_ref[...]
            o_ref[rows, :] = (y * _silu(r_ref[rows, :].astype(F32))).astype(o_ref.dtype)
            return carry

        lax.fori_loop(0, nblk, fin, 0)
    else:
        o_ref[...] = jnp.zeros_like(o_ref)


def _gla(proj, zproj, wgf, bgf, wgb, bgb, onorm, s0f, s0b, T, row_off, emit_o):
    rb = row_off // T
    st_spec = pl.BlockSpec((1, 1, DV, DK), lambda b, h: (b, h, 0, 0))
    return pl.pallas_call(
        functools.partial(_gla_body, T=T, emit_o=emit_o),
        out_shape=(
            jax.ShapeDtypeStruct((B * T, VV), BF16),
            jax.ShapeDtypeStruct((B, H, DV, DK), F32),
            jax.ShapeDtypeStruct((B, H, DV, DK), F32),
        ),
        grid=(B, H),
        in_specs=[
            pl.BlockSpec((T, DK), lambda b, h: (rb + b, COL_K // DK + h)),
            pl.BlockSpec((T, DV), lambda b, h: (rb + b, COL_V // DV + h)),
            pl.BlockSpec((T, DK), lambda b, h: (rb + b, COL_Q // DK + h)),
            pl.BlockSpec((T, DV), lambda b, h: (rb + b, COL_R // DV + h)),
            pl.BlockSpec((T, ZPAD), lambda b, h: (rb + b, 0)),
            pl.BlockSpec((128, DK), lambda b, h: (0, h)),
            pl.BlockSpec((1, DK), lambda b, h: (0, h)),
            pl.BlockSpec((128, DK), lambda b, h: (0, h)),
            pl.BlockSpec((1, DK), lambda b, h: (0, h)),
            pl.BlockSpec((1, DV), lambda b, h: (0, 0)),
            st_spec,
            st_spec,
        ],
        out_specs=(
            pl.BlockSpec((T, DV), lambda b, h: (b, h)),
            st_spec,
            st_spec,
        ),
        scratch_shapes=[pltpu.VMEM((T, DK), BF16)] * 6 + [
            pltpu.VMEM((T, DV), BF16),
            pltpu.VMEM((2, T // CHUNK, 8, DK), F32),
            pltpu.VMEM((T, DV), F32),
            pltpu.VMEM((DV, DK), F32),
            pltpu.VMEM((DV, DK), F32),
        ],
        compiler_params=_cp("arbitrary", "arbitrary"),
        name="gla_scan",
    )(proj, proj, proj, proj, zproj, wgf, bgf, wgb, bgb, onorm, s0f, s0b)


def _fnet_in_body(h_ref, w_ref, cs_ref, a_ref, b_ref):
    u = _dot(h_ref[...], w_ref[...]).astype(BF16)
    for g in range(FG):
        ab = _dot(u[:, g * FD:(g + 1) * FD], cs_ref[...])
        a_ref[:, g * FD:(g + 1) * FD] = ab[:, :FD].astype(BF16)
        b_ref[:, g * FD:(g + 1) * FD] = ab[:, FD:].astype(BF16)


def _fnet_in(h, w, cs, rows):
    tm = 512
    return pl.pallas_call(
        _fnet_in_body,
        out_shape=(jax.ShapeDtypeStruct((rows, D), BF16), jax.ShapeDtypeStruct((rows, D), BF16)),
        grid=(rows // tm,),
        in_specs=[
            pl.BlockSpec((tm, D), lambda i: (i, 0)),
            pl.BlockSpec((D, D), lambda i: (0, 0)),
            pl.BlockSpec((FD, 2 * FD), lambda i: (0, 0)),
        ],
        out_specs=(pl.BlockSpec((tm, D), lambda i: (i, 0)), pl.BlockSpec((tm, D), lambda i: (i, 0))),
        compiler_params=_cp("arbitrary"),
        name="fnet_in",
    )(h, w, cs)


def _fnet_time_body(ct_ref, st_ref, a_ref, b_ref, o_ref):
    o_ref[...] = (_dot(ct_ref[...], a_ref[...]) + _dot(st_ref[...], b_ref[...])).astype(o_ref.dtype)


def _fnet_time(ct, st, a, b, T, row_off):
    tm = min(T, 1024)
    tn = 1024
    rb = row_off // T
    return pl.pallas_call(
        _fnet_time_body,
        out_shape=jax.ShapeDtypeStruct((B * T, D), BF16),
        grid=(B, D // tn, T // tm),
        in_specs=[
            pl.BlockSpec((tm, T), lambda s, j, i: (i, 0)),
            pl.BlockSpec((tm, T), lambda s, j, i: (i, 0)),
            pl.BlockSpec((T, tn), lambda s, j, i: (rb + s, j)),
            pl.BlockSpec((T, tn), lambda s, j, i: (rb + s, j)),
        ],
        out_specs=pl.BlockSpec((tm, tn), lambda s, j, i: (s * (T // tm) + i, j)),
        compiler_params=_cp("arbitrary", "arbitrary", "arbitrary"),
        name="fnet_time",
    )(ct, st, a, b)


def _dft_mats(n, scale):
    idx = np.arange(n, dtype=np.int64)
    ang = 2.0 * np.pi * ((idx[:, None] * idx[None, :]) % n).astype(np.float64) / n
    return np.cos(ang) * scale, np.sin(ang) * scale


def _row_copy(src_hbm, row, dst_vmem, r, sem):
    return pltpu.make_async_copy(src_hbm.at[pl.ds(row, 1)], dst_vmem.at[pl.ds(r, 1)], sem)


GCH = 32


def _moe_body(bs_ref, nv_ref, tok_ref, h_hbm, wg_ref, wu_ref, wd_ref, y_hbm, xbuf, obuf, gsem, osem, wgb, wub, wdb,
              *, nb):
    e = pl.program_id(0)
    b0 = bs_ref[e]
    b1 = bs_ref[e + 1]
    total = bs_ref[ME]

    def gather(g):
        slot = g % 2
        base = g * TME
        for c in range(TME // GCH):
            @pl.when(c * GCH < nv_ref[g])
            def _():
                for r in range(c * GCH, (c + 1) * GCH):
                    _row_copy(h_hbm, tok_ref[base + r], xbuf.at[slot], r, gsem.at[slot]).start(priority=r % 2)

    def gather_wait(g):
        slot = g % 2
        for c in range(TME // GCH):
            @pl.when(c * GCH < nv_ref[g])
            def _():
                pltpu.make_async_copy(h_hbm.at[pl.ds(0, GCH)], xbuf.at[slot, pl.ds(c * GCH, GCH)],
                                      gsem.at[slot]).wait()

    def out_copy(g):
        slot = g % 2
        dst = y_hbm.at[pl.ds(pl.multiple_of(g * TME, TME), TME)]
        return pltpu.make_async_copy(obuf.at[slot], dst, osem.at[slot])

    @pl.when(e == 0)
    def _():
        xbuf[...] = jnp.zeros_like(xbuf)

        @pl.when(total > 0)
        def _():
            gather(0)

    @pl.when(b1 > b0)
    def _():
        wgb[...] = wg_ref[...].astype(BF16)
        wub[...] = wu_ref[...].astype(BF16)
        wdb[...] = wd_ref[...].astype(BF16)

    def block(g, carry):
        slot = g % 2

        @pl.when(g + 1 < total)
        def _():
            gather(g + 1)

        gather_wait(g)
        x_lo, x_hi = _unpack_rows(xbuf[slot])
        x = jnp.concatenate([x_lo.astype(BF16), x_hi.astype(BF16)], axis=1)
        hmid = _silu(_dot(x, wgb[...])) * _dot(x, wub[...])
        y = _pack_rows(_dot(hmid.astype(BF16), wdb[...]))

        @pl.when(g >= 2)
        def _():
            out_copy(g - 2).wait()

        obuf[slot] = y
        out_copy(g).start()
        return carry

    lax.fori_loop(b0, b1, block, 0)

    @pl.when(e == ME - 1)
    def _():
        @pl.when(total >= 2)
        def _():
            out_copy(total - 2).wait()

        @pl.when(total >= 1)
        def _():
            out_copy(total - 1).wait()

        def zero_copy(g):
            dst = y_hbm.at[pl.ds(pl.multiple_of(g * TME, TME), TME)]
            return pltpu.make_async_copy(obuf.at[0], dst, osem.at[0])

        obuf[0] = jnp.zeros((TME, HALF), jnp.uint32)
        lax.fori_loop(total, nb, lambda g, c: (zero_copy(g).start(), c)[1], 0)
        lax.fori_loop(total, nb, lambda g, c: (zero_copy(g).wait(), c)[1], 0)


def _moe_experts(layer, blk_start, blk_rows, buf_tok, h2, w_gate, w_up, w_down):
    nb = buf_tok.shape[0] // TME
    w_in_spec = pl.BlockSpec((None, None, D, MDE), lambda e, bs, nv, tk: (layer, e, 0, 0))
    return pl.pallas_call(
        functools.partial(_moe_body, nb=nb),
        out_shape=jax.ShapeDtypeStruct((nb * TME, HALF), jnp.uint32),
        grid_spec=pltpu.PrefetchScalarGridSpec(
            num_scalar_prefetch=3,
            grid=(ME,),
            in_specs=[
                pl.BlockSpec(memory_space=pl.ANY),
                w_in_spec,
                w_in_spec,
                pl.BlockSpec((None, None, MDE, D), lambda e, bs, nv, tk: (layer, e, 0, 0)),
            ],
            out_specs=pl.BlockSpec(memory_space=pl.ANY),
            scratch_shapes=[
                pltpu.VMEM((2, TME, HALF), jnp.uint32),
                pltpu.VMEM((2, TME, HALF), jnp.uint32),
                pltpu.SemaphoreType.DMA((2,)),
                pltpu.SemaphoreType.DMA((2,)),
                pltpu.VMEM((D, MDE), BF16),
                pltpu.VMEM((D, MDE), BF16),
                pltpu.VMEM((MDE, D), BF16),
            ],
        ),
        compiler_params=_cp("arbitrary"),
        name="moe_experts",
    )(blk_start, blk_rows, buf_tok, h2, w_gate, w_up, w_down)


TMC = 256


def _combine_body(dest_ref, x_ref, gate_ref, wt_ref, g_ref, sh_ref, sc_ref, y_hbm, *rest, nt, last):
    if last:
        h_ref, ybuf, sem = rest
    else:
        o_ref, h_ref, ybuf, sem = rest
    i = pl.program_id(0)

    def gather(tile, slot):
        base = tile * (2 * TMC)

        for r in range(TMC):
            _row_copy(y_hbm, dest_ref[base + 2 * r], ybuf.at[slot, 0], r, sem.at[slot]).start(priority=0)
            _row_copy(y_hbm, dest_ref[base + 2 * r + 1], ybuf.at[slot, 1], r, sem.at[slot]).start(priority=1)

    @pl.when(i == 0)
    def _():
        gather(0, 0)

    @pl.when(i + 1 < nt)
    def _():
        gather(i + 1, (i + 1) % 2)

    slot = i % 2
    for k in range(2):
        pltpu.make_async_copy(y_hbm.at[pl.ds(0, TMC)], ybuf.at[slot, k], sem.at[slot]).wait()
    wt = wt_ref[...]
    w0, w1 = wt[:, 0:1], wt[:, 1:2]
    lo0, hi0 = _unpack_rows(ybuf[slot, 0])
    lo1, hi1 = _unpack_rows(ybuf[slot, 1])
    y = jnp.concatenate([w0 * lo0 + w1 * lo1, w0 * hi0 + w1 * hi1], axis=1)
    xn = x_ref[...] + gate_ref[...] * y
    if last:
        h_ref[...] = xn * lax.rsqrt(jnp.mean(xn * xn, axis=-1, keepdims=True) + EPS) * g_ref[...]
    else:
        o_ref[...] = xn
        h_ref[...] = _norm_mod_val(xn, g_ref[...], sh_ref[...], sc_ref[...]).astype(h_ref.dtype)


def _combine(dest, xs, mod3, wt, yb, rows, g_next, mod3_next, last):
    nt = rows // TMC
    row_spec = pl.BlockSpec((TMC, D), lambda i, d: (i, 0))

    def mod_spec(which):
        return pl.BlockSpec((None, 1, D), lambda i, d: (_seg_of(i, TMC) * N_MOD + which, 0, 0))

    if last:
        out_shape = jax.ShapeDtypeStruct((rows, D), F32)
        out_specs = row_spec
    else:
        out_shape = (jax.ShapeDtypeStruct((rows, D), F32), jax.ShapeDtypeStruct((rows, D), BF16))
        out_specs = (row_spec, row_spec)
    return pl.pallas_call(
        functools.partial(_combine_body, nt=nt, last=last),
        out_shape=out_shape,
        grid_spec=pltpu.PrefetchScalarGridSpec(
            num_scalar_prefetch=1,
            grid=(nt,),
            in_specs=[
                row_spec,
                mod_spec(5),
                pl.BlockSpec((TMC, 128), lambda i, d: (i, 0)),
                pl.BlockSpec((1, D), lambda i, d: (0, 0)),
                mod_spec(0),
                mod_spec(1),
                pl.BlockSpec(memory_space=pl.ANY),
            ],
            out_specs=out_specs,
            scratch_shapes=[
                pltpu.VMEM((2, 2, TMC, HALF), jnp.uint32),
                pltpu.SemaphoreType.DMA((2,)),
            ],
        ),
        compiler_params=_cp("arbitrary"),
        name="moe_combine",
    )(dest, xs, mod3, wt, g_next.reshape(1, D), mod3_next, mod3_next, yb)


PLAN_CHUNK = 1024


def _dest_body(ids_ref, base_ref, o_ref):
    ids = ids_ref[...].astype(F32)
    lane_i = lax.broadcasted_iota(jnp.int32, ids.shape, 1)
    lane = lane_i.astype(F32)

    def pick(k):
        return jnp.sum(jnp.where(lane_i == k, ids, 0.0), axis=-1, keepdims=True)

    base = base_ref[...]
    d0 = jnp.sum(jnp.where(lane == pick(0), base[0:1, :], 0.0), axis=-1, keepdims=True) + pick(2)
    d1 = jnp.sum(jnp.where(lane == pick(1), base[1:2, :], 0.0), axis=-1, keepdims=True) + pick(3)
    o_ref[...] = jnp.where(lane_i == 0, d0, jnp.where(lane_i == 1, d1, 0.0)).astype(jnp.int32)


def _slot_body(dest_ref, tok_ref, *, nslots):
    i = pl.program_id(0)

    @pl.when(i == 0)
    def _():
        def clear(s, carry):
            tok_ref[s] = 0
            return carry

        lax.fori_loop(0, nslots, clear, 0, unroll=32)

    def place(t, carry):
        tok = i * PLAN_CHUNK + t
        tok_ref[dest_ref[0, 2 * t]] = tok
        tok_ref[dest_ref[0, 2 * t + 1]] = tok
        return carry

    lax.fori_loop(0, PLAN_CHUNK, place, 0, unroll=16)


def _route_plan(eid, cnt, rows):
    a = rows * 2
    nb = -(-a // TME) + ME
    steps = rows // PLAN_CHUNK
    c0 = cnt[0, :ME].astype(jnp.int32)
    c1 = cnt[1, :ME].astype(jnp.int32)
    padded = ((c0 + c1 + TME - 1) // TME) * TME
    pad_end = jnp.cumsum(padded)
    pad_start = pad_end - padded
    blk_start = jnp.concatenate([jnp.zeros((1,), jnp.int32), (pad_end // TME).astype(jnp.int32)])
    blk = jnp.arange(nb, dtype=jnp.int32)
    blk_e = jnp.minimum(jnp.searchsorted(blk_start[1:], blk, side='right'), ME - 1)
    blk_rows = jnp.clip((c0 + c1)[blk_e] - (blk - blk_start[blk_e]) * TME, 0, TME).astype(jnp.int32)
    base = jnp.zeros((8, 128), F32).at[0, :ME].set(pad_start.astype(F32)).at[1, :ME].set((pad_start + c0).astype(F32))
    dest_t = pl.pallas_call(
        _dest_body,
        out_shape=jax.ShapeDtypeStruct((rows, 128), jnp.int32),
        grid=(steps,),
        in_specs=[pl.BlockSpec((PLAN_CHUNK, 128), lambda i: (i, 0)), pl.BlockSpec((8, 128), lambda i: (0, 0))],
        out_specs=pl.BlockSpec((PLAN_CHUNK, 128), lambda i: (i, 0)),
        compiler_params=_cp("arbitrary"),
        name="route_dest",
    )(eid, base)
    dest = dest_t[:, :2].reshape(a)
    buf_tok = pl.pallas_call(
        functools.partial(_slot_body, nslots=nb * TME),
        out_shape=jax.ShapeDtypeStruct((nb * TME,), jnp.int32),
        grid=(steps,),
        in_specs=[pl.BlockSpec((None, 1, 2 * PLAN_CHUNK), lambda i: (i, 0, 0), memory_space=pltpu.SMEM)],
        out_specs=pl.BlockSpec(memory_space=pltpu.SMEM),
        compiler_params=_cp("arbitrary"),
        name="route_slots",
    )(dest.reshape(steps, 1, 2 * PLAN_CHUNK))
    return dest, buf_tok, blk_start, blk_rows


def _sincos_2d(rows, cols, d):
    quarter = d // 4
    omega = 1.0 / (POS_BASE ** (jnp.arange(quarter, dtype=F32) / quarter))

    def axis_emb(n):
        p = jnp.arange(n, dtype=F32)[:, None] * omega[None, :]
        return jnp.concatenate([jnp.sin(p), jnp.cos(p)], axis=-1)

    er, ec = axis_emb(rows), axis_emb(cols)
    half = 2 * quarter
    pos = jnp.concatenate([jnp.broadcast_to(er[:, None, :], (rows, cols, half)),
                           jnp.broadcast_to(ec[None, :, :], (rows, cols, half))], axis=-1)
    return pos.reshape(rows * cols, 2 * half)


def _gla_weights(w_in, wg_f, wg_b):
    w_cat = jnp.concatenate([w_in[:, :QK + VV], w_in[:, STATE_COLS:]], axis=1).astype(BF16)
    w_z = jnp.concatenate([w_in[:, QK + VV:STATE_COLS], jnp.zeros((D, ZPAD - 2 * RANK), F32)], axis=1).astype(BF16)
    wgf = jnp.zeros((128, QK), F32).at[:RANK].set(wg_f)
    wgb = jnp.zeros((128, QK), F32).at[RANK:2 * RANK].set(wg_b)
    return w_cat, w_z, wgf, wgb


def kernel(x, c, ctx, c_ctx, ada_w, ada_b, norm1_g, norm2_g, gla_w_in, gla_wg_f, gla_bg_f, gla_wg_b, gla_bg_b,
           gla_onorm_g, gla_w_out, fnet_w_in, fnet_w_out, moe_rw_group, moe_rb_group, moe_rw_expert,
           moe_rb_expert, moe_w_gate, moe_w_up, moe_w_down, final_g):
    cvec = jnp.zeros((SEGS, D), F32).at[:B].set(c).at[CTX_SEG].set(c_ctx)
    mods = _mod_all(cvec, ada_w, ada_b)
    pos = _sincos_2d(SEQ // GRID_W, GRID_W, D)
    mod3s = [mods[i].reshape(SEGS * N_MOD, 1, D) for i in range(DEPTH)]
    xs, h = _assemble(x.reshape(LAT, D), pos, ctx.reshape(NCTX, D), norm1_g[0], mod3s[0])

    cc, sc = _dft_mats(FD, 1.0)
    cs = jnp.asarray(np.concatenate([cc, sc], axis=1), BF16)
    ct_l, st_l = _dft_mats(SEQ, (SEQ * FD) ** -0.5)
    ct_c, st_c = _dft_mats(CTX, (CTX * FD) ** -0.5)
    ct_l, st_l = jnp.asarray(ct_l, BF16), jnp.asarray(-st_l, BF16)
    ct_c, st_c = jnp.asarray(ct_c, BF16), jnp.asarray(-st_c, BF16)
    s_zero = jnp.zeros((B, H, DV, DK), F32)
    last_reader = ((DEPTH - 1) // 2) * 2

    for i in range(DEPTH):
        kind, j = i % 2, i // 2
        ctx_live = i < last_reader
        ctx_needed = i <= last_reader
        mod3 = mod3s[i]
        rows_in = ROWS if ctx_needed else LAT
        rows_out = ROWS if ctx_live else LAT

        if kind == 0:
            w_cat, w_z, wgf, wgb = _gla_weights(gla_w_in[j], gla_wg_f[j], gla_wg_b[j])
            bgf, bgb = gla_bg_f[j].reshape(1, QK), gla_bg_b[j].reshape(1, QK)
            onorm = gla_onorm_g[j].reshape(1, DV)
            proj = _mm(h, w_cat, rows_in, 1024, 1536, BF16, "gla_proj")
            zproj = _mm(h, w_z, rows_in, 512, ZPAD, F32, "gla_gate_proj")
            o_c, sf, sb = _gla(proj, zproj, wgf, bgf, wgb, bgb, onorm, s_zero, s_zero, CTX, LAT, ctx_live)
            o_l, _, _ = _gla(proj, zproj, wgf, bgf, wgb, bgb, onorm, sf, sb, SEQ, 0, True)
            mix_l, mix_c, w_out = o_l, o_c, gla_w_out[j].astype(BF16)
        else:
            a, b = _fnet_in(h, fnet_w_in[j].astype(BF16), cs, rows_in)
            y = _fnet_time(ct_l, st_l, a, b, SEQ, 0)
            y_c = _fnet_time(ct_c, st_c, a, b, CTX, LAT) if ctx_live else y
            mix_l, mix_c, w_out = y, y_c, fnet_w_out[j].astype(BF16)

        rw = jnp.zeros((D, 128), F32).at[:, :MG].set(moe_rw_group[i])
        rw = rw.at[:, MG:MG + ME].set(jnp.transpose(moe_rw_expert[i], (1, 0, 2)).reshape(D, ME))
        rb = jnp.zeros((1, 128), F32).at[0, :MG].set(moe_rb_group[i]).at[0, MG:MG + ME].set(
            moe_rb_expert[i].reshape(ME))
        rw_hi = rw.astype(BF16)
        rw_cat = jnp.concatenate([rw_hi, (rw - rw_hi.astype(F32)).astype(BF16)], axis=1)
        xs, h2, eid, wt, cnt = _mix_route(mix_l, mix_c, w_out, xs, norm2_g[i], mod3, rw_cat, rb, rows_out)
        dest, buf_tok, blk_start, blk_rows = _route_plan(eid, cnt, rows_out)
        yb = _moe_experts(i, blk_start, blk_rows, buf_tok, h2, moe_w_gate, moe_w_up, moe_w_down)
        if i + 1 < DEPTH:
            xs, h = _combine(dest, xs, mod3, wt, yb, rows_out, norm1_g[i + 1], mod3s[i + 1], False)
        else:
            out = _combine(dest, xs, mod3, wt, yb, rows_out, final_g, mod3, True)

    return out.reshape(B, SEQ, D)
```
